```python
import math
import jax, jax.numpy as jnp
from jax import lax
import numpy as np

D_MODEL = 1024
BATCH = 4
SEQ = 4096
DEPTH = 2
DEC_BATCH = 128
DEC_SEQ = 8
PAST_LEN = 16384
PAGE_SIZE = 128

RET_HEADS = 4
RET_DK = 64
RET_DV = 128
RET_W = RET_HEADS * RET_DV
MLA_HEADS = 8
MLA_NOPE = 64
MLA_ROPE = 32
MLA_V = 64
Q_LORA = 384
KV_LORA = 256
MLA_W = MLA_HEADS * MLA_V
MLA_SCALE = (MLA_NOPE + MLA_ROPE) ** -0.5
SG_GROUPS = 4
SG_GDIM = 128
SG_CHUNK = 128
SG_W = SG_GROUPS * SG_GDIM
GD_HEADS = 4
GD_DK = 128
GD_DV = 128
GD_W = GD_HEADS * GD_DV
GD_CONV = 4
GD_CONV_CH = 2 * GD_HEADS * GD_DK + GD_W
N_BRANCH = 4
BRANCH_W = 512
LIN_CHUNK = 64
Q_BLOCK = 128
ROPE_BASE = 10000.0
EPS = 1e-6
NEG = -1e30
DN_ALPHA = (2.0 * DEPTH) ** 0.25
DN_BETA = (8.0 * DEPTH) ** -0.25
F32 = jnp.float32
IN_SPLITS = (RET_HEADS * RET_DK, RET_HEADS * RET_DK, RET_W, RET_W,
             Q_LORA, KV_LORA, MLA_ROPE, MLA_W,
             SG_W, SG_W, SG_W,
             GD_CONV_CH, GD_HEADS, GD_HEADS, GD_W,
             N_BRANCH * D_MODEL)
D_IN = sum(IN_SPLITS)

kernel_name = 'hybrid_gated_parallel_decoder_step'


def _split_cols(h):
    idx = np.cumsum(IN_SPLITS)[:-1].tolist()
    return jnp.split(h, idx, axis=-1)


def _layernorm(x, g, b):
    xf = x.astype(F32)
    mu = xf.mean(-1, keepdims=True)
    var = jnp.square(xf - mu).mean(-1, keepdims=True)
    return (xf - mu) * lax.rsqrt(var + EPS) * g + b


def _unit_norm(x):
    xf = x.astype(F32)
    mu = xf.mean(-1, keepdims=True)
    var = jnp.square(xf - mu).mean(-1, keepdims=True)
    return (xf - mu) * lax.rsqrt(var + EPS)


def _rmsnorm(x, g):
    xf = x.astype(F32)
    return xf * lax.rsqrt(jnp.square(xf).mean(-1, keepdims=True) + EPS) * g


def _l2norm(x):
    xf = x.astype(F32)
    return xf * lax.rsqrt(jnp.square(xf).sum(-1, keepdims=True) + EPS)


def _rope(x, pos):
    d = x.shape[-1]
    inv = ROPE_BASE ** (-jnp.arange(0, d, 2, dtype=F32) / d)
    ang = pos.astype(F32)[:, None] * inv[None, :]
    cos, sin = jnp.cos(ang)[:, None, :], jnp.sin(ang)[:, None, :]
    xf = x.astype(F32)
    x1, x2 = xf[..., :d // 2], xf[..., d // 2:]
    return jnp.concatenate([x1 * cos - x2 * sin, x2 * cos + x1 * sin], axis=-1)


def _retention(q, k, v, s0):
    B, T, H, dk = q.shape
    dv = v.shape[-1]
    C = math.gcd(T, LIN_CHUNK)
    N = T // C
    lg = jnp.log1p(-jnp.exp2(-5.0 - jnp.arange(H, dtype=F32)))
    qc = q.astype(F32).reshape(B, N, C, H, dk)
    kc = k.astype(F32).reshape(B, N, C, H, dk)
    vc = v.astype(F32).reshape(B, N, C, H, dv)
    idx = jnp.arange(C, dtype=F32)
    rel = idx[:, None] - idx[None, :]
    lower = rel >= 0
    dmat = jnp.where(lower, jnp.exp(lg[:, None, None] * jnp.where(lower, rel, 0.0)), 0.0)
    scores = jnp.einsum('bnihd,bnjhd->bnhij', qc, kc) * dmat
    o_intra = jnp.einsum('bnhij,bnjhe->bnihe', scores, vc)
    q_dec = jnp.exp(lg[:, None] * (idx + 1.0))[None] [0]
    k_dec = jnp.exp(lg[:, None] * (C - 1.0 - idx))
    kv = jnp.einsum('bnjhd,hj,bnjhe->bnhde', kc, k_dec, vc)
    c_dec = jnp.exp(lg * C)

    def step(s, kv_n):
        return s * c_dec[:, None, None] + kv_n, s

    s_fin, s_prev = lax.scan(step, s0.astype(F32), jnp.moveaxis(kv, 1, 0))
    o_inter = jnp.einsum('bnihd,hi,nbhde->bnihe', qc, q_dec, s_prev)
    return (o_intra + o_inter).reshape(B, T, H, dv), s_fin


def _gated_delta(q, k, v, g, beta, s0):
    B, T, H, dk = q.shape
    dv = v.shape[-1]
    C = math.gcd(T, LIN_CHUNK)
    N = T // C

    def blk(a):
        return a.astype(F32).reshape((B, N, C, H) + a.shape[3:]).swapaxes(2, 3)

    qc, kc, vc, gc, bc = blk(q), blk(k), blk(v), blk(g), blk(beta)
    gcum = jnp.cumsum(gc, axis=-1)
    lower = jnp.tril(jnp.ones((C, C), bool))
    strict = jnp.tril(jnp.ones((C, C), bool), -1)
    diff = gcum[..., :, None] - gcum[..., None, :]
    gam = jnp.where(lower, jnp.exp(jnp.where(lower, diff, 0.0)), 0.0)
    kk = jnp.einsum('bnhid,bnhjd->bnhij', kc, kc)
    a_mat = jnp.eye(C, dtype=F32) + jnp.where(strict, bc[..., :, None] * kk * gam, 0.0)
    rhs = jnp.concatenate([vc * bc[..., None], kc * (bc * jnp.exp(gcum))[..., None]], axis=-1)
    sol = lax.linalg.triangular_solve(a_mat, rhs, left_side=True, lower=True, unit_diagonal=True)
    u, w = sol[..., :dv], sol[..., dv:]
    qk = jnp.einsum('bnhid,bnhjd->bnhij', qc, kc) * gam
    q_dec = qc * jnp.exp(gcum)[..., None]
    k_dec = kc * jnp.exp(gcum[..., -1:] - gcum)[..., None]
    last = jnp.exp(gcum[..., -1])

    def step(s, xs):
        u_n, w_n, qk_n, qd_n, kd_n, l_n = xs
        v_new = u_n - jnp.einsum('bhcd,bhde->bhce', w_n, s)
        o_n = jnp.einsum('bhcd,bhde->bhce', qd_n, s) + jnp.einsum('bhij,bhje->bhie', qk_n, v_new)
        s = s * l_n[..., None, None] + jnp.einsum('bhcd,bhce->bhde', kd_n, v_new)
        return s, o_n

    xs = tuple(jnp.moveaxis(a, 1, 0) for a in (u, w, qk, q_dec, k_dec, last))
    s_fin, o = lax.scan(step, s0.astype(F32), xs)
    o = jnp.moveaxis(o, 0, 1).swapaxes(2, 3).reshape(B, T, H, dv)
    return o, s_fin


def _short_conv(x, buf, w):
    T = x.shape[1]
    xp = jnp.concatenate([buf.astype(F32), x.astype(F32)], axis=1)
    y = xp[:, 0:T] * w[0]
    for j in range(1, GD_CONV):
        y = y + xp[:, j:j + T] * w[j]
    return jax.nn.silu(y), xp[:, xp.shape[1] - (GD_CONV - 1):]


def _spatial_gate(u, v, w_s, b_s):
    B, T, G, dg = v.shape
    C = min(T, SG_CHUNK)
    N = T // C
    w = jnp.tril(w_s[:, :C, :C])
    s = jnp.einsum('gij,bnjgd->bnigd', w, v.reshape(B, N, C, G, dg)) + b_s[:, :C].T[:, :, None]
    return u * s.reshape(B, T, G, dg)


def _mla_core(q_lat, q_pe, ckv, kpe, mask):
    s = (jnp.einsum('qhc,kc->hqk', q_lat, ckv) + jnp.einsum('qhr,kr->hqk', q_pe, kpe)).astype(F32) * MLA_SCALE
    p = jax.nn.softmax(jnp.where(mask[None], s, NEG), axis=-1)
    return jnp.einsum('hqk,kc->qhc', p.astype(ckv.dtype), ckv)


def _mla_prompt(q_lat, q_pe, ckv, kpe):
    B, T = q_lat.shape[:2]
    qb = math.gcd(T, Q_BLOCK)
    kpos = jnp.arange(T)
    core = jax.vmap(_mla_core, in_axes=(0, 0, 0, 0, None))

    def blk(i):
        start = i * qb
        ql = lax.dynamic_slice_in_dim(q_lat, start, qb, axis=1)
        qp = lax.dynamic_slice_in_dim(q_pe, start, qb, axis=1)
        mask = kpos[None, :] <= (start + jnp.arange(qb))[:, None]
        return core(ql, qp, ckv, kpe, mask)

    o = lax.map(blk, jnp.arange(T // qb))
    return jnp.moveaxis(o, 0, 1).reshape(B, T, MLA_HEADS, KV_LORA)


def _mla_sample(q_lat, q_pe, ckv, kpe, cache_ckv, cache_kpe, page_table, layer):
    T = q_lat.shape[1]
    past = page_table.shape[1] * cache_ckv.shape[2]
    mask = jnp.concatenate([jnp.ones((T, past), bool), jnp.tril(jnp.ones((T, T), bool))], axis=1)

    def one(args):
        ql, qp, cn, kn, pt = args
        ck = jnp.concatenate([cache_ckv[layer, pt].reshape(past, KV_LORA).astype(F32), cn.astype(F32)], axis=0)
        kp = jnp.concatenate([cache_kpe[layer, pt].reshape(past, MLA_ROPE).astype(F32), kn.astype(F32)], axis=0)
        return _mla_core(ql, qp, ck, kp, mask)

    return lax.map(one, (q_lat, q_pe, ckv, kpe, page_table))


def _layer(x, pos, ret_s0, gd_s0, conv_buf, mla_past, w):
    (w_in, ret_gn_g, mla_q_norm, mla_w_uq, mla_kv_norm, mla_w_uk, mla_w_uv,
     sg_ln_g, sg_ln_b, sg_w, sg_b, gd_conv_w, gd_a_log, gd_dt_bias, gd_norm_g,
     w_branch, w_out, ln_g, ln_b) = w
    B, T, _ = x.shape
    h = x @ w_in
    (r_q, r_k, r_v, r_g, m_cq, m_ckv, m_kpe, m_g, s_u, s_v, s_g,
     d_qkv, d_a, d_b, d_g, m_merge) = _split_cols(h)

    q = _rope(r_q.reshape(B, T, RET_HEADS, RET_DK), pos)
    k = _rope(r_k.reshape(B, T, RET_HEADS, RET_DK), pos) * RET_DK ** -0.5
    o_a, ret_s = _retention(q, k, r_v.reshape(B, T, RET_HEADS, RET_DV), ret_s0)
    y_a = jax.nn.silu(r_g) * (_unit_norm(o_a).reshape(B, T, RET_W) * ret_gn_g)

    cq = _rmsnorm(m_cq, mla_q_norm)
    qf = (cq @ mla_w_uq).reshape(B, T, MLA_HEADS, MLA_NOPE + MLA_ROPE)
    q_nope = qf[..., :MLA_NOPE]
    q_pe = _rope(qf[..., MLA_NOPE:], pos)
    ckv = _rmsnorm(m_ckv, mla_kv_norm)
    kpe = _rope(m_kpe[:, :, None, :], pos)[:, :, 0, :]
    q_lat = jnp.einsum('bthd,chd->bthc', q_nope, mla_w_uk)
    if mla_past is None:
        o_lat = _mla_prompt(q_lat, q_pe, ckv, kpe)
    else:
        o_lat = _mla_sample(q_lat, q_pe, ckv, kpe, *mla_past)
    y_b = jax.nn.silu(m_g) * jnp.einsum('bthc,chd->bthd', o_lat, mla_w_uv).reshape(B, T, MLA_W)

    u = jax.nn.gelu(s_u, approximate=False)
    v = _layernorm(jax.nn.gelu(s_v, approximate=False), sg_ln_g, sg_ln_b)
    sgo = _spatial_gate(u.reshape(B, T, SG_GROUPS, SG_GDIM), v.reshape(B, T, SG_GROUPS, SG_GDIM), sg_w, sg_b)
    y_c = jax.nn.silu(s_g) * sgo.reshape(B, T, SG_W)

    qkv, conv_new = _short_conv(d_qkv, conv_buf, gd_conv_w)
    dq, dk, dv = jnp.split(qkv, [GD_HEADS * GD_DK, 2 * GD_HEADS * GD_DK], axis=-1)
    dq = _l2norm(dq.reshape(B, T, GD_HEADS, GD_DK)) * GD_DK ** -0.5
    dk = _l2norm(dk.reshape(B, T, GD_HEADS, GD_DK))
    g = -jnp.exp(gd_a_log) * jax.nn.softplus(d_a.astype(F32) + gd_dt_bias)
    beta = jax.nn.sigmoid(d_b.astype(F32))
    o_d, gd_s = _gated_delta(dq, dk, dv.reshape(B, T, GD_HEADS, GD_DV), g, beta, gd_s0)
    y_d = jax.nn.silu(d_g) * _rmsnorm(o_d, gd_norm_g).reshape(B, T, GD_W)

    gates = jax.nn.sigmoid(m_merge.astype(F32)).reshape(B, T, N_BRANCH, D_MODEL)
    merged = gates[:, :, 0] * (y_a @ w_branch[0])
    for i, y in ((1, y_b), (2, y_c), (3, y_d)):
        merged = merged + gates[:, :, i] * (y @ w_branch[i])
    out = merged @ w_out
    x_new = _layernorm(DN_ALPHA * x.astype(F32) + out, ln_g, ln_b).astype(x.dtype)
    return x_new, ckv, kpe, ret_s, gd_s, conv_new, v


def setup_inputs(seed: int = 0) -> dict:
    key = jax.random.key(seed)
    ks = iter(jax.random.split(key, 32))

    def nrm(shape, s):
        return jax.random.normal(next(ks), shape, F32) * s

    n_pages = PAST_LEN // PAGE_SIZE
    n_pool = (DEC_BATCH * n_pages * 5) // 4
    x_prompt = nrm((BATCH, SEQ, D_MODEL), 1.0)
    x_sample = nrm((DEC_BATCH, DEC_SEQ, D_MODEL), 1.0)
    cache_ckv = nrm((DEPTH, n_pool, PAGE_SIZE, KV_LORA), 1.0)
    cache_kpe = nrm((DEPTH, n_pool, PAGE_SIZE, MLA_ROPE), 1.0)
    state_ret = nrm((DEPTH, DEC_BATCH, RET_HEADS, RET_DK, RET_DV), 1.0)
    state_delta = nrm((DEPTH, DEC_BATCH, GD_HEADS, GD_DK, GD_DV), 0.3)
    state_conv = nrm((DEPTH, DEC_BATCH, GD_CONV - 1, GD_CONV_CH), 1.0)
    page_table = jax.random.permutation(next(ks), n_pool)[:DEC_BATCH * n_pages].reshape(DEC_BATCH, n_pages).astype(jnp.int32)
    dt = jnp.exp(jax.random.uniform(next(ks), (DEPTH, GD_HEADS), F32, math.log(1e-3), math.log(1e-1)))
    gd_dt_bias = dt + jnp.log(-jnp.expm1(-dt))
    gd_a_log = jnp.log(jax.random.uniform(next(ks), (DEPTH, GD_HEADS), F32, 1.0, 16.0))
    return {
        'x_prompt': x_prompt,
        'x_sample': x_sample,
        'cache_ckv': cache_ckv,
        'cache_kpe': cache_kpe,
        'state_ret': state_ret,
        'state_delta': state_delta,
        'state_conv': state_conv,
        'page_table': page_table,
        'w_in': nrm((DEPTH, D_MODEL, D_IN), D_MODEL ** -0.5),
        'ret_gn_g': 1.0 + nrm((DEPTH, RET_W), 0.02),
        'mla_q_norm': 1.0 + nrm((DEPTH, Q_LORA), 0.02),
        'mla_w_uq': nrm((DEPTH, Q_LORA, MLA_HEADS * (MLA_NOPE + MLA_ROPE)), Q_LORA ** -0.5),
        'mla_kv_norm': 1.0 + nrm((DEPTH, KV_LORA), 0.02),
        'mla_w_uk': nrm((DEPTH, KV_LORA, MLA_HEADS, MLA_NOPE), KV_LORA ** -0.5),
        'mla_w_uv': nrm((DEPTH, KV_LORA, MLA_HEADS, MLA_V), KV_LORA ** -0.5),
        'sg_ln_g': 1.0 + nrm((DEPTH, SG_W), 0.02),
        'sg_ln_b': nrm((DEPTH, SG_W), 0.02),
        'sg_w': nrm((DEPTH, SG_GROUPS, SG_CHUNK, SG_CHUNK), SG_CHUNK ** -0.5),
        'sg_b': 1.0 + nrm((DEPTH, SG_GROUPS, SG_CHUNK), 0.02),
        'gd_conv_w': nrm((DEPTH, GD_CONV, GD_CONV_CH), GD_CONV ** -0.5),
        'gd_a_log': gd_a_log,
        'gd_dt_bias': gd_dt_bias,
        'gd_norm_g': 1.0 + nrm((DEPTH, GD_DV), 0.02),
        'w_branch': nrm((DEPTH, N_BRANCH, BRANCH_W, D_MODEL), BRANCH_W ** -0.5 * DN_BETA),
        'w_out': nrm((DEPTH, D_MODEL, D_MODEL), D_MODEL ** -0.5 * DN_BETA),
        'ln_g': 1.0 + nrm((DEPTH, D_MODEL), 0.02),
        'ln_b': nrm((DEPTH, D_MODEL), 0.02),
    }


def reference(x_prompt, x_sample, cache_ckv, cache_kpe, state_ret, state_delta, state_conv, page_table,
              w_in, ret_gn_g, mla_q_norm, mla_w_uq, mla_kv_norm, mla_w_uk, mla_w_uv,
              sg_ln_g, sg_ln_b, sg_w, sg_b, gd_conv_w, gd_a_log, gd_dt_bias, gd_norm_g,
              w_branch, w_out, ln_g, ln_b):
    bp, tp, _ = x_prompt.shape
    bs, ts, _ = x_sample.shape
    past_len = page_table.shape[1] * cache_ckv.shape[2]
    pos_p = jnp.arange(tp)
    pos_s = past_len + jnp.arange(ts)
    weights = (w_in, ret_gn_g, mla_q_norm, mla_w_uq, mla_kv_norm, mla_w_uk, mla_w_uv,
               sg_ln_g, sg_ln_b, sg_w, sg_b, gd_conv_w, gd_a_log, gd_dt_bias, gd_norm_g,
               w_branch, w_out, ln_g, ln_b)
    xp, xs = x_prompt, x_sample
    p_ckv, p_kpe, p_ret, p_delta, p_conv = [], [], [], [], []
    s_ckv, s_kpe, s_ret, s_delta, s_conv, s_sgv = [], [], [], [], [], []
    for l in range(DEPTH):
        wl = tuple(a[l] for a in weights)
        xp, ckv, kpe, rs, gs, cb, _ = _layer(
            xp, pos_p,
            jnp.zeros((bp, RET_HEADS, RET_DK, RET_DV), F32),
            jnp.zeros((bp, GD_HEADS, GD_DK, GD_DV), F32),
            jnp.zeros((bp, GD_CONV - 1, GD_CONV_CH), F32),
            None, wl)
        p_ckv.append(ckv); p_kpe.append(kpe); p_ret.append(rs); p_delta.append(gs); p_conv.append(cb)
        xs, ckv, kpe, rs, gs, cb, sgv = _layer(
            xs, pos_s, state_ret[l], state_delta[l], state_conv[l],
            (cache_ckv, cache_kpe, page_table, l), wl)
        s_ckv.append(ckv); s_kpe.append(kpe); s_ret.append(rs); s_delta.append(gs); s_conv.append(cb); s_sgv.append(sgv)
    return (xp, xs,
            jnp.stack(p_ckv), jnp.stack(p_kpe), jnp.stack(p_ret), jnp.stack(p_delta), jnp.stack(p_conv),
            jnp.stack(s_ckv), jnp.stack(s_kpe), jnp.stack(s_ret), jnp.stack(s_delta), jnp.stack(s_conv), jnp.stack(s_sgv))
```

```python
import functools
import math

import numpy as np
import jax
import jax.numpy as jnp
from jax import lax
from jax.experimental import pallas as pl
from jax.experimental.pallas import tpu as pltpu

F32 = jnp.float32
BF16 = jnp.bfloat16

D_MODEL = 1024
RET_HEADS, RET_DK, RET_DV = 4, 64, 128
RET_W = RET_HEADS * RET_DV
MLA_HEADS, MLA_NOPE, MLA_ROPE, MLA_V = 8, 64, 32, 64
Q_LORA, KV_LORA = 384, 256
MLA_W = MLA_HEADS * MLA_V
MLA_SCALE = (MLA_NOPE + MLA_ROPE) ** -0.5
SG_GROUPS, SG_GDIM, SG_CHUNK = 4, 128, 128
SG_W = SG_GROUPS * SG_GDIM
GD_HEADS, GD_DK, GD_DV, GD_CONV = 4, 128, 128, 4
GD_W = GD_HEADS * GD_DV
GD_CONV_CH = 2 * GD_HEADS * GD_DK + GD_W
N_BRANCH, BRANCH_W = 4, 512
LIN_CHUNK = 64
ROPE_BASE = 10000.0
EPS = 1e-6
NEG = -1e30
IN_SPLITS = (RET_HEADS * RET_DK, RET_HEADS * RET_DK, RET_W, RET_W,
             Q_LORA, KV_LORA, MLA_ROPE, MLA_W,
             SG_W, SG_W, SG_W,
             GD_CONV_CH, GD_HEADS, GD_HEADS, GD_W,
             N_BRANCH * D_MODEL)
_OFF = np.concatenate([[0], np.cumsum(IN_SPLITS)]).tolist()

LANES = 128
KCAT_W = KV_LORA + LANES
VMEM_LIMIT = 56 * 1024 * 1024

_TB = (((1,), (1,)), ((), ()))
_TA = (((0,), (0,)), ((), ()))


def _dot(a, b):
    return jnp.dot(a, b, preferred_element_type=F32)


def _dot_tb(a, b):
    return lax.dot_general(a, b, _TB, preferred_element_type=F32)


def _dot_ta(a, b):
    return lax.dot_general(a, b, _TA, preferred_element_type=F32)


def _split2(a):
    hi = a.astype(BF16)
    lo = (a - hi.astype(F32)).astype(BF16)
    return hi, lo


def _dot3(a, b):
    ah, al = _split2(a)
    bh, bl = _split2(b)
    return _dot(ah, bh) + (_dot(ah, bl) + _dot(al, bh))


def _dot_mask(mask_bf16, x):
    hi = x.astype(BF16)
    r1 = x - hi.astype(F32)
    mid = r1.astype(BF16)
    lo = (r1 - mid.astype(F32)).astype(BF16)
    return _dot(mask_bf16, hi) + (_dot(mask_bf16, mid) + _dot(mask_bf16, lo))


def _silu(x):
    return x * jax.nn.sigmoid(x)


def _gelu(x):
    return 0.5 * x * (1.0 + lax.erf(x * np.float32(math.sqrt(0.5))))


def _rope(x, cos, sin_signed, d):
    w = x.shape[-1]
    half = d // 2
    lane = lax.broadcasted_iota(jnp.int32, x.shape, 1)
    first = (lane % d) < half
    rot = jnp.where(first, pltpu.roll(x, w - half, 1), pltpu.roll(x, half, 1))
    return x * cos + rot * sin_signed


def _const_spec(shape):
    nd = len(shape)
    return pl.BlockSpec(shape, lambda *_: (0,) * nd, pipeline_mode=pl.Buffered(1))


def _params(sem):
    return pltpu.CompilerParams(dimension_semantics=sem, vmem_limit_bytes=VMEM_LIMIT)


def _ret_kernel(*refs, R, NS, RB, cdec, has_s0):
    if has_s0:
        (x_ref, w_ref, cos_ref, sin_ref, gn_ref, dmat_ref, qdec_ref, kdec_ref, s0_ref,
         y_ref, sfin_ref, q_s, k_s, v_s, g_s) = refs
    else:
        (x_ref, w_ref, cos_ref, sin_ref, gn_ref, dmat_ref, qdec_ref, kdec_ref,
         y_ref, sfin_ref, q_s, k_s, v_s, g_s) = refs
    C = R // NS
    h = _dot(x_ref[...].astype(BF16), w_ref[...])
    cos = cos_ref[...]
    sin = sin_ref[...]
    hq = RET_HEADS * RET_DK
    q_s[...] = _rope(h[:, 0:hq], cos, sin, RET_DK)
    k_s[...] = _rope(h[:, hq:2 * hq], cos, sin, RET_DK) * (RET_DK ** -0.5)
    v_s[...] = h[:, 2 * hq:2 * hq + RET_W]
    g_s[...] = _silu(h[:, 2 * hq + RET_W:2 * hq + 2 * RET_W])

    if has_s0:
        sfin_ref[...] = s0_ref[...]
    else:
        @pl.when(pl.program_id(1) == 0)
        def _():
            sfin_ref[...] = jnp.zeros(sfin_ref.shape, F32)

    def chunk(c, carry):
        r0 = pl.multiple_of(c * R, R)
        qc = q_s[pl.ds(r0, R), :]
        kc = k_s[pl.ds(r0, R), :]
        vc = v_s[pl.ds(r0, R), :]
        gc = g_s[pl.ds(r0, R), :]
        for hh in range(RET_HEADS):
            qh = qc[:, hh * RET_DK:(hh + 1) * RET_DK].astype(BF16)
            kh = kc[:, hh * RET_DK:(hh + 1) * RET_DK].astype(BF16)
            vh = vc[:, hh * RET_DV:(hh + 1) * RET_DV]
            p = (_dot_tb(qh, kh) * dmat_ref[hh]).astype(BF16)
            o = _dot(p, vh.astype(BF16))
            vk = (vh * kdec_ref[hh]).astype(BF16)
            qd = qdec_ref[hh]
            parts = []
            for s in range(NS):
                sidx = c * NS + s if has_s0 else 0
                st = sfin_ref[sidx, hh]
                parts.append(_dot(qh[s * C:(s + 1) * C], st.astype(BF16)) * qd[s * C:(s + 1) * C])
                kv = _dot_ta(kh[s * C:(s + 1) * C], vk[s * C:(s + 1) * C])
                sfin_ref[sidx, hh] = st * cdec[hh] + kv
            o = o + (parts[0] if NS == 1 else jnp.concatenate(parts, axis=0))
            mu = jnp.mean(o, axis=-1, keepdims=True)
            d = o - mu
            var = jnp.mean(d * d, axis=-1, keepdims=True)
            on = d * lax.rsqrt(var + EPS) * gn_ref[:, hh * RET_DV:(hh + 1) * RET_DV]
            y_ref[pl.ds(r0, R), hh * RET_DV:(hh + 1) * RET_DV] = (
                gc[:, hh * RET_DV:(hh + 1) * RET_DV] * on).astype(y_ref.dtype)
        return carry

    lax.fori_loop(0, RB // R, chunk, 0)


def _ret_consts(R, NS):
    C = R // NS
    lg = np.log1p(-np.exp2(-5.0 - np.arange(RET_HEADS, dtype=np.float64)))
    idx = np.arange(R)
    tok = idx % C
    same = (idx[:, None] // C) == (idx[None, :] // C)
    rel = tok[:, None] - tok[None, :]
    low = same & (rel >= 0)
    dmat = np.where(low[None], np.exp(lg[:, None, None] * np.where(low, rel, 0)[None]), 0.0)
    qdec = np.exp(lg[:, None] * (tok[None, :] + 1.0))
    kdec = np.exp(lg[:, None] * (C - 1.0 - tok[None, :]))
    cdec = tuple(float(v) for v in np.exp(lg * C))
    bc = lambda a: jnp.asarray(np.broadcast_to(a[:, :, None], (RET_HEADS, R, LANES)), F32)
    return jnp.asarray(dmat, F32), bc(qdec), bc(kdec), cdec


def _ret_call(x, w, cos, sin, gn, s0, *, B, T, out_dtype):
    N = x.shape[0]
    has_s0 = s0 is not None
    R = LIN_CHUNK
    if has_s0:
        C = math.gcd(T, LIN_CHUNK)
        assert C == T, "sample stream is a single chunk per sequence"
        NS = R // C
        RB = 2 * R
        nb, nt = N // RB, 1
        nsb = RB // C
    else:
        NS, RB = 1, 512
        assert T % RB == 0 and math.gcd(T, LIN_CHUNK) == R
        nb, nt = B, T // RB
        nsb = 1
    dmat, qdec, kdec, cdec = _ret_consts(R, NS)
    tr = cos.shape[0] // RB
    hq = RET_HEADS * RET_DK
    in_specs = [
        pl.BlockSpec((RB, D_MODEL), lambda b, t: (b * nt + t, 0)),
        _const_spec(w.shape),
        pl.BlockSpec((RB, hq), lambda b, t: ((b * nt + t) % tr, 0)),
        pl.BlockSpec((RB, hq), lambda b, t: ((b * nt + t) % tr, 0)),
        _const_spec(gn.shape), _const_spec(dmat.shape), _const_spec(qdec.shape), _const_spec(kdec.shape),
    ]
    args = [x, w, cos, sin, gn, dmat, qdec, kdec]
    st_spec = pl.BlockSpec((nsb, RET_HEADS, RET_DK, RET_DV), lambda b, t: (b, 0, 0, 0))
    if has_s0:
        in_specs.append(st_spec)
        args.append(s0)
    nseq = N // T
    return pl.pallas_call(
        functools.partial(_ret_kernel, R=R, NS=NS, RB=RB, cdec=cdec, has_s0=has_s0),
        grid=(nb, nt),
        in_specs=in_specs,
        out_specs=[pl.BlockSpec((RB, RET_W), lambda b, t: (b * nt + t, 0)), st_spec],
        out_shape=[jax.ShapeDtypeStruct((N, RET_W), out_dtype),
                   jax.ShapeDtypeStruct((nseq, RET_HEADS, RET_DK, RET_DV), F32)],
        scratch_shapes=[pltpu.VMEM((RB, hq), F32), pltpu.VMEM((RB, hq), F32),
                        pltpu.VMEM((RB, RET_W), F32), pltpu.VMEM((RB, RET_W), F32)],
        compiler_params=_params(("arbitrary", "arbitrary")),
        name="ret",
    )(*args)


def _mlap_kernel(x_ref, w_ref, wuq_ref, qn_ref, kvn_ref, cos_ref, sin_ref,
                 qnope_ref, qpe_ref, kcat_ref, ckv_ref, kpe_ref, gm_ref):
    h = _dot(x_ref[...].astype(BF16), w_ref[...])
    cq = h[:, 0:Q_LORA]
    cq = cq * lax.rsqrt(jnp.mean(cq * cq, axis=-1, keepdims=True) + EPS) * qn_ref[...]
    qf = _dot(cq.astype(BF16), wuq_ref[...])
    nw = MLA_HEADS * MLA_NOPE
    qnope_ref[...] = (qf[:, 0:nw] * MLA_SCALE).astype(qnope_ref.dtype)
    cos = cos_ref[...]
    sin = sin_ref[...]
    qpe_ref[...] = _rope(qf[:, nw:nw + MLA_HEADS * MLA_ROPE], cos, sin, MLA_ROPE) * MLA_SCALE
    c0 = Q_LORA
    kv = h[:, c0:c0 + KV_LORA]
    ckv = kv * lax.rsqrt(jnp.mean(kv * kv, axis=-1, keepdims=True) + EPS) * kvn_ref[...]
    ckv_ref[...] = ckv
    g0 = c0 + KV_LORA
    gm_ref[...] = _silu(h[:, g0:g0 + MLA_W]).astype(gm_ref.dtype)
    p0 = g0 + MLA_W
    kblk = _rope(h[:, p0:p0 + LANES], cos[:, 0:LANES], sin[:, 0:LANES], MLA_ROPE)
    kpe_ref[...] = kblk[:, 0:MLA_ROPE]
    rep = kblk
    for i in range(1, LANES // MLA_ROPE):
        rep = rep + pltpu.roll(kblk, i * MLA_ROPE, 1)
    kcat_ref[:, 0:KV_LORA] = ckv.astype(BF16)
    kcat_ref[:, KV_LORA:KCAT_W] = rep.astype(BF16)


def _mlap_call(x, w, wuq, qn, kvn, cos, sin, *, act_dtype):
    N = x.shape[0]
    TM = 512 if N % 512 == 0 else N
    tr = cos.shape[0] // TM
    row = lambda wd: pl.BlockSpec((TM, wd), lambda i: (i, 0))
    tab = pl.BlockSpec((TM, MLA_HEADS * MLA_ROPE), lambda i: (i % tr, 0))
    nw = MLA_HEADS * MLA_NOPE
    return pl.pallas_call(
        _mlap_kernel,
        grid=(N // TM,),
        in_specs=[row(D_MODEL), _const_spec(w.shape), _const_spec(wuq.shape), _const_spec(qn.shape),
                  _const_spec(kvn.shape), tab, tab],
        out_specs=[row(nw), row(MLA_HEADS * MLA_ROPE), row(KCAT_W), row(KV_LORA), row(MLA_ROPE), row(MLA_W)],
        out_shape=[jax.ShapeDtypeStruct((N, nw), act_dtype),
                   jax.ShapeDtypeStruct((N, MLA_HEADS * MLA_ROPE), F32),
                   jax.ShapeDtypeStruct((N, KCAT_W), BF16),
                   jax.ShapeDtypeStruct((N, KV_LORA), F32),
                   jax.ShapeDtypeStruct((N, MLA_ROPE), F32),
                   jax.ShapeDtypeStruct((N, MLA_W), act_dtype)],
        compiler_params=_params(("arbitrary",)),
        name="mlap",
    )(x, w, wuq, qn, kvn, cos, sin)


def _attn_prefill_kernel(qn_ref, qpe_ref, kcat_ref, wuk_ref, gm_ref, wuv_ref, y_ref,
                         q_s, m_s, l_s, acc_s, *, BQ, TK):
    i = pl.program_id(1)
    H = MLA_HEADS
    lane = lax.broadcasted_iota(jnp.int32, (BQ, LANES), 1)
    for hh in range(H):
        ql = _dot(qn_ref[:, hh * MLA_NOPE:(hh + 1) * MLA_NOPE], wuk_ref[hh])
        q_s[hh * BQ:(hh + 1) * BQ, 0:KV_LORA] = ql.astype(BF16)
        per = LANES // MLA_ROPE
        blk = qpe_ref[:, (hh // per) * LANES:(hh // per + 1) * LANES]
        off = (hh % per) * MLA_ROPE
        keep = (lane >= off) & (lane < off + MLA_ROPE)
        q_s[hh * BQ:(hh + 1) * BQ, KV_LORA:KCAT_W] = jnp.where(keep, blk, 0.0).astype(BF16)
    m_s[...] = jnp.full(m_s.shape, NEG, F32)
    l_s[...] = jnp.zeros(l_s.shape, F32)
    acc_s[...] = jnp.zeros(acc_s.shape, F32)

    def step(j, masked):
        k0 = pl.multiple_of(j * TK, TK)
        kb = kcat_ref[pl.ds(k0, TK), :]
        s = _dot_tb(q_s[...], kb)
        if masked:
            row = lax.broadcasted_iota(jnp.int32, s.shape, 0)
            col = lax.broadcasted_iota(jnp.int32, s.shape, 1)
            s = jnp.where(k0 + col <= i * BQ + (row % BQ), s, NEG)
        m_prev = m_s[...]
        m_new = jnp.maximum(m_prev, jnp.max(s, axis=-1, keepdims=True))
        alpha = jnp.exp(m_prev - m_new)
        p = jnp.exp(s - m_new)
        l_s[...] = alpha * l_s[...] + jnp.sum(p, axis=-1, keepdims=True)
        acc_s[...] = alpha * acc_s[...] + _dot(p.astype(BF16), kb[:, 0:KV_LORA])
        m_s[...] = m_new

    last = (i * BQ + BQ - 1) // TK

    def body(j, carry):
        step(j, False)
        return carry

    lax.fori_loop(0, last, body, 0)
    step(last, True)

    o = acc_s[...] / l_s[...]
    for p2 in range(H // 2):
        lhs = jnp.concatenate([o[(2 * p2) * BQ:(2 * p2 + 1) * BQ], o[(2 * p2 + 1) * BQ:(2 * p2 + 2) * BQ]], axis=1)
        yb = _dot(lhs.astype(BF16), wuv_ref[p2])
        y_ref[:, p2 * LANES:(p2 + 1) * LANES] = (gm_ref[:, p2 * LANES:(p2 + 1) * LANES].astype(F32) * yb).astype(y_ref.dtype)


def _attn_prefill_call(qn, qpe, kcat, wuk, gm, wuv, *, B, T):
    N = qn.shape[0]
    BQ, TK = 128, 256
    assert T % TK == 0 and T % BQ == 0
    nq = T // BQ
    H = MLA_HEADS
    row = lambda wd: pl.BlockSpec((BQ, wd), lambda b, i: (b * nq + i, 0))
    return pl.pallas_call(
        functools.partial(_attn_prefill_kernel, BQ=BQ, TK=TK),
        grid=(B, nq),
        in_specs=[row(H * MLA_NOPE), row(H * MLA_ROPE),
                  pl.BlockSpec((T, KCAT_W), lambda b, i: (b, 0)),
                  _const_spec(wuk.shape), row(MLA_W), _const_spec(wuv.shape)],
        out_specs=row(MLA_W),
        out_shape=jax.ShapeDtypeStruct((N, MLA_W), BF16),
        scratch_shapes=[pltpu.VMEM((H * BQ, KCAT_W), BF16), pltpu.VMEM((H * BQ, 1), F32),
                        pltpu.VMEM((H * BQ, 1), F32), pltpu.VMEM((H * BQ, KV_LORA), F32)],
        compiler_params=_params(("arbitrary", "arbitrary")),
        name="attn_prefill",
    )(qn, qpe, kcat, wuk, gm, wuv)


def _attn_decode_kernel(pt_ref, qn_ref, qpe_ref, ckvn_ref, kpen_ref, gm_ref, wuk_ref, wuv_ref,
                        cckv_ref, ckpe_ref, y_ref,
                        kbuf, pbuf, sem, ql_s, qp_s, m_s, l_s, acc_s, *, T, G, NSTEP, layer, PAGE):
    b = pl.program_id(0)
    j = pl.program_id(1)
    nb = pl.num_programs(0)
    n = b * NSTEP + j
    slot = n % 2
    H = MLA_HEADS

    def copies(bb, jj, sl):
        out = []
        for g in range(G):
            page = pt_ref[bb, jj * G + g]
            out.append(pltpu.make_async_copy(cckv_ref.at[layer, page], kbuf.at[sl, g], sem.at[0, sl]))
            out.append(pltpu.make_async_copy(ckpe_ref.at[layer, page], pbuf.at[sl, g], sem.at[1, sl]))
        return out

    @pl.when(n == 0)
    def _():
        for c in copies(b, j, slot):
            c.start()

    @pl.when(j + 1 < NSTEP)
    def _():
        for c in copies(b, j + 1, 1 - slot):
            c.start()

    @pl.when((j + 1 == NSTEP) & (b + 1 < nb))
    def _():
        for c in copies(b + 1, 0, 1 - slot):
            c.start()

    @pl.when(j == 0)
    def _():
        for hh in range(H):
            ql = _dot(qn_ref[:, hh * MLA_NOPE:(hh + 1) * MLA_NOPE].astype(BF16), wuk_ref[hh])
            ql_s[hh * T:(hh + 1) * T, :] = ql
            qp_s[hh * T:(hh + 1) * T, :] = qpe_ref[:, hh * MLA_ROPE:(hh + 1) * MLA_ROPE]
        kn = ckvn_ref[...].astype(BF16)
        s = _dot_tb(ql_s[...].astype(BF16), kn) + _dot_tb(qp_s[...].astype(BF16), kpen_ref[...].astype(BF16))
        row = lax.broadcasted_iota(jnp.int32, s.shape, 0)
        col = lax.broadcasted_iota(jnp.int32, s.shape, 1)
        s = jnp.where(col <= row % T, s, NEG)
        m = jnp.max(s, axis=-1, keepdims=True)
        p = jnp.exp(s - m)
        m_s[...] = m
        l_s[...] = jnp.sum(p, axis=-1, keepdims=True)
        acc_s[...] = _dot(p.astype(BF16), kn)

    for c in copies(b, j, slot):
        c.wait()

    kb = kbuf[slot].reshape(G * PAGE, KV_LORA).astype(BF16)
    pb = pbuf[slot].reshape(G * PAGE, MLA_ROPE).astype(BF16)
    s = _dot_tb(ql_s[...].astype(BF16), kb) + _dot_tb(qp_s[...].astype(BF16), pb)
    m_prev = m_s[...]
    m_new = jnp.maximum(m_prev, jnp.max(s, axis=-1, keepdims=True))
    alpha = jnp.exp(m_prev - m_new)
    p = jnp.exp(s - m_new)
    l_s[...] = alpha * l_s[...] + jnp.sum(p, axis=-1, keepdims=True)
    acc_s[...] = alpha * acc_s[...] + _dot(p.astype(BF16), kb)
    m_s[...] = m_new

    @pl.when(j == NSTEP - 1)
    def _():
        o = acc_s[...] / l_s[...]
        for p2 in range(H // 2):
            lhs = jnp.concatenate([o[(2 * p2) * T:(2 * p2 + 1) * T], o[(2 * p2 + 1) * T:(2 * p2 + 2) * T]], axis=1)
            yb = _dot(lhs.astype(BF16), wuv_ref[p2])
            y_ref[:, p2 * LANES:(p2 + 1) * LANES] = gm_ref[:, p2 * LANES:(p2 + 1) * LANES] * yb


def _attn_decode_call(page_table, qn, qpe, ckv, kpe, gm, wuk, wuv, cache_ckv, cache_kpe, *, B, T, layer):
    N = qn.shape[0]
    n_pages = page_table.shape[1]
    PAGE = cache_ckv.shape[2]
    G = math.gcd(n_pages, 32)
    NSTEP = n_pages // G
    H = MLA_HEADS
    row = lambda wd: pl.BlockSpec((T, wd), lambda b, j, pt: (b, 0))
    cst = lambda shp: pl.BlockSpec(shp, lambda b, j, pt: (0,) * len(shp))
    grid_spec = pltpu.PrefetchScalarGridSpec(
        num_scalar_prefetch=1,
        grid=(B, NSTEP),
        in_specs=[row(H * MLA_NOPE), row(H * MLA_ROPE), row(KV_LORA), row(MLA_ROPE), row(MLA_W),
                  cst(wuk.shape), cst(wuv.shape),
                  pl.BlockSpec(memory_space=pl.ANY), pl.BlockSpec(memory_space=pl.ANY)],
        out_specs=row(MLA_W),
        scratch_shapes=[pltpu.VMEM((2, G, PAGE, KV_LORA), F32), pltpu.VMEM((2, G, PAGE, MLA_ROPE), F32),
                        pltpu.SemaphoreType.DMA((2, 2)),
                        pltpu.VMEM((H * T, KV_LORA), F32), pltpu.VMEM((H * T, MLA_ROPE), F32),
                        pltpu.VMEM((H * T, 1), F32), pltpu.VMEM((H * T, 1), F32),
                        pltpu.VMEM((H * T, KV_LORA), F32)],
    )
    return pl.pallas_call(
        functools.partial(_attn_decode_kernel, T=T, G=G, NSTEP=NSTEP, layer=layer, PAGE=PAGE),
        grid_spec=grid_spec,
        out_shape=jax.ShapeDtypeStruct((N, MLA_W), F32),
        compiler_params=_params(("arbitrary", "arbitrary")),
        name="attn_decode",
    )(page_table, qn, qpe, ckv, kpe, gm, wuk, wuv, cache_ckv, cache_kpe)


def _sg_kernel(x_ref, w_ref, lng_ref, lnb_ref, wt_ref, bs_ref, y_ref, *rest, TM, emit_v):
    h = _dot(x_ref[...].astype(BF16), w_ref[...])
    u = _gelu(h[:, 0:SG_W])
    gv = _gelu(h[:, SG_W:2 * SG_W])
    mu = jnp.mean(gv, axis=-1, keepdims=True)
    d = gv - mu
    var = jnp.mean(d * d, axis=-1, keepdims=True)
    v = d * lax.rsqrt(var + EPS) * lng_ref[...] + lnb_ref[...]
    if emit_v:
        rest[0][...] = v
    gs = _silu(h[:, 2 * SG_W:3 * SG_W])
    vb = v.astype(BF16)
    for c in range(TM // SG_CHUNK):
        rows = slice(c * SG_CHUNK, (c + 1) * SG_CHUNK)
        for g in range(SG_GROUPS):
            cols = slice(g * SG_GDIM, (g + 1) * SG_GDIM)
            s = _dot(wt_ref[g], vb[rows, cols]) + bs_ref[g]
            y_ref[rows, cols] = (gs[rows, cols] * (u[rows, cols] * s)).astype(y_ref.dtype)


def _sg_call(x, w, lng, lnb, wt, bs, *, emit_v, out_dtype):
    N = x.shape[0]
    TM = 512
    assert N % TM == 0
    row = lambda wd: pl.BlockSpec((TM, wd), lambda i: (i, 0))
    out_specs = [row(SG_W)]
    out_shape = [jax.ShapeDtypeStruct((N, SG_W), out_dtype)]
    if emit_v:
        out_specs.append(row(SG_W))
        out_shape.append(jax.ShapeDtypeStruct((N, SG_W), F32))
    return pl.pallas_call(
        functools.partial(_sg_kernel, TM=TM, emit_v=emit_v),
        grid=(N // TM,),
        in_specs=[row(D_MODEL), _const_spec(w.shape), _const_spec(lng.shape), _const_spec(lnb.shape),
                  _const_spec(wt.shape), _const_spec(bs.shape)],
        out_specs=out_specs,
        out_shape=out_shape,
        compiler_params=_params(("arbitrary",)),
        name="sg",
    )(x, w, lng, lnb, wt, bs)


def _gd_kernel(*refs, R, NS, RB, has_s0):
    if has_s0:
        (x_ref, w_ref, cw_ref, av_ref, dtb_ref, ng_ref, msk_ref, bufp_ref, s0_ref,
         y_ref, sfin_ref, conv_ref, q_s, k_s, v_s, gate_s, gb_s) = refs
    else:
        (x_ref, w_ref, cw_ref, av_ref, dtb_ref, ng_ref, msk_ref,
         y_ref, sfin_ref, conv_ref, q_s, k_s, v_s, gate_s, gb_s, carry_s) = refs
    C = R // NS
    J = max(1, int(math.ceil(math.log2(C))))
    CH = GD_CONV_CH
    h = _dot(x_ref[...].astype(BF16), w_ref[...])
    xq = h[:, 0:CH]
    y = xq * cw_ref[GD_CONV - 1:GD_CONV, :]
    if has_s0:
        x3 = xq.reshape(RB // 8, 8, CH)
        b3 = bufp_ref[...].reshape(RB // 8, 8, CH)
        tpos = lax.broadcasted_iota(jnp.int32, x3.shape, 1)
        for k in range(1, GD_CONV):
            sh = jnp.where(tpos >= k, pltpu.roll(x3, k, 1), pltpu.roll(b3, k, 1))
            y = y + sh.reshape(RB, CH) * cw_ref[GD_CONV - 1 - k:GD_CONV - k, :]
        for s in range(RB // C):
            conv_ref[s] = xq[s * C + C - (GD_CONV - 1):s * C + C, :]
    else:
        @pl.when(pl.program_id(1) == 0)
        def _():
            carry_s[...] = jnp.zeros(carry_s.shape, F32)
        cprev = carry_s[...]
        row8 = lax.broadcasted_iota(jnp.int32, (8, CH), 0)
        for k in range(1, GD_CONV):
            big = pltpu.roll(xq, k, 0)
            head = jnp.where(row8 < k, pltpu.roll(cprev, k, 0), big[0:8])
            sh = jnp.concatenate([head, big[8:]], axis=0)
            y = y + sh * cw_ref[GD_CONV - 1 - k:GD_CONV - k, :]
        carry_s[...] = xq[RB - 8:RB]
        conv_ref[0] = xq[RB - (GD_CONV - 1):RB, :]
    qkv = _silu(y)
    hw = GD_HEADS * GD_DK
    for hh in range(GD_HEADS):
        qb = qkv[:, hh * GD_DK:(hh + 1) * GD_DK]
        q_s[:, hh * GD_DK:(hh + 1) * GD_DK] = qb * lax.rsqrt(jnp.sum(qb * qb, axis=-1, keepdims=True) + EPS) * (GD_DK ** -0.5)
        kb = qkv[:, hw + hh * GD_DK:hw + (hh + 1) * GD_DK]
        k_s[:, hh * GD_DK:(hh + 1) * GD_DK] = kb * lax.rsqrt(jnp.sum(kb * kb, axis=-1, keepdims=True) + EPS)
    v_s[...] = qkv[:, 2 * hw:2 * hw + GD_W]
    gate_s[...] = _silu(h[:, CH:CH + GD_W])
    ab = h[:, CH + GD_W:CH + GD_W + LANES]
    lane = lax.broadcasted_iota(jnp.int32, ab.shape, 1)
    gb_s[...] = jnp.where(lane < GD_HEADS, av_ref[...] * jax.nn.softplus(ab + dtb_ref[...]), jax.nn.sigmoid(ab))

    if has_s0:
        sfin_ref[...] = s0_ref[...]
    else:
        @pl.when(pl.program_id(1) == 0)
        def _():
            sfin_ref[...] = jnp.zeros(sfin_ref.shape, F32)

    def chunk(c, carry):
        r0 = pl.multiple_of(c * R, R)
        qc = q_s[pl.ds(r0, R), :]
        kc = k_s[pl.ds(r0, R), :]
        vc = v_s[pl.ds(r0, R), :]
        gtc = gate_s[pl.ds(r0, R), :]
        gbc = gb_s[pl.ds(r0, R), :]
        lowf = msk_ref[0]
        low = lowf > 0.0
        strictf = msk_ref[1]
        gc4 = _dot_mask(lowf.astype(BF16), gbc)
        gl4 = _dot_mask(msk_ref[2].astype(BF16), gbc)
        ri = lax.broadcasted_iota(jnp.int32, (R, R), 0)
        ci = lax.broadcasted_iota(jnp.int32, (R, R), 1)
        eye = (ri == ci).astype(F32)
        for hh in range(GD_HEADS):
            sl = slice(hh * GD_DK, (hh + 1) * GD_DK)
            qh, kh, vh = qc[:, sl], kc[:, sl], vc[:, sl]
            gcb = jnp.broadcast_to(gc4[:, hh:hh + 1], (R, LANES))
            glb = jnp.broadcast_to(gl4[:, hh:hh + 1], (R, LANES))
            beta = jnp.broadcast_to(gbc[:, GD_HEADS + hh:GD_HEADS + hh + 1], (R, LANES))
            gi = gcb[:, 0:R]
            gam = jnp.where(low, jnp.exp(jnp.where(low, gi - gi.T, 0.0)), 0.0)
            khb = kh.astype(BF16)
            qkk = _dot_tb(jnp.concatenate([qh.astype(BF16), khb], axis=0), khb)
            qk = qkk[0:R] * gam
            nm = -(strictf * (beta[:, 0:R] * qkk[R:2 * R] * gam))
            pinv = eye + nm
            mp = nm
            for _ in range(J - 1):
                mp = _dot3(mp, mp)
                pinv = pinv + _dot3(pinv, mp)
            eg = jnp.exp(gcb)
            sol = _dot3(pinv, jnp.concatenate([vh * beta, kh * (beta * eg)], axis=1))
            u, w = sol[:, 0:GD_DV], sol[:, GD_DV:GD_DV + GD_DK]
            qd = (qh * eg).astype(BF16)
            kd = (kh * jnp.exp(glb - gcb)).astype(BF16)
            el = jnp.exp(glb)
            wb = w.astype(BF16)
            vns, ois = [], []
            for s in range(NS):
                rows = slice(s * C, (s + 1) * C)
                sidx = c * NS + s if has_s0 else 0
                st = sfin_ref[sidx, hh]
                sb = st.astype(BF16)
                vn = u[rows] - _dot(wb[rows], sb)
                ois.append(_dot(qd[rows], sb))
                sfin_ref[sidx, hh] = st * el[s * C:s * C + 1, :] + _dot_ta(kd[rows], vn.astype(BF16))
                vns.append(vn)
            vnew = vns[0] if NS == 1 else jnp.concatenate(vns, axis=0)
            o = (ois[0] if NS == 1 else jnp.concatenate(ois, axis=0)) + _dot(qk.astype(BF16), vnew.astype(BF16))
            on = o * lax.rsqrt(jnp.mean(o * o, axis=-1, keepdims=True) + EPS) * ng_ref[...]
            y_ref[pl.ds(r0, R), sl] = (gtc[:, sl] * on).astype(y_ref.dtype)
        return carry

    lax.fori_loop(0, RB // R, chunk, 0)


def _gd_masks(R, NS):
    C = R // NS
    idx = np.arange(R)
    same = (idx[:, None] // C) == (idx[None, :] // C)
    low = same & (idx[:, None] >= idx[None, :])
    strict = same & (idx[:, None] > idx[None, :])
    return jnp.asarray(np.stack([low, strict, same]).astype(np.float32))


def _gd_call(x, w, cw, av, dtb, ng, bufp, s0, *, B, T, out_dtype):
    N = x.shape[0]
    has_s0 = s0 is not None
    R = LIN_CHUNK
    CH = GD_CONV_CH
    if has_s0:
        C = math.gcd(T, LIN_CHUNK)
        assert C == T and T == 8 and T >= GD_CONV - 1
        NS = R // C
        RB = 2 * R
        nb, nt = N // RB, 1
        nsb = RB // C
    else:
        NS, RB = 1, 256
        assert T % RB == 0 and math.gcd(T, LIN_CHUNK) == R
        nb, nt = B, T // RB
        nsb = 1
    msk = _gd_masks(R, NS)
    rowspec = lambda wd: pl.BlockSpec((RB, wd), lambda b, t: (b * nt + t, 0))
    in_specs = [rowspec(D_MODEL), _const_spec(w.shape), _const_spec(cw.shape), _const_spec(av.shape),
                _const_spec(dtb.shape), _const_spec(ng.shape), _const_spec(msk.shape)]
    args = [x, w, cw, av, dtb, ng, msk]
    st_spec = pl.BlockSpec((nsb, GD_HEADS, GD_DK, GD_DV), lambda b, t: (b, 0, 0, 0))
    scratch = [pltpu.VMEM((RB, GD_W), F32)] * 4 + [pltpu.VMEM((RB, LANES), F32)]
    if has_s0:
        in_specs += [rowspec(CH), st_spec]
        args += [bufp, s0]
    else:
        scratch.append(pltpu.VMEM((8, CH), F32))
    nseq = N // T
    return pl.pallas_call(
        functools.partial(_gd_kernel, R=R, NS=NS, RB=RB, has_s0=has_s0),
        grid=(nb, nt),
        in_specs=in_specs,
        out_specs=[rowspec(GD_W), st_spec,
                   pl.BlockSpec((nsb, GD_CONV - 1, CH), lambda b, t: (b, 0, 0))],
        out_shape=[jax.ShapeDtypeStruct((N, GD_W), out_dtype),
                   jax.ShapeDtypeStruct((nseq, GD_HEADS, GD_DK, GD_DV), F32),
                   jax.ShapeDtypeStruct((nseq, GD_CONV - 1, CH), F32)],
        scratch_shapes=scratch,
        compiler_params=_params(("arbitrary", "arbitrary")),
        name="gd",
    )(*args)


def _back_kernel(x_ref, ya_ref, yb_ref, yc_ref, yd_ref, wm_ref, wb_ref, wo_ref, lg_ref, lb_ref, o_ref, *, alpha):
    x = x_ref[...]
    gates = jax.nn.sigmoid(_dot(x.astype(BF16), wm_ref[...]))
    merged = None
    for i, yr in enumerate((ya_ref, yb_ref, yc_ref, yd_ref)):
        t = gates[:, i * D_MODEL:(i + 1) * D_MODEL] * _dot(yr[...].astype(BF16), wb_ref[i])
        merged = t if merged is None else merged + t
    z = alpha * x + _dot(merged.astype(BF16), wo_ref[...])
    mu = jnp.mean(z, axis=-1, keepdims=True)
    d = z - mu
    var = jnp.mean(d * d, axis=-1, keepdims=True)
    o_ref[...] = d * lax.rsqrt(var + EPS) * lg_ref[...] + lb_ref[...]


def _back_call(x, ya, yb, yc, yd, wm, wb, wo, lg, lb, *, alpha):
    N = x.shape[0]
    TM = 256
    assert N % TM == 0
    row = lambda wd: pl.BlockSpec((TM, wd), lambda i: (i, 0))
    return pl.pallas_call(
        functools.partial(_back_kernel, alpha=alpha),
        grid=(N // TM,),
        in_specs=[row(D_MODEL), row(BRANCH_W), row(BRANCH_W), row(BRANCH_W), row(BRANCH_W),
                  _const_spec(wm.shape), _const_spec(wb.shape), _const_spec(wo.shape),
                  _const_spec(lg.shape), _const_spec(lb.shape)],
        out_specs=row(D_MODEL),
        out_shape=jax.ShapeDtypeStruct((N, D_MODEL), F32),
        compiler_params=_params(("arbitrary",)),
        name="back",
    )(x, ya, yb, yc, yd, wm, wb, wo, lg, lb)


def _rope_tables(pos, d, reps, rows):
    inv = ROPE_BASE ** (-jnp.arange(0, d, 2, dtype=F32) / d)
    ang = pos.astype(F32)[:, None] * inv[None, :]
    c, s = jnp.cos(ang), jnp.sin(ang)
    cos = jnp.tile(jnp.concatenate([c, c], axis=-1), (rows // pos.shape[0], reps))
    sin = jnp.tile(jnp.concatenate([-s, s], axis=-1), (rows // pos.shape[0], reps))
    return cos, sin


def _prep_weights(w_in, mla_w_uq, mla_w_uk, mla_w_uv, w_branch, w_out):
    o = _OFF
    cols = lambda a, b: w_in[:, :, o[a]:o[b]]
    depth = w_in.shape[0]
    zpad = lambda n: jnp.zeros((depth, D_MODEL, n), w_in.dtype)
    w_ret = cols(0, 4).astype(BF16)
    w_mla = jnp.concatenate([cols(4, 6), cols(7, 8), cols(6, 7), zpad(LANES - MLA_ROPE)], axis=-1).astype(BF16)
    w_sg = cols(8, 11).astype(BF16)
    w_gd = jnp.concatenate([cols(11, 12), cols(14, 15), cols(12, 14), zpad(LANES - 2 * GD_HEADS)], axis=-1).astype(BF16)
    w_merge = cols(15, 16).astype(BF16)
    uq = mla_w_uq.reshape(depth, Q_LORA, MLA_HEADS, MLA_NOPE + MLA_ROPE)
    w_uq = jnp.concatenate([uq[..., :MLA_NOPE].reshape(depth, Q_LORA, -1),
                            uq[..., MLA_NOPE:].reshape(depth, Q_LORA, -1)], axis=-1).astype(BF16)
    w_uk = jnp.transpose(mla_w_uk, (0, 2, 3, 1)).astype(BF16)
    uv = mla_w_uv.reshape(depth, KV_LORA, MLA_HEADS // 2, 2, MLA_V)
    z = jnp.zeros((depth, KV_LORA, MLA_HEADS // 2, MLA_V), mla_w_uv.dtype)
    top = jnp.concatenate([uv[:, :, :, 0], z], axis=-1)
    bot = jnp.concatenate([z, uv[:, :, :, 1]], axis=-1)
    w_uv = jnp.transpose(jnp.concatenate([top, bot], axis=1), (0, 2, 1, 3)).astype(BF16)
    return dict(w_ret=w_ret, w_mla=w_mla, w_sg=w_sg, w_gd=w_gd, w_merge=w_merge, w_uq=w_uq, w_uk=w_uk, w_uv=w_uv,
                w_branch=w_branch.astype(BF16), w_out=w_out.astype(BF16))


def _sg_mix_weights(sg_w, sg_b, C):
    reps = SG_CHUNK // C
    w = jnp.tril(sg_w[:, :, :C, :C])
    if reps > 1:
        eye = jnp.eye(reps, dtype=w.dtype)
        w = jnp.einsum('ab,dgij->dgaibj', eye, w).reshape(w.shape[0], SG_GROUPS, SG_CHUNK, SG_CHUNK)
    b = jnp.tile(sg_b[:, :, :C], (1, 1, reps))
    bs = jnp.broadcast_to(b[..., None], b.shape + (SG_GDIM,))
    return w.astype(BF16), bs.astype(F32)


def _layer_stream(x, l, W, P, *, B, T, tabs, sample):
    cosr, sinr, cosm, sinm = tabs
    act = F32 if sample is not None else BF16
    ya, s_ret = _ret_call(x, W['w_ret'][l], cosr, sinr, P['ret_gn_g'][l][None],
                          sample['state_ret'][l] if sample else None, B=B, T=T, out_dtype=act)
    qn, qpe, kcat, ckv, kpe, gm = _mlap_call(x, W['w_mla'][l], W['w_uq'][l], P['mla_q_norm'][l][None],
                                             P['mla_kv_norm'][l][None], cosm, sinm, act_dtype=act)
    if sample:
        yb = _attn_decode_call(sample['page_table'], qn, qpe, ckv, kpe, gm, W['w_uk'][l], W['w_uv'][l],
                               sample['cache_ckv'], sample['cache_kpe'], B=B, T=T, layer=l)
    else:
        yb = _attn_prefill_call(qn, qpe, kcat, W['w_uk'][l], gm, W['w_uv'][l], B=B, T=T)
    wt, bs = (P['sg_wt_s'], P['sg_bs_s']) if sample else (P['sg_wt_p'], P['sg_bs_p'])
    sg_out = _sg_call(x, W['w_sg'][l], P['sg_ln_g'][l][None], P['sg_ln_b'][l][None], wt[l], bs[l],
                      emit_v=sample is not None, out_dtype=act)
    yc = sg_out[0]
    sgv = sg_out[1] if sample else None
    yd, s_gd, conv_new = _gd_call(x, W['w_gd'][l], P['gd_conv_w'][l], P['gd_av'][l], P['gd_dtb'][l],
                                  P['gd_norm_g'][l][None],
                                  sample['bufp'][l] if sample else None,
                                  sample['state_delta'][l] if sample else None, B=B, T=T, out_dtype=act)
    xn = _back_call(x, ya, yb, yc, yd, W['w_merge'][l], W['w_branch'][l], W['w_out'][l],
                    P['ln_g'][l][None], P['ln_b'][l][None], alpha=P['alpha'])
    return xn, ckv, kpe, s_ret, s_gd, conv_new, sgv


def kernel(x_prompt, x_sample, cache_ckv, cache_kpe, state_ret, state_delta, state_conv, page_table,
           w_in, ret_gn_g, mla_q_norm, mla_w_uq, mla_kv_norm, mla_w_uk, mla_w_uv,
           sg_ln_g, sg_ln_b, sg_w, sg_b, gd_conv_w, gd_a_log, gd_dt_bias, gd_norm_g,
           w_branch, w_out, ln_g, ln_b):
    depth = w_in.shape[0]
    bp, tp, _ = x_prompt.shape
    bs_, ts, _ = x_sample.shape
    past_len = page_table.shape[1] * cache_ckv.shape[2]
    W = _prep_weights(w_in, mla_w_uq, mla_w_uk, mla_w_uv, w_branch, w_out)
    pad_small = lambda v: jnp.pad(v, ((0, 0), (0, LANES - v.shape[-1])))[:, None, :]
    P = dict(ret_gn_g=ret_gn_g, mla_q_norm=mla_q_norm, mla_kv_norm=mla_kv_norm, sg_ln_g=sg_ln_g, sg_ln_b=sg_ln_b,
             gd_conv_w=gd_conv_w, gd_norm_g=gd_norm_g, ln_g=ln_g, ln_b=ln_b,
             gd_av=pad_small(-jnp.exp(gd_a_log)), gd_dtb=pad_small(gd_dt_bias),
             alpha=float((2.0 * depth) ** 0.25))
    P['sg_wt_p'], P['sg_bs_p'] = _sg_mix_weights(sg_w, sg_b, min(tp, SG_CHUNK))
    P['sg_wt_s'], P['sg_bs_s'] = _sg_mix_weights(sg_w, sg_b, min(ts, SG_CHUNK))

    pos_p = jnp.arange(tp)
    pos_s = past_len + jnp.arange(ts)
    tab_rows_s = 512
    tabs_p = _rope_tables(pos_p, RET_DK, RET_HEADS, tp) + _rope_tables(pos_p, MLA_ROPE, MLA_HEADS, tp)
    tabs_s = _rope_tables(pos_s, RET_DK, RET_HEADS, tab_rows_s) + _rope_tables(pos_s, MLA_ROPE, MLA_HEADS, tab_rows_s)

    bufp = jnp.pad(state_conv, ((0, 0), (0, 0), (ts - (GD_CONV - 1), 0), (0, 0))).reshape(depth, bs_ * ts, GD_CONV_CH)
    sample = dict(state_ret=state_ret, state_delta=state_delta, bufp=bufp, page_table=page_table,
                  cache_ckv=cache_ckv, cache_kpe=cache_kpe)

    xp = x_prompt.reshape(bp * tp, D_MODEL)
    xs = x_sample.reshape(bs_ * ts, D_MODEL)
    outs_p, outs_s = [], []
    for l in range(depth):
        xp, *rest = _layer_stream(xp, l, W, P, B=bp, T=tp, tabs=tabs_p, sample=None)
        outs_p.append(rest)
        xs, *rest = _layer_stream(xs, l, W, P, B=bs_, T=ts, tabs=tabs_s, sample=sample)
        outs_s.append(rest)

    def stack(outs, k, shape):
        return jnp.stack([o[k].reshape(shape) for o in outs])

    return (xp.reshape(bp, tp, D_MODEL), xs.reshape(bs_, ts, D_MODEL),
            stack(outs_p, 0, (bp, tp, KV_LORA)), stack(outs_p, 1, (bp, tp, MLA_ROPE)),
            stack(outs_p, 2, (bp, RET_HEADS, RET_DK, RET_DV)), stack(outs_p, 3, (bp, GD_HEADS, GD_DK, GD_DV)),
            stack(outs_p, 4, (bp, GD_CONV - 1, GD_CONV_CH)),
            stack(outs_s, 0, (bs_, ts, KV_LORA)), stack(outs_s, 1, (bs_, ts, MLA_ROPE)),
            stack(outs_s, 2, (bs_, RET_HEADS, RET_DK, RET_DV)), stack(outs_s, 3, (bs_, GD_HEADS, GD_DK, GD_DV)),
            stack(outs_s, 4, (bs_, GD_CONV - 1, GD_CONV_CH)), stack(outs_s, 5, (bs_, ts, SG_W)))
```

```python
import functools
import math

import numpy as np
import jax
import jax.numpy as jnp
from jax import lax
from jax.experimental import pallas as pl
from jax.experimental.pallas import tpu as pltpu

F32 = jnp.float32
BF16 = jnp.bfloat16

D_MODEL = 1024
RET_HEADS, RET_DK, RET_DV = 4, 64, 128
RET_W = RET_HEADS * RET_DV
MLA_HEADS, MLA_NOPE, MLA_ROPE, MLA_V = 8, 64, 32, 64
Q_LORA, KV_LORA = 384, 256
MLA_W = MLA_HEADS * MLA_V
MLA_SCALE = (MLA_NOPE + MLA_ROPE) ** -0.5
SG_GROUPS, SG_GDIM, SG_CHUNK = 4, 128, 128
SG_W = SG_GROUPS * SG_GDIM
GD_HEADS, GD_DK, GD_DV, GD_CONV = 4, 128, 128, 4
GD_W = GD_HEADS * GD_DV
GD_CONV_CH = 2 * GD_HEADS * GD_DK + GD_W
N_BRANCH, BRANCH_W = 4, 512
LIN_CHUNK = 64
ROPE_BASE = 10000.0
EPS = 1e-6
NEG = -1e30
IN_SPLITS = (RET_HEADS * RET_DK, RET_HEADS * RET_DK, RET_W, RET_W,
             Q_LORA, KV_LORA, MLA_ROPE, MLA_W,
             SG_W, SG_W, SG_W,
             GD_CONV_CH, GD_HEADS, GD_HEADS, GD_W,
             N_BRANCH * D_MODEL)
_OFF = np.concatenate([[0], np.cumsum(IN_SPLITS)]).tolist()

LANES = 128
KCAT_W = KV_LORA + LANES
ATT_TK = 256
ATT_LAG = 8
ATT_UNROLL = 2
QSCALE = MLA_SCALE * math.log2(math.e)
VMEM_LIMIT = 56 * 1024 * 1024

_TB = (((1,), (1,)), ((), ()))
_TA = (((0,), (0,)), ((), ()))


def _dot(a, b):
    return jnp.dot(a, b, preferred_element_type=F32)


def _dot_tb(a, b):
    return lax.dot_general(a, b, _TB, preferred_element_type=F32)


def _dot_ta(a, b):
    return lax.dot_general(a, b, _TA, preferred_element_type=F32)


def _split2(a):
    hi = a.astype(BF16)
    lo = (a - hi.astype(F32)).astype(BF16)
    return hi, lo


def _dot3(a, b):
    ah, al = _split2(a)
    bh, bl = _split2(b)
    return _dot(ah, bh) + (_dot(ah, bl) + _dot(al, bh))


def _dot_mask(mask_bf16, x):
    hi = x.astype(BF16)
    r1 = x - hi.astype(F32)
    mid = r1.astype(BF16)
    lo = (r1 - mid.astype(F32)).astype(BF16)
    return _dot(mask_bf16, hi) + (_dot(mask_bf16, mid) + _dot(mask_bf16, lo))


def _silu(x):
    return x * jax.nn.sigmoid(x)


def _gelu(x):
    return 0.5 * x * (1.0 + lax.erf(x * np.float32(math.sqrt(0.5))))


def _rope(x, cos, sin_signed, d):
    w = x.shape[-1]
    half = d // 2
    lane = lax.broadcasted_iota(jnp.int32, x.shape, 1)
    first = (lane % d) < half
    rot = jnp.where(first, pltpu.roll(x, w - half, 1), pltpu.roll(x, half, 1))
    return x * cos + rot * sin_signed


def _const_spec(shape):
    nd = len(shape)
    return pl.BlockSpec(shape, lambda *_: (0,) * nd, pipeline_mode=pl.Buffered(1))


def _params(sem):
    return pltpu.CompilerParams(dimension_semantics=sem, vmem_limit_bytes=VMEM_LIMIT)


def _ret_kernel(*refs, R, NS, RB, cdec, has_s0):
    if has_s0:
        (x_ref, w_ref, cos_ref, sin_ref, gn_ref, dmat_ref, qdec_ref, kdec_ref, s0_ref,
         y_ref, sfin_ref) = refs
    else:
        (x_ref, w_ref, cos_ref, sin_ref, gn_ref, dmat_ref, qdec_ref, kdec_ref,
         y_ref, sfin_ref) = refs
    C = R // NS
    h = _dot(x_ref[...].astype(BF16), w_ref[...])
    cos = cos_ref[...]
    sin = sin_ref[...]
    hq = RET_HEADS * RET_DK
    q = _rope(h[:, 0:hq], cos, sin, RET_DK)
    k = _rope(h[:, hq:2 * hq], cos, sin, RET_DK) * (RET_DK ** -0.5)
    v = h[:, 2 * hq:2 * hq + RET_W]
    g = _silu(h[:, 2 * hq + RET_W:2 * hq + 2 * RET_W])

    if has_s0:
        sfin_ref[...] = s0_ref[...]
    else:
        @pl.when(pl.program_id(1) == 0)
        def _():
            sfin_ref[...] = jnp.zeros(sfin_ref.shape, F32)

    probs = [(c, hh) for c in range(RB // R) for hh in range(RET_HEADS)]
    qh, kh, vh, sc, kv = {}, {}, {}, {}, {}
    for pr in probs:
        c, hh = pr
        rs = slice(c * R, (c + 1) * R)
        qh[pr] = q[rs, hh * RET_DK:(hh + 1) * RET_DK].astype(BF16)
        kh[pr] = k[rs, hh * RET_DK:(hh + 1) * RET_DK].astype(BF16)
        vh[pr] = v[rs, hh * RET_DV:(hh + 1) * RET_DV]
        sc[pr] = _dot_tb(qh[pr], kh[pr])
        vk = (vh[pr] * kdec_ref[hh]).astype(BF16)
        for s in range(NS):
            kv[pr, s] = _dot_ta(kh[pr][s * C:(s + 1) * C], vk[s * C:(s + 1) * C])
    sprev = {}
    for pr in probs:
        c, hh = pr
        for s in range(NS):
            sidx = c * NS + s if has_s0 else 0
            st = sfin_ref[sidx, hh]
            sprev[pr, s] = st.astype(BF16)
            sfin_ref[sidx, hh] = st * cdec[hh] + kv[pr, s]
    for pr in probs:
        c, hh = pr
        rs = slice(c * R, (c + 1) * R)
        vs = slice(hh * RET_DV, (hh + 1) * RET_DV)
        o = _dot((sc[pr] * dmat_ref[hh]).astype(BF16), vh[pr].astype(BF16))
        qd = qdec_ref[hh]
        parts = [_dot(qh[pr][s * C:(s + 1) * C], sprev[pr, s]) * qd[s * C:(s + 1) * C] for s in range(NS)]
        o = o + (parts[0] if NS == 1 else jnp.concatenate(parts, axis=0))
        mu = jnp.mean(o, axis=-1, keepdims=True)
        d = o - mu
        var = jnp.mean(d * d, axis=-1, keepdims=True)
        on = d * lax.rsqrt(var + EPS) * gn_ref[:, vs]
        y_ref[rs, vs] = (g[rs, vs] * on).astype(y_ref.dtype)


def _ret_consts(R, NS):
    C = R // NS
    lg = np.log1p(-np.exp2(-5.0 - np.arange(RET_HEADS, dtype=np.float64)))
    idx = np.arange(R)
    tok = idx % C
    same = (idx[:, None] // C) == (idx[None, :] // C)
    rel = tok[:, None] - tok[None, :]
    low = same & (rel >= 0)
    dmat = np.where(low[None], np.exp(lg[:, None, None] * np.where(low, rel, 0)[None]), 0.0)
    qdec = np.exp(lg[:, None] * (tok[None, :] + 1.0))
    kdec = np.exp(lg[:, None] * (C - 1.0 - tok[None, :]))
    cdec = tuple(float(v) for v in np.exp(lg * C))
    bc = lambda a: jnp.asarray(np.broadcast_to(a[:, :, None], (RET_HEADS, R, LANES)), F32)
    return jnp.asarray(dmat, F32), bc(qdec), bc(kdec), cdec


def _ret_call(x, w, cos, sin, gn, s0, *, B, T, out_dtype):
    N = x.shape[0]
    has_s0 = s0 is not None
    R = LIN_CHUNK
    if has_s0:
        C = math.gcd(T, LIN_CHUNK)
        assert C == T, "sample stream is a single chunk per sequence"
        NS = R // C
        RB = 2 * R
        nb, nt = N // RB, 1
        nsb = RB // C
    else:
        NS, RB = 1, 512
        assert T % RB == 0 and math.gcd(T, LIN_CHUNK) == R
        nb, nt = B, T // RB
        nsb = 1
    dmat, qdec, kdec, cdec = _ret_consts(R, NS)
    tr = cos.shape[0] // RB
    hq = RET_HEADS * RET_DK
    in_specs = [
        pl.BlockSpec((RB, D_MODEL), lambda b, t: (b * nt + t, 0)),
        _const_spec(w.shape),
        pl.BlockSpec((RB, hq), lambda b, t: ((b * nt + t) % tr, 0)),
        pl.BlockSpec((RB, hq), lambda b, t: ((b * nt + t) % tr, 0)),
        _const_spec(gn.shape), _const_spec(dmat.shape), _const_spec(qdec.shape), _const_spec(kdec.shape),
    ]
    args = [x, w, cos, sin, gn, dmat, qdec, kdec]
    st_spec = pl.BlockSpec((nsb, RET_HEADS, RET_DK, RET_DV), lambda b, t: (b, 0, 0, 0))
    if has_s0:
        in_specs.append(st_spec)
        args.append(s0)
    nseq = N // T
    return pl.pallas_call(
        functools.partial(_ret_kernel, R=R, NS=NS, RB=RB, cdec=cdec, has_s0=has_s0),
        grid=(nb, nt),
        in_specs=in_specs,
        out_specs=[pl.BlockSpec((RB, RET_W), lambda b, t: (b * nt + t, 0)), st_spec],
        out_shape=[jax.ShapeDtypeStruct((N, RET_W), out_dtype),
                   jax.ShapeDtypeStruct((nseq, RET_HEADS, RET_DK, RET_DV), F32)],
        compiler_params=_params(("arbitrary", "arbitrary")),
        name="ret",
    )(*args)


def _mlap_kernel(x_ref, w_ref, wuq_ref, qn_ref, kvn_ref, cos_ref, sin_ref, *outs, TM, emit_kt):
    if emit_kt:
        qnope_ref, qpe_ref, ckv_ref, kpe_ref, gm_ref, kt_ref, vb_ref = outs
    else:
        qnope_ref, qpe_ref, ckv_ref, kpe_ref, gm_ref = outs
    h = _dot(x_ref[...].astype(BF16), w_ref[...])
    cq = h[:, 0:Q_LORA]
    cq = cq * lax.rsqrt(jnp.mean(cq * cq, axis=-1, keepdims=True) + EPS) * qn_ref[...]
    qf = _dot(cq.astype(BF16), wuq_ref[...])
    nw = MLA_HEADS * MLA_NOPE
    qnope_ref[...] = (qf[:, 0:nw] * QSCALE).astype(qnope_ref.dtype)
    cos = cos_ref[...]
    sin = sin_ref[...]
    qpe_ref[...] = _rope(qf[:, nw:nw + MLA_HEADS * MLA_ROPE], cos, sin, MLA_ROPE) * QSCALE
    c0 = Q_LORA
    kv = h[:, c0:c0 + KV_LORA]
    ckv = kv * lax.rsqrt(jnp.mean(kv * kv, axis=-1, keepdims=True) + EPS) * kvn_ref[...]
    ckv_ref[...] = ckv
    g0 = c0 + KV_LORA
    gm_ref[...] = _silu(h[:, g0:g0 + MLA_W]).astype(gm_ref.dtype)
    p0 = g0 + MLA_W
    kblk = _rope(h[:, p0:p0 + LANES], cos[:, 0:LANES], sin[:, 0:LANES], MLA_ROPE)
    kpe_ref[...] = kblk[:, 0:MLA_ROPE]
    if emit_kt:
        rep = kblk
        for i in range(1, LANES // MLA_ROPE):
            rep = rep + pltpu.roll(kblk, i * MLA_ROPE, 1)
        kct = jnp.concatenate([ckv, rep], axis=1).T
        for i in range(TM // ATT_TK):
            kt_ref[i] = kct[:, i * ATT_TK:(i + 1) * ATT_TK].astype(BF16)
        vb_ref[...] = ckv.astype(BF16)


def _mlap_call(x, w, wuq, qn, kvn, cos, sin, *, act_dtype, emit_kt):
    N = x.shape[0]
    TM = 512
    assert N % TM == 0 and TM % ATT_TK == 0
    tr = cos.shape[0] // TM
    row = lambda wd: pl.BlockSpec((TM, wd), lambda i: (i, 0))
    tab = pl.BlockSpec((TM, MLA_HEADS * MLA_ROPE), lambda i: (i % tr, 0))
    nw = MLA_HEADS * MLA_NOPE
    out_specs = [row(nw), row(MLA_HEADS * MLA_ROPE), row(KV_LORA), row(MLA_ROPE), row(MLA_W)]
    out_shape = [jax.ShapeDtypeStruct((N, nw), act_dtype),
                 jax.ShapeDtypeStruct((N, MLA_HEADS * MLA_ROPE), F32),
                 jax.ShapeDtypeStruct((N, KV_LORA), F32),
                 jax.ShapeDtypeStruct((N, MLA_ROPE), F32),
                 jax.ShapeDtypeStruct((N, MLA_W), act_dtype)]
    if emit_kt:
        out_specs += [pl.BlockSpec((TM // ATT_TK, KCAT_W, ATT_TK), lambda i: (i, 0, 0)), row(KV_LORA)]
        out_shape += [jax.ShapeDtypeStruct((N // ATT_TK, KCAT_W, ATT_TK), BF16),
                      jax.ShapeDtypeStruct((N, KV_LORA), BF16)]
    return pl.pallas_call(
        functools.partial(_mlap_kernel, TM=TM, emit_kt=emit_kt),
        grid=(N // TM,),
        in_specs=[row(D_MODEL), _const_spec(w.shape), _const_spec(wuq.shape), _const_spec(qn.shape),
                  _const_spec(kvn.shape), tab, tab],
        out_specs=out_specs,
        out_shape=out_shape,
        compiler_params=_params(("arbitrary",)),
        name="mlap",
    )(x, w, wuq, qn, kvn, cos, sin)


def _attn_prefill_kernel(qn_ref, qpe_ref, kt_ref, v_ref, wuk_ref, gm_ref, wuv_ref, y_ref,
                         q_s, m_s, l_s, acc_s, *, BQ, TK):
    i = pl.program_id(1)
    H = MLA_HEADS
    lane = lax.broadcasted_iota(jnp.int32, (BQ, LANES), 1)
    for hh in range(H):
        ql = _dot(qn_ref[:, hh * MLA_NOPE:(hh + 1) * MLA_NOPE], wuk_ref[hh])
        q_s[hh * BQ:(hh + 1) * BQ, 0:KV_LORA] = ql.astype(BF16)
        per = LANES // MLA_ROPE
        blk = qpe_ref[:, (hh // per) * LANES:(hh // per + 1) * LANES]
        off = (hh % per) * MLA_ROPE
        keep = (lane >= off) & (lane < off + MLA_ROPE)
        q_s[hh * BQ:(hh + 1) * BQ, KV_LORA:KCAT_W] = jnp.where(keep, blk, 0.0).astype(BF16)
    m_s[...] = jnp.full(m_s.shape, NEG, F32)
    l_s[...] = jnp.zeros(l_s.shape, F32)
    acc_s[...] = jnp.zeros(acc_s.shape, F32)

    def steps(blocks):
        kts = [kt_ref[j] for j, _ in blocks]
        vbs = [v_ref[pl.ds(pl.multiple_of(j * TK, TK), TK), :] for j, _ in blocks]
        row = lax.broadcasted_iota(jnp.int32, (BQ, TK), 0)
        col = lax.broadcasted_iota(jnp.int32, (BQ, TK), 1)
        items = [(bi, hh) for bi in range(len(blocks)) for hh in range(H)]
        scores = {}
        for t in range(len(items) + ATT_LAG):
            if t < len(items):
                bi, hh = items[t]
                scores[t] = _dot(q_s[hh * BQ:(hh + 1) * BQ, :], kts[bi])
            if t < ATT_LAG:
                continue
            bi, hh = items[t - ATT_LAG]
            rs = slice(hh * BQ, (hh + 1) * BQ)
            s = scores.pop(t - ATT_LAG)
            if blocks[bi][1]:
                s = jnp.where(blocks[bi][0] * TK + col <= i * BQ + row, s, NEG)
            m_prev = m_s[rs, :]
            m_new = jnp.maximum(m_prev, jnp.max(s, axis=-1, keepdims=True))
            alpha = jnp.exp2(m_prev - m_new)
            p = jnp.exp2(s - jnp.concatenate([m_new] * (TK // LANES), axis=1))
            l_s[rs, :] = alpha[:, 0:1] * l_s[rs, :] + jnp.sum(p, axis=-1, keepdims=True)
            acc_s[rs, :] = (jnp.concatenate([alpha] * (KV_LORA // LANES), axis=1) * acc_s[rs, :]
                            + _dot(p.astype(BF16), vbs[bi]))
            m_s[rs, :] = m_new

    last = (i * BQ + BQ - 1) // TK
    npair = last // ATT_UNROLL

    def body(jj, carry):
        steps([(jj * ATT_UNROLL + u, False) for u in range(ATT_UNROLL)])
        return carry

    lax.fori_loop(0, npair, body, 0)

    def tail(j, carry):
        steps([(j, False)])
        return carry

    lax.fori_loop(npair * ATT_UNROLL, last, tail, 0)
    steps([(last, True)])

    for p2 in range(H // 2):
        halves = []
        for hh in (2 * p2, 2 * p2 + 1):
            rs = slice(hh * BQ, (hh + 1) * BQ)
            halves.append((acc_s[rs, :] / l_s[rs, :]).astype(BF16))
        yb = _dot(jnp.concatenate(halves, axis=1), wuv_ref[p2])
        y_ref[:, p2 * LANES:(p2 + 1) * LANES] = (gm_ref[:, p2 * LANES:(p2 + 1) * LANES].astype(F32) * yb).astype(y_ref.dtype)


def _attn_prefill_call(qn, qpe, kt, vb, wuk, gm, wuv, *, B, T):
    N = qn.shape[0]
    BQ, TK = 128, ATT_TK
    assert T % TK == 0 and T % BQ == 0
    nq = T // BQ
    H = MLA_HEADS
    row = lambda wd: pl.BlockSpec((BQ, wd), lambda b, i: (b * nq + i, 0))
    return pl.pallas_call(
        functools.partial(_attn_prefill_kernel, BQ=BQ, TK=TK),
        grid=(B, nq),
        in_specs=[row(H * MLA_NOPE), row(H * MLA_ROPE),
                  pl.BlockSpec((T // TK, KCAT_W, TK), lambda b, i: (b, 0, 0)),
                  pl.BlockSpec((T, KV_LORA), lambda b, i: (b, 0)),
                  _const_spec(wuk.shape), row(MLA_W), _const_spec(wuv.shape)],
        out_specs=row(MLA_W),
        out_shape=jax.ShapeDtypeStruct((N, MLA_W), BF16),
        scratch_shapes=[pltpu.VMEM((H * BQ, KCAT_W), BF16), pltpu.VMEM((H * BQ, LANES), F32),
                        pltpu.VMEM((H * BQ, 1), F32), pltpu.VMEM((H * BQ, KV_LORA), F32)],
        compiler_params=_params(("arbitrary", "arbitrary")),
        name="attn_prefill",
    )(qn, qpe, kt, vb, wuk, gm, wuv)


def _attn_decode_kernel(pt_ref, qn_ref, qpe_ref, ckvn_ref, kpen_ref, gm_ref, wuk_ref, wuv_ref,
                        cckv_ref, ckpe_ref, y_ref,
                        kbuf, pbuf, sem, ql_s, qp_s, m_s, l_s, acc_s, *, T, G, NSTEP, NSUB, layer, PAGE):
    b = pl.program_id(0)
    j = pl.program_id(1)
    n = b * NSTEP + j
    total = pl.num_programs(0) * NSTEP
    slot = n % 2
    H = MLA_HEADS

    def copies(bb, jj, sl):
        out = []
        for g in range(G):
            page = pt_ref[bb, jj * G + g]
            out.append(pltpu.make_async_copy(cckv_ref.at[layer, page], kbuf.at[sl, g], sem.at[0, sl]))
            out.append(pltpu.make_async_copy(ckpe_ref.at[layer, page], pbuf.at[sl, g], sem.at[1, sl]))
        return out

    @pl.when(n == 0)
    def _():
        for c in copies(b, j, slot):
            c.start()

    is_last = n == total - 1
    wrap = j + 1 == NSTEP
    bn = jnp.where(is_last, b, jnp.where(wrap, b + 1, b))
    jn = jnp.where(is_last, j, jnp.where(wrap, 0, j + 1))
    for c in copies(bn, jn, 1 - slot):
        c.start()

    @pl.when(j == 0)
    def _():
        for hh in range(H):
            ql = _dot(qn_ref[:, hh * MLA_NOPE:(hh + 1) * MLA_NOPE].astype(BF16), wuk_ref[hh])
            ql_s[hh * T:(hh + 1) * T, :] = ql
            qp_s[hh * T:(hh + 1) * T, :] = qpe_ref[:, hh * MLA_ROPE:(hh + 1) * MLA_ROPE]
        kn = ckvn_ref[...].astype(BF16)
        s = _dot_tb(ql_s[...].astype(BF16), kn) + _dot_tb(qp_s[...].astype(BF16), kpen_ref[...].astype(BF16))
        row = lax.broadcasted_iota(jnp.int32, s.shape, 0)
        col = lax.broadcasted_iota(jnp.int32, s.shape, 1)
        s = jnp.where(col <= row % T, s, NEG)
        m = jnp.max(s, axis=-1, keepdims=True)
        p = jnp.exp2(s - m)
        m_s[...] = m
        l_s[...] = jnp.sum(p, axis=-1, keepdims=True)
        acc_s[...] = _dot(p.astype(BF16), kn)

    for c in copies(b, j, slot):
        c.wait()

    ql = ql_s[...].astype(BF16)
    qp = qp_s[...].astype(BF16)
    m = m_s[...]
    l = l_s[...]
    acc = acc_s[...]
    GS = G // NSUB
    kbs, scores = [], []
    for u in range(NSUB):
        kb = kbuf[slot, u * GS:(u + 1) * GS].reshape(GS * PAGE, KV_LORA).astype(BF16)
        pcat = jnp.concatenate([pbuf[slot, u * GS + g] for g in range(GS)], axis=1).astype(BF16)
        kbs.append(kb)
        scores.append(_dot_tb(ql, kb) + _dot(qp, pcat))
    for u in range(NSUB):
        kb, s = kbs[u], scores[u]
        m_new = jnp.maximum(m, jnp.max(s, axis=-1, keepdims=True))
        alpha = jnp.exp2(m - m_new)
        p = jnp.exp2(s - m_new)
        l = alpha * l + jnp.sum(p, axis=-1, keepdims=True)
        acc = alpha * acc + _dot(p.astype(BF16), kb)
        m = m_new
    m_s[...] = m
    l_s[...] = l
    acc_s[...] = acc

    @pl.when(j == NSTEP - 1)
    def _():
        o = acc / l
        for p2 in range(H // 2):
            lhs = jnp.concatenate([o[(2 * p2) * T:(2 * p2 + 1) * T], o[(2 * p2 + 1) * T:(2 * p2 + 2) * T]], axis=1)
            yb = _dot(lhs.astype(BF16), wuv_ref[p2])
            y_ref[:, p2 * LANES:(p2 + 1) * LANES] = gm_ref[:, p2 * LANES:(p2 + 1) * LANES] * yb

    @pl.when(is_last)
    def _():
        for c in copies(b, j, 1 - slot):
            c.wait()


def _attn_decode_call(page_table, qn, qpe, ckv, kpe, gm, wuk, wuv, cache_ckv, cache_kpet, *, B, T, layer):
    N = qn.shape[0]
    n_pages = page_table.shape[1]
    PAGE = cache_ckv.shape[2]
    G = math.gcd(n_pages, 32)
    NSTEP = n_pages // G
    NSUB = math.gcd(G, 4)
    H = MLA_HEADS
    row = lambda wd: pl.BlockSpec((T, wd), lambda b, j, pt: (b, 0))
    cst = lambda shp: pl.BlockSpec(shp, lambda b, j, pt: (0,) * len(shp))
    grid_spec = pltpu.PrefetchScalarGridSpec(
        num_scalar_prefetch=1,
        grid=(B, NSTEP),
        in_specs=[row(H * MLA_NOPE), row(H * MLA_ROPE), row(KV_LORA), row(MLA_ROPE), row(MLA_W),
                  cst(wuk.shape), cst(wuv.shape),
                  pl.BlockSpec(memory_space=pl.ANY), pl.BlockSpec(memory_space=pl.ANY)],
        out_specs=row(MLA_W),
        scratch_shapes=[pltpu.VMEM((2, G, PAGE, KV_LORA), F32), pltpu.VMEM((2, G, MLA_ROPE, PAGE), F32),
                        pltpu.SemaphoreType.DMA((2, 2)),
                        pltpu.VMEM((H * T, KV_LORA), F32), pltpu.VMEM((H * T, MLA_ROPE), F32),
                        pltpu.VMEM((H * T, 1), F32), pltpu.VMEM((H * T, 1), F32),
                        pltpu.VMEM((H * T, KV_LORA), F32)],
    )
    return pl.pallas_call(
        functools.partial(_attn_decode_kernel, T=T, G=G, NSTEP=NSTEP, NSUB=NSUB, layer=layer, PAGE=PAGE),
        grid_spec=grid_spec,
        out_shape=jax.ShapeDtypeStruct((N, MLA_W), F32),
        compiler_params=_params(("arbitrary", "arbitrary")),
        name="attn_decode",
    )(page_table, qn, qpe, ckv, kpe, gm, wuk, wuv, cache_ckv, cache_kpet)


def _sg_kernel(x_ref, w_ref, lng_ref, lnb_ref, wt_ref, bs_ref, y_ref, *rest, TM, emit_v):
    h = _dot(x_ref[...].astype(BF16), w_ref[...])
    u = _gelu(h[:, 0:SG_W])
    gv = _gelu(h[:, SG_W:2 * SG_W])
    mu = jnp.mean(gv, axis=-1, keepdims=True)
    d = gv - mu
    var = jnp.mean(d * d, axis=-1, keepdims=True)
    v = d * lax.rsqrt(var + EPS) * lng_ref[...] + lnb_ref[...]
    if emit_v:
        rest[0][...] = v
    gs = _silu(h[:, 2 * SG_W:3 * SG_W])
    vb = v.astype(BF16)
    for c in range(TM // SG_CHUNK):
        rows = slice(c * SG_CHUNK, (c + 1) * SG_CHUNK)
        for g in range(SG_GROUPS):
            cols = slice(g * SG_GDIM, (g + 1) * SG_GDIM)
            s = _dot(wt_ref[g], vb[rows, cols]) + bs_ref[g]
            y_ref[rows, cols] = (gs[rows, cols] * (u[rows, cols] * s)).astype(y_ref.dtype)


def _sg_call(x, w, lng, lnb, wt, bs, *, emit_v, out_dtype):
    N = x.shape[0]
    TM = 512
    assert N % TM == 0
    row = lambda wd: pl.BlockSpec((TM, wd), lambda i: (i, 0))
    out_specs = [row(SG_W)]
    out_shape = [jax.ShapeDtypeStruct((N, SG_W), out_dtype)]
    if emit_v:
        out_specs.append(row(SG_W))
        out_shape.append(jax.ShapeDtypeStruct((N, SG_W), F32))
    return pl.pallas_call(
        functools.partial(_sg_kernel, TM=TM, emit_v=emit_v),
        grid=(N // TM,),
        in_specs=[row(D_MODEL), _const_spec(w.shape), _const_spec(lng.shape), _const_spec(lnb.shape),
                  _const_spec(wt.shape), _const_spec(bs.shape)],
        out_specs=out_specs,
        out_shape=out_shape,
        compiler_params=_params(("arbitrary",)),
        name="sg",
    )(x, w, lng, lnb, wt, bs)


def _gd_kernel(*refs, R, NS, RB, has_s0):
    if has_s0:
        (x_ref, w_ref, cw_ref, av_ref, dtb_ref, ng_ref, msk_ref, bufp_ref, s0_ref,
         y_ref, sfin_ref, conv_ref) = refs
    else:
        (x_ref, w_ref, cw_ref, av_ref, dtb_ref, ng_ref, msk_ref,
         y_ref, sfin_ref, conv_ref, carry_s) = refs
    C = R // NS
    J = max(1, int(math.ceil(math.log2(C))))
    CH = GD_CONV_CH
    h = _dot(x_ref[...].astype(BF16), w_ref[...])
    xq = h[:, 0:CH]
    y = xq * cw_ref[GD_CONV - 1:GD_CONV, :]
    if has_s0:
        x3 = xq.reshape(RB // 8, 8, CH)
        b3 = bufp_ref[...].reshape(RB // 8, 8, CH)
        tpos = lax.broadcasted_iota(jnp.int32, x3.shape, 1)
        for k in range(1, GD_CONV):
            sh = jnp.where(tpos >= k, pltpu.roll(x3, k, 1), pltpu.roll(b3, k, 1))
            y = y + sh.reshape(RB, CH) * cw_ref[GD_CONV - 1 - k:GD_CONV - k, :]
        for s in range(RB // C):
            conv_ref[s] = xq[s * C + C - (GD_CONV - 1):s * C + C, :]
    else:
        @pl.when(pl.program_id(1) == 0)
        def _():
            carry_s[...] = jnp.zeros(carry_s.shape, F32)
        cprev = carry_s[...]
        row8 = lax.broadcasted_iota(jnp.int32, (8, CH), 0)
        for k in range(1, GD_CONV):
            big = pltpu.roll(xq, k, 0)
            head = jnp.where(row8 < k, pltpu.roll(cprev, k, 0), big[0:8])
            sh = jnp.concatenate([head, big[8:]], axis=0)
            y = y + sh * cw_ref[GD_CONV - 1 - k:GD_CONV - k, :]
        carry_s[...] = xq[RB - 8:RB]
        conv_ref[0] = xq[RB - (GD_CONV - 1):RB, :]
    qkv = _silu(y)
    hw = GD_HEADS * GD_DK
    qn, kn = [], []
    for hh in range(GD_HEADS):
        qb = qkv[:, hh * GD_DK:(hh + 1) * GD_DK]
        qn.append(qb * lax.rsqrt(jnp.sum(qb * qb, axis=-1, keepdims=True) + EPS) * (GD_DK ** -0.5))
        kb = qkv[:, hw + hh * GD_DK:hw + (hh + 1) * GD_DK]
        kn.append(kb * lax.rsqrt(jnp.sum(kb * kb, axis=-1, keepdims=True) + EPS))
    vv = qkv[:, 2 * hw:2 * hw + GD_W]
    gate = _silu(h[:, CH:CH + GD_W])
    ab = h[:, CH + GD_W:CH + GD_W + LANES]
    lane = lax.broadcasted_iota(jnp.int32, ab.shape, 1)
    gb = jnp.where(lane < GD_HEADS, av_ref[...] * jax.nn.softplus(ab + dtb_ref[...]), jax.nn.sigmoid(ab))

    if has_s0:
        sfin_ref[...] = s0_ref[...]
    else:
        @pl.when(pl.program_id(1) == 0)
        def _():
            sfin_ref[...] = jnp.zeros(sfin_ref.shape, F32)

    lowf = msk_ref[0]
    low = lowf > 0.0
    strictf = msk_ref[1]
    lowb = lowf.astype(BF16)
    sameb = msk_ref[2].astype(BF16)
    ri = lax.broadcasted_iota(jnp.int32, (R, R), 0)
    ci = lax.broadcasted_iota(jnp.int32, (R, R), 1)
    eye = (ri == ci).astype(F32)

    nch = RB // R
    probs = [(c, hh) for c in range(nch) for hh in range(GD_HEADS)]
    gbc = [gb[c * R:(c + 1) * R] for c in range(nch)]
    gc4 = [_dot_mask(lowb, g) for g in gbc]
    gl4 = [_dot_mask(sameb, g) for g in gbc]
    qh, kh, qkk = {}, {}, {}
    for pr in probs:
        c, hh = pr
        rs = slice(c * R, (c + 1) * R)
        qh[pr], kh[pr] = qn[hh][rs], kn[hh][rs]
        khb = kh[pr].astype(BF16)
        qkk[pr] = _dot_tb(jnp.concatenate([qh[pr].astype(BF16), khb], axis=0), khb)
    gcb, glb, beta, gam, qk, mp, pinv = {}, {}, {}, {}, {}, {}, {}
    for pr in probs:
        c, hh = pr
        gcb[pr] = jnp.broadcast_to(gc4[c][:, hh:hh + 1], (R, LANES))
        glb[pr] = jnp.broadcast_to(gl4[c][:, hh:hh + 1], (R, LANES))
        beta[pr] = jnp.broadcast_to(gbc[c][:, GD_HEADS + hh:GD_HEADS + hh + 1], (R, LANES))
        gi = gcb[pr][:, 0:R]
        gam[pr] = jnp.where(low, jnp.exp(jnp.where(low, gi - gi.T, 0.0)), 0.0)
        qk[pr] = (qkk[pr][0:R] * gam[pr]).astype(BF16)
        mp[pr] = -(strictf * (beta[pr][:, 0:R] * qkk[pr][R:2 * R] * gam[pr]))
        pinv[pr] = eye + mp[pr]
    for _ in range(J - 1):
        for pr in probs:
            mp[pr] = _dot3(mp[pr], mp[pr])
        for pr in probs:
            pinv[pr] = pinv[pr] + _dot3(pinv[pr], mp[pr])
    pre = {}
    for pr in probs:
        c, hh = pr
        vh = vv[c * R:(c + 1) * R, hh * GD_DK:(hh + 1) * GD_DK]
        eg = jnp.exp(gcb[pr])
        sol = _dot3(pinv[pr], jnp.concatenate([vh * beta[pr], kh[pr] * (beta[pr] * eg)], axis=1))
        pre[pr] = (sol[:, 0:GD_DV], sol[:, GD_DV:GD_DV + GD_DK].astype(BF16),
                   (qh[pr] * eg).astype(BF16), (kh[pr] * jnp.exp(glb[pr] - gcb[pr])).astype(BF16),
                   jnp.exp(glb[pr]))

    for c in range(nch):
        rs = slice(c * R, (c + 1) * R)
        seqs = [(hh, s) for hh in range(GD_HEADS) for s in range(NS)]
        st, sb, ws, os_ = {}, {}, {}, {}
        for hh, s in seqs:
            sidx = c * NS + s if has_s0 else 0
            st[hh, s] = sfin_ref[sidx, hh]
            sb[hh, s] = st[hh, s].astype(BF16)
        for hh, s in seqs:
            rows = slice(s * C, (s + 1) * C)
            u, wb, qd, kd, el = pre[c, hh]
            ws[hh, s] = _dot(wb[rows], sb[hh, s])
            os_[hh, s] = _dot(qd[rows], sb[hh, s])
        vnew = {}
        for hh in range(GD_HEADS):
            u = pre[c, hh][0]
            vns = [u[s * C:(s + 1) * C] - ws[hh, s] for s in range(NS)]
            vnew[hh] = vns[0] if NS == 1 else jnp.concatenate(vns, axis=0)
        for hh, s in seqs:
            rows = slice(s * C, (s + 1) * C)
            kd, el = pre[c, hh][3], pre[c, hh][4]
            sidx = c * NS + s if has_s0 else 0
            sfin_ref[sidx, hh] = (st[hh, s] * el[s * C:s * C + 1, :]
                                  + _dot_ta(kd[rows], vnew[hh][rows].astype(BF16)))
        for hh in range(GD_HEADS):
            sl = slice(hh * GD_DK, (hh + 1) * GD_DK)
            oi = [os_[hh, s] for s in range(NS)]
            o = (oi[0] if NS == 1 else jnp.concatenate(oi, axis=0)) + _dot(qk[c, hh], vnew[hh].astype(BF16))
            on = o * lax.rsqrt(jnp.mean(o * o, axis=-1, keepdims=True) + EPS) * ng_ref[...]
            y_ref[rs, sl] = (gate[rs, sl] * on).astype(y_ref.dtype)


def _gd_masks(R, NS):
    C = R // NS
    idx = np.arange(R)
    same = (idx[:, None] // C) == (idx[None, :] // C)
    low = same & (idx[:, None] >= idx[None, :])
    strict = same & (idx[:, None] > idx[None, :])
    return jnp.asarray(np.stack([low, strict, same]).astype(np.float32))


def _gd_call(x, w, cw, av, dtb, ng, bufp, s0, *, B, T, out_dtype):
    N = x.shape[0]
    has_s0 = s0 is not None
    R = LIN_CHUNK
    CH = GD_CONV_CH
    if has_s0:
        C = math.gcd(T, LIN_CHUNK)
        assert C == T and T == 8 and T >= GD_CONV - 1
        NS = R // C
        RB = 2 * R
        nb, nt = N // RB, 1
        nsb = RB // C
    else:
        NS, RB = 1, 256
        assert T % RB == 0 and math.gcd(T, LIN_CHUNK) == R
        nb, nt = B, T // RB
        nsb = 1
    msk = _gd_masks(R, NS)
    rowspec = lambda wd: pl.BlockSpec((RB, wd), lambda b, t: (b * nt + t, 0))
    in_specs = [rowspec(D_MODEL), _const_spec(w.shape), _const_spec(cw.shape), _const_spec(av.shape),
                _const_spec(dtb.shape), _const_spec(ng.shape), _const_spec(msk.shape)]
    args = [x, w, cw, av, dtb, ng, msk]
    st_spec = pl.BlockSpec((nsb, GD_HEADS, GD_DK, GD_DV), lambda b, t: (b, 0, 0, 0))
    scratch = []
    if has_s0:
        in_specs += [rowspec(CH), st_spec]
        args += [bufp, s0]
    else:
        scratch.append(pltpu.VMEM((8, CH), F32))
    nseq = N // T
    return pl.pallas_call(
        functools.partial(_gd_kernel, R=R, NS=NS, RB=RB, has_s0=has_s0),
        grid=(nb, nt),
        in_specs=in_specs,
        out_specs=[rowspec(GD_W), st_spec,
                   pl.BlockSpec((nsb, GD_CONV - 1, CH), lambda b, t: (b, 0, 0))],
        out_shape=[jax.ShapeDtypeStruct((N, GD_W), out_dtype),
                   jax.ShapeDtypeStruct((nseq, GD_HEADS, GD_DK, GD_DV), F32),
                   jax.ShapeDtypeStruct((nseq, GD_CONV - 1, CH), F32)],
        scratch_shapes=scratch,
        compiler_params=_params(("arbitrary", "arbitrary")),
        name="gd",
    )(*args)


def _back_kernel(x_ref, ya_ref, yb_ref, yc_ref, yd_ref, wm_ref, wb_ref, wo_ref, lg_ref, lb_ref, o_ref, *, alpha):
    x = x_ref[...]
    gates = jax.nn.sigmoid(_dot(x.astype(BF16), wm_ref[...]))
    merged = None
    for i, yr in enumerate((ya_ref, yb_ref, yc_ref, yd_ref)):
        t = gates[:, i * D_MODEL:(i + 1) * D_MODEL] * _dot(yr[...].astype(BF16), wb_ref[i])
        merged = t if merged is None else merged + t
    z = alpha * x + _dot(merged.astype(BF16), wo_ref[...])
    mu = jnp.mean(z, axis=-1, keepdims=True)
    d = z - mu
    var = jnp.mean(d * d, axis=-1, keepdims=True)
    o_ref[...] = d * lax.rsqrt(var + EPS) * lg_ref[...] + lb_ref[...]


def _back_call(x, ya, yb, yc, yd, wm, wb, wo, lg, lb, *, alpha):
    N = x.shape[0]
    TM = 256
    assert N % TM == 0
    row = lambda wd: pl.BlockSpec((TM, wd), lambda i: (i, 0))
    return pl.pallas_call(
        functools.partial(_back_kernel, alpha=alpha),
        grid=(N // TM,),
        in_specs=[row(D_MODEL), row(BRANCH_W), row(BRANCH_W), row(BRANCH_W), row(BRANCH_W),
                  _const_spec(wm.shape), _const_spec(wb.shape), _const_spec(wo.shape),
                  _const_spec(lg.shape), _const_spec(lb.shape)],
        out_specs=row(D_MODEL),
        out_shape=jax.ShapeDtypeStruct((N, D_MODEL), F32),
        compiler_params=_params(("arbitrary",)),
        name="back",
    )(x, ya, yb, yc, yd, wm, wb, wo, lg, lb)


def _rope_tables(pos, d, reps, rows):
    inv = ROPE_BASE ** (-jnp.arange(0, d, 2, dtype=F32) / d)
    ang = pos.astype(F32)[:, None] * inv[None, :]
    c, s = jnp.cos(ang), jnp.sin(ang)
    cos = jnp.tile(jnp.concatenate([c, c], axis=-1), (rows // pos.shape[0], reps))
    sin = jnp.tile(jnp.concatenate([-s, s], axis=-1), (rows // pos.shape[0], reps))
    return cos, sin


def _prep_weights(w_in, mla_w_uq, mla_w_uk, mla_w_uv, w_branch, w_out):
    o = _OFF
    cols = lambda a, b: w_in[:, :, o[a]:o[b]]
    depth = w_in.shape[0]
    zpad = lambda n: jnp.zeros((depth, D_MODEL, n), w_in.dtype)
    w_ret = cols(0, 4).astype(BF16)
    w_mla = jnp.concatenate([cols(4, 6), cols(7, 8), cols(6, 7), zpad(LANES - MLA_ROPE)], axis=-1).astype(BF16)
    w_sg = cols(8, 11).astype(BF16)
    w_gd = jnp.concatenate([cols(11, 12), cols(14, 15), cols(12, 14), zpad(LANES - 2 * GD_HEADS)], axis=-1).astype(BF16)
    w_merge = cols(15, 16).astype(BF16)
    uq = mla_w_uq.reshape(depth, Q_LORA, MLA_HEADS, MLA_NOPE + MLA_ROPE)
    w_uq = jnp.concatenate([uq[..., :MLA_NOPE].reshape(depth, Q_LORA, -1),
                            uq[..., MLA_NOPE:].reshape(depth, Q_LORA, -1)], axis=-1).astype(BF16)
    w_uk = jnp.transpose(mla_w_uk, (0, 2, 3, 1)).astype(BF16)
    uv = mla_w_uv.reshape(depth, KV_LORA, MLA_HEADS // 2, 2, MLA_V)
    z = jnp.zeros((depth, KV_LORA, MLA_HEADS // 2, MLA_V), mla_w_uv.dtype)
    top = jnp.concatenate([uv[:, :, :, 0], z], axis=-1)
    bot = jnp.concatenate([z, uv[:, :, :, 1]], axis=-1)
    w_uv = jnp.transpose(jnp.concatenate([top, bot], axis=1), (0, 2, 1, 3)).astype(BF16)
    return dict(w_ret=w_ret, w_mla=w_mla, w_sg=w_sg, w_gd=w_gd, w_merge=w_merge, w_uq=w_uq, w_uk=w_uk, w_uv=w_uv,
                w_branch=w_branch.astype(BF16), w_out=w_out.astype(BF16))


def _sg_mix_weights(sg_w, sg_b, C):
    reps = SG_CHUNK // C
    w = jnp.tril(sg_w[:, :, :C, :C])
    if reps > 1:
        eye = jnp.eye(reps, dtype=w.dtype)
        w = jnp.einsum('ab,dgij->dgaibj', eye, w).reshape(w.shape[0], SG_GROUPS, SG_CHUNK, SG_CHUNK)
    b = jnp.tile(sg_b[:, :, :C], (1, 1, reps))
    bs = jnp.broadcast_to(b[..., None], b.shape + (SG_GDIM,))
    return w.astype(BF16), bs.astype(F32)


def _layer_stream(x, l, W, P, *, B, T, tabs, sample):
    cosr, sinr, cosm, sinm = tabs
    act = F32 if sample is not None else BF16
    ya, s_ret = _ret_call(x, W['w_ret'][l], cosr, sinr, P['ret_gn_g'][l][None],
                          sample['state_ret'][l] if sample else None, B=B, T=T, out_dtype=act)
    mo = _mlap_call(x, W['w_mla'][l], W['w_uq'][l], P['mla_q_norm'][l][None],
                    P['mla_kv_norm'][l][None], cosm, sinm, act_dtype=act, emit_kt=sample is None)
    qn, qpe, ckv, kpe, gm = mo[:5]
    if sample:
        yb = _attn_decode_call(sample['page_table'], qn, qpe, ckv, kpe, gm, W['w_uk'][l], W['w_uv'][l],
                               sample['cache_ckv'], sample['cache_kpet'], B=B, T=T, layer=l)
    else:
        yb = _attn_prefill_call(qn, qpe, mo[5], mo[6], W['w_uk'][l], gm, W['w_uv'][l], B=B, T=T)
    wt, bs = (P['sg_wt_s'], P['sg_bs_s']) if sample else (P['sg_wt_p'], P['sg_bs_p'])
    sg_out = _sg_call(x, W['w_sg'][l], P['sg_ln_g'][l][None], P['sg_ln_b'][l][None], wt[l], bs[l],
                      emit_v=sample is not None, out_dtype=act)
    yc = sg_out[0]
    sgv = sg_out[1] if sample else None
    yd, s_gd, conv_new = _gd_call(x, W['w_gd'][l], P['gd_conv_w'][l], P['gd_av'][l], P['gd_dtb'][l],
                                  P['gd_norm_g'][l][None],
                                  sample['bufp'][l] if sample else None,
                                  sample['state_delta'][l] if sample else None, B=B, T=T, out_dtype=act)
    xn = _back_call(x, ya, yb, yc, yd, W['w_merge'][l], W['w_branch'][l], W['w_out'][l],
                    P['ln_g'][l][None], P['ln_b'][l][None], alpha=P['alpha'])
    return xn, ckv, kpe, s_ret, s_gd, conv_new, sgv


def kernel(x_prompt, x_sample, cache_ckv, cache_kpe, state_ret, state_delta, state_conv, page_table,
           w_in, ret_gn_g, mla_q_norm, mla_w_uq, mla_kv_norm, mla_w_uk, mla_w_uv,
           sg_ln_g, sg_ln_b, sg_w, sg_b, gd_conv_w, gd_a_log, gd_dt_bias, gd_norm_g,
           w_branch, w_out, ln_g, ln_b):
    depth = w_in.shape[0]
    bp, tp, _ = x_prompt.shape
    bs_, ts, _ = x_sample.shape
    past_len = page_table.shape[1] * cache_ckv.shape[2]
    W = _prep_weights(w_in, mla_w_uq, mla_w_uk, mla_w_uv, w_branch, w_out)
    pad_small = lambda v: jnp.pad(v, ((0, 0), (0, LANES - v.shape[-1])))[:, None, :]
    P = dict(ret_gn_g=ret_gn_g, mla_q_norm=mla_q_norm, mla_kv_norm=mla_kv_norm, sg_ln_g=sg_ln_g, sg_ln_b=sg_ln_b,
             gd_conv_w=gd_conv_w, gd_norm_g=gd_norm_g, ln_g=ln_g, ln_b=ln_b,
             gd_av=pad_small(-jnp.exp(gd_a_log)), gd_dtb=pad_small(gd_dt_bias),
             alpha=float((2.0 * depth) ** 0.25))
    P['sg_wt_p'], P['sg_bs_p'] = _sg_mix_weights(sg_w, sg_b, min(tp, SG_CHUNK))
    P['sg_wt_s'], P['sg_bs_s'] = _sg_mix_weights(sg_w, sg_b, min(ts, SG_CHUNK))

    pos_p = jnp.arange(tp)
    pos_s = past_len + jnp.arange(ts)
    tab_rows_s = 512
    tabs_p = _rope_tables(pos_p, RET_DK, RET_HEADS, tp) + _rope_tables(pos_p, MLA_ROPE, MLA_HEADS, tp)
    tabs_s = _rope_tables(pos_s, RET_DK, RET_HEADS, tab_rows_s) + _rope_tables(pos_s, MLA_ROPE, MLA_HEADS, tab_rows_s)

    bufp = jnp.pad(state_conv, ((0, 0), (0, 0), (ts - (GD_CONV - 1), 0), (0, 0))).reshape(depth, bs_ * ts, GD_CONV_CH)
    sample = dict(state_ret=state_ret, state_delta=state_delta, bufp=bufp, page_table=page_table,
                  cache_ckv=cache_ckv, cache_kpet=jnp.swapaxes(cache_kpe, 2, 3))

    xp = x_prompt.reshape(bp * tp, D_MODEL)
    xs = x_sample.reshape(bs_ * ts, D_MODEL)
    outs_p, outs_s = [], []
    for l in range(depth):
        xp, *rest = _layer_stream(xp, l, W, P, B=bp, T=tp, tabs=tabs_p, sample=None)
        outs_p.append(rest)
        xs, *rest = _layer_stream(xs, l, W, P, B=bs_, T=ts, tabs=tabs_s, sample=sample)
        outs_s.append(rest)

    def stack(outs, k, shape):
        return jnp.stack([o[k].reshape(shape) for o in outs])

    return (xp.reshape(bp, tp, D_MODEL), xs.reshape(bs_, ts, D_MODEL),
            stack(outs_p, 0, (bp, tp, KV_LORA)), stack(outs_p, 1, (bp, tp, MLA_ROPE)),
            stack(outs_p, 2, (bp, RET_HEADS, RET_DK, RET_DV)), stack(outs_p, 3, (bp, GD_HEADS, GD_DK, GD_DV)),
            stack(outs_p, 4, (bp, GD_CONV - 1, GD_CONV_CH)),
            stack(outs_s, 0, (bs_, ts, KV_LORA)), stack(outs_s, 1, (bs_, ts, MLA_ROPE)),
            stack(outs_s, 2, (bs_, RET_HEADS, RET_DK, RET_DV)), stack(outs_s, 3, (bs_, GD_HEADS, GD_DK, GD_DV)),
            stack(outs_s, 4, (bs_, GD_CONV - 1, GD_CONV_CH)), stack(outs_s, 5, (bs_, ts, SG_W)))
```

```python
import functools
import math

import numpy as np
import jax
import jax.numpy as jnp
from jax import lax
from jax.experimental import pallas as pl
from jax.experimental.pallas import tpu as pltpu

F32 = jnp.float32
BF16 = jnp.bfloat16

D_MODEL = 1024
RET_HEADS, RET_DK, RET_DV = 4, 64, 128
RET_W = RET_HEADS * RET_DV
MLA_HEADS, MLA_NOPE, MLA_ROPE, MLA_V = 8, 64, 32, 64
Q_LORA, KV_LORA = 384, 256
MLA_W = MLA_HEADS * MLA_V
MLA_SCALE = (MLA_NOPE + MLA_ROPE) ** -0.5
SG_GROUPS, SG_GDIM, SG_CHUNK = 4, 128, 128
SG_W = SG_GROUPS * SG_GDIM
GD_HEADS, GD_DK, GD_DV, GD_CONV = 4, 128, 128, 4
GD_W = GD_HEADS * GD_DV
GD_CONV_CH = 2 * GD_HEADS * GD_DK + GD_W
N_BRANCH, BRANCH_W = 4, 512
LIN_CHUNK = 64
ROPE_BASE = 10000.0
EPS = 1e-6
NEG = -1e30
IN_SPLITS = (RET_HEADS * RET_DK, RET_HEADS * RET_DK, RET_W, RET_W,
             Q_LORA, KV_LORA, MLA_ROPE, MLA_W,
             SG_W, SG_W, SG_W,
             GD_CONV_CH, GD_HEADS, GD_HEADS, GD_W,
             N_BRANCH * D_MODEL)
_OFF = np.concatenate([[0], np.cumsum(IN_SPLITS)]).tolist()

LANES = 128
KCAT_W = KV_LORA + LANES
ATT_TK = 256
ATT_LAG = 8
ATT_UNROLL = 2
GD_RB = 256
DEC_GROUP = 32
QSCALE = MLA_SCALE * math.log2(math.e)
VMEM_LIMIT = 56 * 1024 * 1024

_TB = (((1,), (1,)), ((), ()))
_TA = (((0,), (0,)), ((), ()))


def _dot(a, b):
    return jnp.dot(a, b, preferred_element_type=F32)


def _dot_tb(a, b):
    return lax.dot_general(a, b, _TB, preferred_element_type=F32)


def _dot_ta(a, b):
    return lax.dot_general(a, b, _TA, preferred_element_type=F32)


def _split2(a):
    hi = a.astype(BF16)
    lo = (a - hi.astype(F32)).astype(BF16)
    return hi, lo


def _dot3(a, b):
    ah, al = _split2(a)
    bh, bl = _split2(b)
    return _dot(ah, bh) + (_dot(ah, bl) + _dot(al, bh))


def _dot_mask(mask_bf16, x):
    hi = x.astype(BF16)
    r1 = x - hi.astype(F32)
    mid = r1.astype(BF16)
    lo = (r1 - mid.astype(F32)).astype(BF16)
    return _dot(mask_bf16, hi) + (_dot(mask_bf16, mid) + _dot(mask_bf16, lo))


def _silu(x):
    return x * jax.nn.sigmoid(x)


def _gelu(x):
    return 0.5 * x * (1.0 + lax.erf(x * np.float32(math.sqrt(0.5))))


def _rope(x, cos, sin_signed, d):
    w = x.shape[-1]
    half = d // 2
    lane = lax.broadcasted_iota(jnp.int32, x.shape, 1)
    first = (lane % d) < half
    rot = jnp.where(first, pltpu.roll(x, w - half, 1), pltpu.roll(x, half, 1))
    return x * cos + rot * sin_signed


def _const_spec(shape):
    nd = len(shape)
    return pl.BlockSpec(shape, lambda *_: (0,) * nd, pipeline_mode=pl.Buffered(1))


def _layer_spec(a, layer):
    nd = a.ndim - 1
    return pl.BlockSpec((None,) + a.shape[1:], lambda *_: (layer,) + (0,) * nd, pipeline_mode=pl.Buffered(1))


def _params(sem):
    return pltpu.CompilerParams(dimension_semantics=sem, vmem_limit_bytes=VMEM_LIMIT)


def _ret_kernel(*refs, R, NS, RB, cdec, has_s0):
    if has_s0:
        (x_ref, w_ref, cos_ref, sin_ref, gn_ref, dmat_ref, qdec_ref, kdec_ref, s0_ref,
         y_ref, sfin_ref) = refs
    else:
        (x_ref, w_ref, cos_ref, sin_ref, gn_ref, dmat_ref, qdec_ref, kdec_ref,
         y_ref, sfin_ref) = refs
    C = R // NS
    h = _dot(x_ref[...].astype(BF16), w_ref[...])
    cos = cos_ref[...]
    sin = sin_ref[...]
    hq = RET_HEADS * RET_DK
    q = _rope(h[:, 0:hq], cos, sin, RET_DK)
    k = _rope(h[:, hq:2 * hq], cos, sin, RET_DK) * (RET_DK ** -0.5)
    v = h[:, 2 * hq:2 * hq + RET_W]
    g = _silu(h[:, 2 * hq + RET_W:2 * hq + 2 * RET_W])

    if has_s0:
        sfin_ref[...] = s0_ref[...]
    else:
        @pl.when(pl.program_id(1) == 0)
        def _():
            sfin_ref[...] = jnp.zeros(sfin_ref.shape, F32)

    probs = [(c, hh) for c in range(RB // R) for hh in range(RET_HEADS)]
    qh, kh, vh, sc, kv = {}, {}, {}, {}, {}
    for pr in probs:
        c, hh = pr
        rs = slice(c * R, (c + 1) * R)
        qh[pr] = q[rs, hh * RET_DK:(hh + 1) * RET_DK].astype(BF16)
        kh[pr] = k[rs, hh * RET_DK:(hh + 1) * RET_DK].astype(BF16)
        vh[pr] = v[rs, hh * RET_DV:(hh + 1) * RET_DV]
        sc[pr] = _dot_tb(qh[pr], kh[pr])
        vk = (vh[pr] * kdec_ref[hh]).astype(BF16)
        for s in range(NS):
            kv[pr, s] = _dot_ta(kh[pr][s * C:(s + 1) * C], vk[s * C:(s + 1) * C])
    sprev = {}
    for pr in probs:
        c, hh = pr
        for s in range(NS):
            sidx = c * NS + s if has_s0 else 0
            st = sfin_ref[sidx, hh]
            sprev[pr, s] = st.astype(BF16)
            sfin_ref[sidx, hh] = st * cdec[hh] + kv[pr, s]
    for pr in probs:
        c, hh = pr
        rs = slice(c * R, (c + 1) * R)
        vs = slice(hh * RET_DV, (hh + 1) * RET_DV)
        o = _dot((sc[pr] * dmat_ref[hh]).astype(BF16), vh[pr].astype(BF16))
        qd = qdec_ref[hh]
        parts = [_dot(qh[pr][s * C:(s + 1) * C], sprev[pr, s]) * qd[s * C:(s + 1) * C] for s in range(NS)]
        o = o + (parts[0] if NS == 1 else jnp.concatenate(parts, axis=0))
        mu = jnp.mean(o, axis=-1, keepdims=True)
        d = o - mu
        var = jnp.mean(d * d, axis=-1, keepdims=True)
        on = d * lax.rsqrt(var + EPS) * gn_ref[:, vs]
        y_ref[rs, vs] = (g[rs, vs] * on).astype(y_ref.dtype)


def _ret_consts(R, NS):
    C = R // NS
    lg = np.log1p(-np.exp2(-5.0 - np.arange(RET_HEADS, dtype=np.float64)))
    idx = np.arange(R)
    tok = idx % C
    same = (idx[:, None] // C) == (idx[None, :] // C)
    rel = tok[:, None] - tok[None, :]
    low = same & (rel >= 0)
    dmat = np.where(low[None], np.exp(lg[:, None, None] * np.where(low, rel, 0)[None]), 0.0)
    qdec = np.exp(lg[:, None] * (tok[None, :] + 1.0))
    kdec = np.exp(lg[:, None] * (C - 1.0 - tok[None, :]))
    cdec = tuple(float(v) for v in np.exp(lg * C))
    bc = lambda a: jnp.asarray(np.broadcast_to(a[:, :, None], (RET_HEADS, R, LANES)), F32)
    return jnp.asarray(dmat, F32), bc(qdec), bc(kdec), cdec


def _ret_call(x, w, cos, sin, gn, s0, *, B, T, layer, out_dtype):
    N = x.shape[0]
    has_s0 = s0 is not None
    R = LIN_CHUNK
    if has_s0:
        C = math.gcd(T, LIN_CHUNK)
        assert C == T, "sample stream is a single chunk per sequence"
        NS = R // C
        RB = 2 * R
        nb, nt = N // RB, 1
        nsb = RB // C
    else:
        NS, RB = 1, 512
        assert T % RB == 0 and math.gcd(T, LIN_CHUNK) == R
        nb, nt = B, T // RB
        nsb = 1
    dmat, qdec, kdec, cdec = _ret_consts(R, NS)
    tr = cos.shape[0] // RB
    hq = RET_HEADS * RET_DK
    in_specs = [
        pl.BlockSpec((RB, D_MODEL), lambda b, t: (b * nt + t, 0)),
        _layer_spec(w, layer),
        pl.BlockSpec((RB, hq), lambda b, t: ((b * nt + t) % tr, 0)),
        pl.BlockSpec((RB, hq), lambda b, t: ((b * nt + t) % tr, 0)),
        _const_spec(gn.shape), _const_spec(dmat.shape), _const_spec(qdec.shape), _const_spec(kdec.shape),
    ]
    args = [x, w, cos, sin, gn, dmat, qdec, kdec]
    st_spec = pl.BlockSpec((nsb, RET_HEADS, RET_DK, RET_DV), lambda b, t: (b, 0, 0, 0))
    if has_s0:
        in_specs.append(pl.BlockSpec((None, nsb, RET_HEADS, RET_DK, RET_DV), lambda b, t: (layer, b, 0, 0, 0)))
        args.append(s0)
    nseq = N // T
    return pl.pallas_call(
        functools.partial(_ret_kernel, R=R, NS=NS, RB=RB, cdec=cdec, has_s0=has_s0),
        grid=(nb, nt),
        in_specs=in_specs,
        out_specs=[pl.BlockSpec((RB, RET_W), lambda b, t: (b * nt + t, 0)), st_spec],
        out_shape=[jax.ShapeDtypeStruct((N, RET_W), out_dtype),
                   jax.ShapeDtypeStruct((nseq, RET_HEADS, RET_DK, RET_DV), F32)],
        compiler_params=_params(("arbitrary", "arbitrary")),
        name="ret",
    )(*args)


def _mlap_kernel(x_ref, w_ref, wuq_ref, qn_ref, kvn_ref, cos_ref, sin_ref, *outs, TM, emit_kt):
    if emit_kt:
        qnope_ref, qpe_ref, ckv_ref, kpe_ref, gm_ref, kt_ref, vb_ref = outs
    else:
        qnope_ref, qpe_ref, ckv_ref, kpe_ref, gm_ref = outs
    h = _dot(x_ref[...].astype(BF16), w_ref[...])
    cq = h[:, 0:Q_LORA]
    cq = cq * lax.rsqrt(jnp.mean(cq * cq, axis=-1, keepdims=True) + EPS) * qn_ref[...]
    qf = _dot(cq.astype(BF16), wuq_ref[...])
    nw = MLA_HEADS * MLA_NOPE
    qnope_ref[...] = (qf[:, 0:nw] * QSCALE).astype(qnope_ref.dtype)
    cos = cos_ref[...]
    sin = sin_ref[...]
    qpe_ref[...] = _rope(qf[:, nw:nw + MLA_HEADS * MLA_ROPE], cos, sin, MLA_ROPE) * QSCALE
    c0 = Q_LORA
    kv = h[:, c0:c0 + KV_LORA]
    ckv = kv * lax.rsqrt(jnp.mean(kv * kv, axis=-1, keepdims=True) + EPS) * kvn_ref[...]
    ckv_ref[...] = ckv
    g0 = c0 + KV_LORA
    gm_ref[...] = _silu(h[:, g0:g0 + MLA_W]).astype(gm_ref.dtype)
    p0 = g0 + MLA_W
    kblk = _rope(h[:, p0:p0 + LANES], cos[:, 0:LANES], sin[:, 0:LANES], MLA_ROPE)
    kpe_ref[...] = kblk[:, 0:MLA_ROPE]
    if emit_kt:
        rep = kblk
        for i in range(1, LANES // MLA_ROPE):
            rep = rep + pltpu.roll(kblk, i * MLA_ROPE, 1)
        kct = jnp.concatenate([ckv, rep], axis=1).T
        for i in range(TM // ATT_TK):
            kt_ref[i] = kct[:, i * ATT_TK:(i + 1) * ATT_TK].astype(BF16)
        vb_ref[...] = ckv.astype(BF16)


def _mlap_call(x, w, wuq, qn, kvn, cos, sin, *, layer, act_dtype, emit_kt):
    N = x.shape[0]
    TM = 512
    assert N % TM == 0 and TM % ATT_TK == 0
    tr = cos.shape[0] // TM
    row = lambda wd: pl.BlockSpec((TM, wd), lambda i: (i, 0))
    tab = pl.BlockSpec((TM, MLA_HEADS * MLA_ROPE), lambda i: (i % tr, 0))
    nw = MLA_HEADS * MLA_NOPE
    out_specs = [row(nw), row(MLA_HEADS * MLA_ROPE), row(KV_LORA), row(MLA_ROPE), row(MLA_W)]
    out_shape = [jax.ShapeDtypeStruct((N, nw), act_dtype),
                 jax.ShapeDtypeStruct((N, MLA_HEADS * MLA_ROPE), F32),
                 jax.ShapeDtypeStruct((N, KV_LORA), F32),
                 jax.ShapeDtypeStruct((N, MLA_ROPE), F32),
                 jax.ShapeDtypeStruct((N, MLA_W), act_dtype)]
    if emit_kt:
        out_specs += [pl.BlockSpec((TM // ATT_TK, KCAT_W, ATT_TK), lambda i: (i, 0, 0)), row(KV_LORA)]
        out_shape += [jax.ShapeDtypeStruct((N // ATT_TK, KCAT_W, ATT_TK), BF16),
                      jax.ShapeDtypeStruct((N, KV_LORA), BF16)]
    return pl.pallas_call(
        functools.partial(_mlap_kernel, TM=TM, emit_kt=emit_kt),
        grid=(N // TM,),
        in_specs=[row(D_MODEL), _layer_spec(w, layer), _layer_spec(wuq, layer), _const_spec(qn.shape),
                  _const_spec(kvn.shape), tab, tab],
        out_specs=out_specs,
        out_shape=out_shape,
        compiler_params=_params(("arbitrary",)),
        name="mlap",
    )(x, w, wuq, qn, kvn, cos, sin)


def _attn_prefill_kernel(qn_ref, qpe_ref, kt_ref, v_ref, wuk_ref, gm_ref, wuv_ref, y_ref,
                         q_s, m_s, l_s, acc_s, *, BQ, TK):
    i = pl.program_id(1)
    H = MLA_HEADS
    lane = lax.broadcasted_iota(jnp.int32, (BQ, LANES), 1)
    for hh in range(H):
        ql = _dot(qn_ref[:, hh * MLA_NOPE:(hh + 1) * MLA_NOPE], wuk_ref[hh])
        q_s[hh * BQ:(hh + 1) * BQ, 0:KV_LORA] = ql.astype(BF16)
        per = LANES // MLA_ROPE
        blk = qpe_ref[:, (hh // per) * LANES:(hh // per + 1) * LANES]
        off = (hh % per) * MLA_ROPE
        keep = (lane >= off) & (lane < off + MLA_ROPE)
        q_s[hh * BQ:(hh + 1) * BQ, KV_LORA:KCAT_W] = jnp.where(keep, blk, 0.0).astype(BF16)
    m_s[...] = jnp.full(m_s.shape, NEG, F32)
    l_s[...] = jnp.zeros(l_s.shape, F32)
    acc_s[...] = jnp.zeros(acc_s.shape, F32)

    def steps(blocks):
        kts = [kt_ref[j] for j, _ in blocks]
        vbs = [v_ref[pl.ds(pl.multiple_of(j * TK, TK), TK), :] for j, _ in blocks]
        row = lax.broadcasted_iota(jnp.int32, (BQ, TK), 0)
        col = lax.broadcasted_iota(jnp.int32, (BQ, TK), 1)
        items = [(bi, hh) for bi in range(len(blocks)) for hh in range(H)]
        scores = {}
        for t in range(len(items) + ATT_LAG):
            if t < len(items):
                bi, hh = items[t]
                scores[t] = _dot(q_s[hh * BQ:(hh + 1) * BQ, :], kts[bi])
            if t < ATT_LAG:
                continue
            bi, hh = items[t - ATT_LAG]
            rs = slice(hh * BQ, (hh + 1) * BQ)
            s = scores.pop(t - ATT_LAG)
            if blocks[bi][1]:
                s = jnp.where(blocks[bi][0] * TK + col <= i * BQ + row, s, NEG)
            m_prev = m_s[rs, :]
            m_new = jnp.maximum(m_prev, jnp.max(s, axis=-1, keepdims=True))
            alpha = jnp.exp2(m_prev - m_new)
            p = jnp.exp2(s - jnp.concatenate([m_new] * (TK // LANES), axis=1))
            l_s[rs, :] = alpha[:, 0:1] * l_s[rs, :] + jnp.sum(p, axis=-1, keepdims=True)
            acc_s[rs, :] = (jnp.concatenate([alpha] * (KV_LORA // LANES), axis=1) * acc_s[rs, :]
                            + _dot(p.astype(BF16), vbs[bi]))
            m_s[rs, :] = m_new

    last = (i * BQ + BQ - 1) // TK
    npair = last // ATT_UNROLL

    def body(jj, carry):
        steps([(jj * ATT_UNROLL + u, False) for u in range(ATT_UNROLL)])
        return carry

    lax.fori_loop(0, npair, body, 0)

    def tail(j, carry):
        steps([(j, False)])
        return carry

    lax.fori_loop(npair * ATT_UNROLL, last, tail, 0)
    steps([(last, True)])

    for p2 in range(H // 2):
        halves = []
        for hh in (2 * p2, 2 * p2 + 1):
            rs = slice(hh * BQ, (hh + 1) * BQ)
            halves.append((acc_s[rs, :] / l_s[rs, :]).astype(BF16))
        yb = _dot(jnp.concatenate(halves, axis=1), wuv_ref[p2])
        y_ref[:, p2 * LANES:(p2 + 1) * LANES] = (gm_ref[:, p2 * LANES:(p2 + 1) * LANES].astype(F32) * yb).astype(y_ref.dtype)


def _attn_prefill_call(qn, qpe, kt, vb, wuk, gm, wuv, *, B, T, layer):
    N = qn.shape[0]
    BQ, TK = 128, ATT_TK
    assert T % TK == 0 and T % BQ == 0
    nq = T // BQ
    H = MLA_HEADS
    row = lambda wd: pl.BlockSpec((BQ, wd), lambda b, i: (b * nq + i, 0))
    return pl.pallas_call(
        functools.partial(_attn_prefill_kernel, BQ=BQ, TK=TK),
        grid=(B, nq),
        in_specs=[row(H * MLA_NOPE), row(H * MLA_ROPE),
                  pl.BlockSpec((T // TK, KCAT_W, TK), lambda b, i: (b, 0, 0)),
                  pl.BlockSpec((T, KV_LORA), lambda b, i: (b, 0)),
                  _layer_spec(wuk, layer), row(MLA_W), _layer_spec(wuv, layer)],
        out_specs=row(MLA_W),
        out_shape=jax.ShapeDtypeStruct((N, MLA_W), BF16),
        scratch_shapes=[pltpu.VMEM((H * BQ, KCAT_W), BF16), pltpu.VMEM((H * BQ, LANES), F32),
                        pltpu.VMEM((H * BQ, 1), F32), pltpu.VMEM((H * BQ, KV_LORA), F32)],
        compiler_params=_params(("arbitrary", "arbitrary")),
        name="attn_prefill",
    )(qn, qpe, kt, vb, wuk, gm, wuv)


def _attn_decode_kernel(pt_ref, qn_ref, qpe_ref, ckvn_ref, kpen_ref, gm_ref, wuk_ref, wuv_ref,
                        cckv_ref, ckpe_ref, y_ref,
                        kbuf, pbuf, sem, *, T, G, NSTEP, NSUB, LOOK, layer, PAGE):
    b = pl.program_id(0)
    nb = pl.num_programs(0)
    H = MLA_HEADS

    def copies(bb, jj):
        out = []
        for g in range(G):
            page = pt_ref[bb, jj * G + g]
            out.append(pltpu.make_async_copy(cckv_ref.at[layer, page], kbuf.at[jj, g], sem.at[0, jj]))
            out.append(pltpu.make_async_copy(ckpe_ref.at[layer, page], pbuf.at[jj, g], sem.at[1, jj]))
        return out

    @pl.when(b == 0)
    def _():
        for jj in range(LOOK):
            for c in copies(b, jj):
                c.start()

    ql = jnp.concatenate([_dot(qn_ref[:, hh * MLA_NOPE:(hh + 1) * MLA_NOPE].astype(BF16), wuk_ref[hh])
                          for hh in range(H)], axis=0).astype(BF16)
    qp = jnp.concatenate([qpe_ref[:, hh * MLA_ROPE:(hh + 1) * MLA_ROPE] for hh in range(H)], axis=0).astype(BF16)
    kn = ckvn_ref[...].astype(BF16)
    s = _dot_tb(ql, kn) + _dot_tb(qp, kpen_ref[...].astype(BF16))
    row = lax.broadcasted_iota(jnp.int32, s.shape, 0)
    col = lax.broadcasted_iota(jnp.int32, s.shape, 1)
    s = jnp.where(col <= row % T, s, NEG)
    m = jnp.max(s, axis=-1, keepdims=True)
    p = jnp.exp2(s - m)
    l = jnp.sum(p, axis=-1, keepdims=True)
    acc = _dot(p.astype(BF16), kn)

    GS = G // NSUB
    for j in range(NSTEP):
        jn = j + LOOK
        if jn < NSTEP:
            for c in copies(b, jn):
                c.start()
        else:
            @pl.when(b + 1 < nb)
            def _():
                for c in copies(b + 1, jn - NSTEP):
                    c.start()
        for c in copies(b, j):
            c.wait()
        kbs, scores = [], []
        for u in range(NSUB):
            kb = kbuf[j, u * GS:(u + 1) * GS].reshape(GS * PAGE, KV_LORA).astype(BF16)
            pcat = jnp.concatenate([pbuf[j, u * GS + g] for g in range(GS)], axis=1).astype(BF16)
            kbs.append(kb)
            scores.append(_dot_tb(ql, kb) + _dot(qp, pcat))
        for u in range(NSUB):
            kb, s = kbs[u], scores[u]
            m_new = jnp.maximum(m, jnp.max(s, axis=-1, keepdims=True))
            alpha = jnp.exp2(m - m_new)
            p = jnp.exp2(s - m_new)
            l = alpha * l + jnp.sum(p, axis=-1, keepdims=True)
            acc = alpha * acc + _dot(p.astype(BF16), kb)
            m = m_new

    o = acc / l
    for p2 in range(H // 2):
        lhs = jnp.concatenate([o[(2 * p2) * T:(2 * p2 + 1) * T], o[(2 * p2 + 1) * T:(2 * p2 + 2) * T]], axis=1)
        yb = _dot(lhs.astype(BF16), wuv_ref[p2])
        y_ref[:, p2 * LANES:(p2 + 1) * LANES] = gm_ref[:, p2 * LANES:(p2 + 1) * LANES] * yb


def _attn_decode_call(page_table, qn, qpe, ckv, kpe, gm, wuk, wuv, cache_ckv, cache_kpet, *, B, T, layer):
    N = qn.shape[0]
    n_pages = page_table.shape[1]
    PAGE = cache_ckv.shape[2]
    G = math.gcd(n_pages, DEC_GROUP)
    if n_pages // G < 2 and G % 2 == 0:
        G //= 2
    NSTEP = n_pages // G
    assert NSTEP >= 2, "the page groups of one sample are double buffered"
    page_bytes = PAGE * (KV_LORA + MLA_ROPE) * 4
    assert n_pages * page_bytes <= VMEM_LIMIT // 2, "one sample's pages must fit the VMEM buffers"
    LOOK = min(2, NSTEP - 1)
    NSUB = math.gcd(G, 4)
    H = MLA_HEADS
    row = lambda wd: pl.BlockSpec((T, wd), lambda b, pt: (b, 0))
    grid_spec = pltpu.PrefetchScalarGridSpec(
        num_scalar_prefetch=1,
        grid=(B,),
        in_specs=[row(H * MLA_NOPE), row(H * MLA_ROPE), row(KV_LORA), row(MLA_ROPE), row(MLA_W),
                  _layer_spec(wuk, layer), _layer_spec(wuv, layer),
                  pl.BlockSpec(memory_space=pl.ANY), pl.BlockSpec(memory_space=pl.ANY)],
        out_specs=row(MLA_W),
        scratch_shapes=[pltpu.VMEM((NSTEP, G, PAGE, KV_LORA), F32), pltpu.VMEM((NSTEP, G, MLA_ROPE, PAGE), F32),
                        pltpu.SemaphoreType.DMA((2, NSTEP))],
    )
    return pl.pallas_call(
        functools.partial(_attn_decode_kernel, T=T, G=G, NSTEP=NSTEP, NSUB=NSUB, LOOK=LOOK, layer=layer, PAGE=PAGE),
        grid_spec=grid_spec,
        out_shape=jax.ShapeDtypeStruct((N, MLA_W), F32),
        compiler_params=_params(("arbitrary",)),
        name="attn_decode",
    )(page_table, qn, qpe, ckv, kpe, gm, wuk, wuv, cache_ckv, cache_kpet)


def _sg_kernel(x_ref, w_ref, lng_ref, lnb_ref, wt_ref, bs_ref, y_ref, *rest, TM, emit_v):
    h = _dot(x_ref[...].astype(BF16), w_ref[...])
    u = _gelu(h[:, 0:SG_W])
    gv = _gelu(h[:, SG_W:2 * SG_W])
    mu = jnp.mean(gv, axis=-1, keepdims=True)
    d = gv - mu
    var = jnp.mean(d * d, axis=-1, keepdims=True)
    v = d * lax.rsqrt(var + EPS) * lng_ref[...] + lnb_ref[...]
    if emit_v:
        rest[0][...] = v
    gs = _silu(h[:, 2 * SG_W:3 * SG_W])
    vb = v.astype(BF16)
    for c in range(TM // SG_CHUNK):
        rows = slice(c * SG_CHUNK, (c + 1) * SG_CHUNK)
        for g in range(SG_GROUPS):
            cols = slice(g * SG_GDIM, (g + 1) * SG_GDIM)
            s = _dot(wt_ref[g], vb[rows, cols]) + bs_ref[g]
            y_ref[rows, cols] = (gs[rows, cols] * (u[rows, cols] * s)).astype(y_ref.dtype)


def _sg_call(x, w, lng, lnb, wt, bs, *, layer, emit_v, out_dtype):
    N = x.shape[0]
    TM = 512
    assert N % TM == 0
    row = lambda wd: pl.BlockSpec((TM, wd), lambda i: (i, 0))
    out_specs = [row(SG_W)]
    out_shape = [jax.ShapeDtypeStruct((N, SG_W), out_dtype)]
    if emit_v:
        out_specs.append(row(SG_W))
        out_shape.append(jax.ShapeDtypeStruct((N, SG_W), F32))
    return pl.pallas_call(
        functools.partial(_sg_kernel, TM=TM, emit_v=emit_v),
        grid=(N // TM,),
        in_specs=[row(D_MODEL), _layer_spec(w, layer), _const_spec(lng.shape), _const_spec(lnb.shape),
                  _layer_spec(wt, layer), _layer_spec(bs, layer)],
        out_specs=out_specs,
        out_shape=out_shape,
        compiler_params=_params(("arbitrary",)),
        name="sg",
    )(x, w, lng, lnb, wt, bs)


def _gd_kernel(*refs, R, NS, RB, has_s0):
    if has_s0:
        (x_ref, w_ref, cw_ref, av_ref, dtb_ref, ng_ref, msk_ref, bufp_ref, s0_ref,
         y_ref, sfin_ref, conv_ref) = refs
    else:
        (x_ref, w_ref, cw_ref, av_ref, dtb_ref, ng_ref, msk_ref,
         y_ref, sfin_ref, conv_ref, carry_s) = refs
    C = R // NS
    J = max(1, int(math.ceil(math.log2(C))))
    CH = GD_CONV_CH
    h = _dot(x_ref[...].astype(BF16), w_ref[...])
    xq = h[:, 0:CH]
    y = xq * cw_ref[GD_CONV - 1:GD_CONV, :]
    if has_s0:
        x3 = xq.reshape(RB // 8, 8, CH)
        b3 = bufp_ref[...].reshape(RB // 8, 8, CH)
        tpos = lax.broadcasted_iota(jnp.int32, x3.shape, 1)
        for k in range(1, GD_CONV):
            sh = jnp.where(tpos >= k, pltpu.roll(x3, k, 1), pltpu.roll(b3, k, 1))
            y = y + sh.reshape(RB, CH) * cw_ref[GD_CONV - 1 - k:GD_CONV - k, :]
        for s in range(RB // C):
            conv_ref[s] = xq[s * C + C - (GD_CONV - 1):s * C + C, :]
    else:
        @pl.when(pl.program_id(1) == 0)
        def _():
            carry_s[...] = jnp.zeros(carry_s.shape, F32)
        cprev = carry_s[...]
        row8 = lax.broadcasted_iota(jnp.int32, (8, CH), 0)
        for k in range(1, GD_CONV):
            big = pltpu.roll(xq, k, 0)
            head = jnp.where(row8 < k, pltpu.roll(cprev, k, 0), big[0:8])
            sh = jnp.concatenate([head, big[8:]], axis=0)
            y = y + sh * cw_ref[GD_CONV - 1 - k:GD_CONV - k, :]
        carry_s[...] = xq[RB - 8:RB]
        conv_ref[0] = xq[RB - (GD_CONV - 1):RB, :]
    qkv = _silu(y)
    hw = GD_HEADS * GD_DK
    qn, kn = [], []
    for hh in range(GD_HEADS):
        qb = qkv[:, hh * GD_DK:(hh + 1) * GD_DK]
        qn.append(qb * lax.rsqrt(jnp.sum(qb * qb, axis=-1, keepdims=True) + EPS) * (GD_DK ** -0.5))
        kb = qkv[:, hw + hh * GD_DK:hw + (hh + 1) * GD_DK]
        kn.append(kb * lax.rsqrt(jnp.sum(kb * kb, axis=-1, keepdims=True) + EPS))
    vv = qkv[:, 2 * hw:2 * hw + GD_W]
    gate = _silu(h[:, CH:CH + GD_W])
    ab = h[:, CH + GD_W:CH + GD_W + LANES]
    lane = lax.broadcasted_iota(jnp.int32, ab.shape, 1)
    gb = jnp.where(lane < GD_HEADS, av_ref[...] * jax.nn.softplus(ab + dtb_ref[...]), jax.nn.sigmoid(ab))

    if has_s0:
        sfin_ref[...] = s0_ref[...]
    else:
        @pl.when(pl.program_id(1) == 0)
        def _():
            sfin_ref[...] = jnp.zeros(sfin_ref.shape, F32)

    lowf = msk_ref[0]
    low = lowf > 0.0
    strictf = msk_ref[1]
    lowb = lowf.astype(BF16)
    sameb = msk_ref[2].astype(BF16)
    ri = lax.broadcasted_iota(jnp.int32, (R, R), 0)
    ci = lax.broadcasted_iota(jnp.int32, (R, R), 1)
    eye = (ri == ci).astype(F32)

    nch = RB // R
    probs = [(c, hh) for c in range(nch) for hh in range(GD_HEADS)]
    gbc = [gb[c * R:(c + 1) * R] for c in range(nch)]
    gc4 = [_dot_mask(lowb, g) for g in gbc]
    gl4 = [_dot_mask(sameb, g) for g in gbc]
    qh, kh, qkk = {}, {}, {}
    for pr in probs:
        c, hh = pr
        rs = slice(c * R, (c + 1) * R)
        qh[pr], kh[pr] = qn[hh][rs], kn[hh][rs]
        khb = kh[pr].astype(BF16)
        qkk[pr] = _dot_tb(jnp.concatenate([qh[pr].astype(BF16), khb], axis=0), khb)
    gcb, glb, beta, gam, qk, mp, pinv = {}, {}, {}, {}, {}, {}, {}
    for pr in probs:
        c, hh = pr
        gcb[pr] = jnp.broadcast_to(gc4[c][:, hh:hh + 1], (R, LANES))
        glb[pr] = jnp.broadcast_to(gl4[c][:, hh:hh + 1], (R, LANES))
        beta[pr] = jnp.broadcast_to(gbc[c][:, GD_HEADS + hh:GD_HEADS + hh + 1], (R, LANES))
        gi = gcb[pr][:, 0:R]
        gam[pr] = jnp.where(low, jnp.exp(jnp.where(low, gi - gi.T, 0.0)), 0.0)
        qk[pr] = (qkk[pr][0:R] * gam[pr]).astype(BF16)
        mp[pr] = -(strictf * (beta[pr][:, 0:R] * qkk[pr][R:2 * R] * gam[pr]))
        pinv[pr] = eye + mp[pr]
    for _ in range(J - 1):
        for pr in probs:
            mp[pr] = _dot3(mp[pr], mp[pr])
        for pr in probs:
            pinv[pr] = pinv[pr] + _dot3(pinv[pr], mp[pr])
    pre = {}
    for pr in probs:
        c, hh = pr
        vh = vv[c * R:(c + 1) * R, hh * GD_DK:(hh + 1) * GD_DK]
        eg = jnp.exp(gcb[pr])
        sol = _dot3(pinv[pr], jnp.concatenate([vh * beta[pr], kh[pr] * (beta[pr] * eg)], axis=1))
        pre[pr] = (sol[:, 0:GD_DV], sol[:, GD_DV:GD_DV + GD_DK].astype(BF16),
                   (qh[pr] * eg).astype(BF16), (kh[pr] * jnp.exp(glb[pr] - gcb[pr])).astype(BF16),
                   jnp.exp(glb[pr]))

    for c in range(nch):
        rs = slice(c * R, (c + 1) * R)
        seqs = [(hh, s) for hh in range(GD_HEADS) for s in range(NS)]
        st, sb, ws, os_ = {}, {}, {}, {}
        for hh, s in seqs:
            sidx = c * NS + s if has_s0 else 0
            st[hh, s] = sfin_ref[sidx, hh]
            sb[hh, s] = st[hh, s].astype(BF16)
        for hh, s in seqs:
            rows = slice(s * C, (s + 1) * C)
            u, wb, qd, kd, el = pre[c, hh]
            ws[hh, s] = _dot(wb[rows], sb[hh, s])
            os_[hh, s] = _dot(qd[rows], sb[hh, s])
        vnew = {}
        for hh in range(GD_HEADS):
            u = pre[c, hh][0]
            vns = [u[s * C:(s + 1) * C] - ws[hh, s] for s in range(NS)]
            vnew[hh] = vns[0] if NS == 1 else jnp.concatenate(vns, axis=0)
        for hh, s in seqs:
            rows = slice(s * C, (s + 1) * C)
            kd, el = pre[c, hh][3], pre[c, hh][4]
            sidx = c * NS + s if has_s0 else 0
            sfin_ref[sidx, hh] = (st[hh, s] * el[s * C:s * C + 1, :]
                                  + _dot_ta(kd[rows], vnew[hh][rows].astype(BF16)))
        for hh in range(GD_HEADS):
            sl = slice(hh * GD_DK, (hh + 1) * GD_DK)
            oi = [os_[hh, s] for s in range(NS)]
            o = (oi[0] if NS == 1 else jnp.concatenate(oi, axis=0)) + _dot(qk[c, hh], vnew[hh].astype(BF16))
            on = o * lax.rsqrt(jnp.mean(o * o, axis=-1, keepdims=True) + EPS) * ng_ref[...]
            y_ref[rs, sl] = (gate[rs, sl] * on).astype(y_ref.dtype)


def _gd_masks(R, NS):
    C = R // NS
    idx = np.arange(R)
    same = (idx[:, None] // C) == (idx[None, :] // C)
    low = same & (idx[:, None] >= idx[None, :])
    strict = same & (idx[:, None] > idx[None, :])
    return jnp.asarray(np.stack([low, strict, same]).astype(np.float32))


def _gd_call(x, w, cw, av, dtb, ng, bufp, s0, *, B, T, layer, out_dtype):
    N = x.shape[0]
    has_s0 = s0 is not None
    R = LIN_CHUNK
    CH = GD_CONV_CH
    if has_s0:
        C = math.gcd(T, LIN_CHUNK)
        assert C == T and T == 8 and T >= GD_CONV - 1
        NS = R // C
        RB = 2 * R
        nb, nt = N // RB, 1
        nsb = RB // C
    else:
        NS, RB = 1, GD_RB
        assert T % RB == 0 and math.gcd(T, LIN_CHUNK) == R
        nb, nt = B, T // RB
        nsb = 1
    msk = _gd_masks(R, NS)
    rowspec = lambda wd: pl.BlockSpec((RB, wd), lambda b, t: (b * nt + t, 0))
    in_specs = [rowspec(D_MODEL), _layer_spec(w, layer), _layer_spec(cw, layer), _const_spec(av.shape),
                _const_spec(dtb.shape), _const_spec(ng.shape), _const_spec(msk.shape)]
    args = [x, w, cw, av, dtb, ng, msk]
    st_spec = pl.BlockSpec((nsb, GD_HEADS, GD_DK, GD_DV), lambda b, t: (b, 0, 0, 0))
    scratch = []
    if has_s0:
        in_specs += [pl.BlockSpec((None, RB, CH), lambda b, t: (layer, b * nt + t, 0)),
                     pl.BlockSpec((None, nsb, GD_HEADS, GD_DK, GD_DV), lambda b, t: (layer, b, 0, 0, 0))]
        args += [bufp, s0]
    else:
        scratch.append(pltpu.VMEM((8, CH), F32))
    nseq = N // T
    return pl.pallas_call(
        functools.partial(_gd_kernel, R=R, NS=NS, RB=RB, has_s0=has_s0),
        grid=(nb, nt),
        in_specs=in_specs,
        out_specs=[rowspec(GD_W), st_spec,
                   pl.BlockSpec((nsb, GD_CONV - 1, CH), lambda b, t: (b, 0, 0))],
        out_shape=[jax.ShapeDtypeStruct((N, GD_W), out_dtype),
                   jax.ShapeDtypeStruct((nseq, GD_HEADS, GD_DK, GD_DV), F32),
                   jax.ShapeDtypeStruct((nseq, GD_CONV - 1, CH), F32)],
        scratch_shapes=scratch,
        compiler_params=_params(("arbitrary", "arbitrary")),
        name="gd",
    )(*args)


def _back_kernel(x_ref, ya_ref, yb_ref, yc_ref, yd_ref, wm_ref, wb_ref, wo_ref, lg_ref, lb_ref, o_ref, *, alpha):
    x = x_ref[...]
    gates = jax.nn.sigmoid(_dot(x.astype(BF16), wm_ref[...]))
    merged = None
    for i, yr in enumerate((ya_ref, yb_ref, yc_ref, yd_ref)):
        t = gates[:, i * D_MODEL:(i + 1) * D_MODEL] * _dot(yr[...].astype(BF16), wb_ref[i])
        merged = t if merged is None else merged + t
    z = alpha * x + _dot(merged.astype(BF16), wo_ref[...])
    mu = jnp.mean(z, axis=-1, keepdims=True)
    d = z - mu
    var = jnp.mean(d * d, axis=-1, keepdims=True)
    o_ref[...] = d * lax.rsqrt(var + EPS) * lg_ref[...] + lb_ref[...]


def _back_call(x, ya, yb, yc, yd, wm, wb, wo, lg, lb, *, layer, alpha):
    N = x.shape[0]
    TM = 256
    assert N % TM == 0
    row = lambda wd: pl.BlockSpec((TM, wd), lambda i: (i, 0))
    return pl.pallas_call(
        functools.partial(_back_kernel, alpha=alpha),
        grid=(N // TM,),
        in_specs=[row(D_MODEL), row(BRANCH_W), row(BRANCH_W), row(BRANCH_W), row(BRANCH_W),
                  _layer_spec(wm, layer), _layer_spec(wb, layer), _layer_spec(wo, layer),
                  _const_spec(lg.shape), _const_spec(lb.shape)],
        out_specs=row(D_MODEL),
        out_shape=jax.ShapeDtypeStruct((N, D_MODEL), F32),
        compiler_params=_params(("arbitrary",)),
        name="back",
    )(x, ya, yb, yc, yd, wm, wb, wo, lg, lb)


def _rope_tables(pos, d, reps, rows):
    inv = ROPE_BASE ** (-jnp.arange(0, d, 2, dtype=F32) / d)
    ang = pos.astype(F32)[:, None] * inv[None, :]
    c, s = jnp.cos(ang), jnp.sin(ang)
    cos = jnp.tile(jnp.concatenate([c, c], axis=-1), (rows // pos.shape[0], reps))
    sin = jnp.tile(jnp.concatenate([-s, s], axis=-1), (rows // pos.shape[0], reps))
    return cos, sin


def _prep_weights(w_in, mla_w_uq, mla_w_uk, mla_w_uv, w_branch, w_out):
    o = _OFF
    cols = lambda a, b: w_in[:, :, o[a]:o[b]]
    depth = w_in.shape[0]
    zpad = lambda n: jnp.zeros((depth, D_MODEL, n), w_in.dtype)
    w_ret = cols(0, 4).astype(BF16)
    w_mla = jnp.concatenate([cols(4, 6), cols(7, 8), cols(6, 7), zpad(LANES - MLA_ROPE)], axis=-1).astype(BF16)
    w_sg = cols(8, 11).astype(BF16)
    w_gd = jnp.concatenate([cols(11, 12), cols(14, 15), cols(12, 14), zpad(LANES - 2 * GD_HEADS)], axis=-1).astype(BF16)
    w_merge = cols(15, 16).astype(BF16)
    uq = mla_w_uq.reshape(depth, Q_LORA, MLA_HEADS, MLA_NOPE + MLA_ROPE)
    w_uq = jnp.concatenate([uq[..., :MLA_NOPE].reshape(depth, Q_LORA, -1),
                            uq[..., MLA_NOPE:].reshape(depth, Q_LORA, -1)], axis=-1).astype(BF16)
    w_uk = jnp.transpose(mla_w_uk, (0, 2, 3, 1)).astype(BF16)
    uv = mla_w_uv.reshape(depth, KV_LORA, MLA_HEADS // 2, 2, MLA_V)
    z = jnp.zeros((depth, KV_LORA, MLA_HEADS // 2, MLA_V), mla_w_uv.dtype)
    top = jnp.concatenate([uv[:, :, :, 0], z], axis=-1)
    bot = jnp.concatenate([z, uv[:, :, :, 1]], axis=-1)
    w_uv = jnp.transpose(jnp.concatenate([top, bot], axis=1), (0, 2, 1, 3)).astype(BF16)
    return dict(w_ret=w_ret, w_mla=w_mla, w_sg=w_sg, w_gd=w_gd, w_merge=w_merge, w_uq=w_uq, w_uk=w_uk, w_uv=w_uv,
                w_branch=w_branch.astype(BF16), w_out=w_out.astype(BF16))


def _sg_mix_weights(sg_w, sg_b, C):
    reps = SG_CHUNK // C
    w = jnp.tril(sg_w[:, :, :C, :C])
    if reps > 1:
        eye = jnp.eye(reps, dtype=w.dtype)
        w = jnp.einsum('ab,dgij->dgaibj', eye, w).reshape(w.shape[0], SG_GROUPS, SG_CHUNK, SG_CHUNK)
    b = jnp.tile(sg_b[:, :, :C], (1, 1, reps))
    bs = jnp.broadcast_to(b[..., None], b.shape + (SG_GDIM,))
    return w.astype(BF16), bs.astype(F32)


def _layer_stream(x, l, W, P, *, B, T, tabs, sample):
    cosr, sinr, cosm, sinm = tabs
    act = F32 if sample is not None else BF16
    ya, s_ret = _ret_call(x, W['w_ret'], cosr, sinr, P['ret_gn_g'][l][None],
                          sample['state_ret'] if sample else None, B=B, T=T, layer=l, out_dtype=act)
    mo = _mlap_call(x, W['w_mla'], W['w_uq'], P['mla_q_norm'][l][None],
                    P['mla_kv_norm'][l][None], cosm, sinm, layer=l, act_dtype=act, emit_kt=sample is None)
    qn, qpe, ckv, kpe, gm = mo[:5]
    if sample:
        yb = _attn_decode_call(sample['page_table'], qn, qpe, ckv, kpe, gm, W['w_uk'], W['w_uv'],
                               sample['cache_ckv'], sample['cache_kpet'], B=B, T=T, layer=l)
    else:
        yb = _attn_prefill_call(qn, qpe, mo[5], mo[6], W['w_uk'], gm, W['w_uv'], B=B, T=T, layer=l)
    wt, bs = (P['sg_wt_s'], P['sg_bs_s']) if sample else (P['sg_wt_p'], P['sg_bs_p'])
    sg_out = _sg_call(x, W['w_sg'], P['sg_ln_g'][l][None], P['sg_ln_b'][l][None], wt, bs,
                      layer=l, emit_v=sample is not None, out_dtype=act)
    yc = sg_out[0]
    sgv = sg_out[1] if sample else None
    yd, s_gd, conv_new = _gd_call(x, W['w_gd'], P['gd_conv_w'], P['gd_av'][l], P['gd_dtb'][l],
                                  P['gd_norm_g'][l][None],
                                  sample['bufp'] if sample else None,
                                  sample['state_delta'] if sample else None, B=B, T=T, layer=l, out_dtype=act)
    xn = _back_call(x, ya, yb, yc, yd, W['w_merge'], W['w_branch'], W['w_out'],
                    P['ln_g'][l][None], P['ln_b'][l][None], layer=l, alpha=P['alpha'])
    return xn, ckv, kpe, s_ret, s_gd, conv_new, sgv


def kernel(x_prompt, x_sample, cache_ckv, cache_kpe, state_ret, state_delta, state_conv, page_table,
           w_in, ret_gn_g, mla_q_norm, mla_w_uq, mla_kv_norm, mla_w_uk, mla_w_uv,
           sg_ln_g, sg_ln_b, sg_w, sg_b, gd_conv_w, gd_a_log, gd_dt_bias, gd_norm_g,
           w_branch, w_out, ln_g, ln_b):
    depth = w_in.shape[0]
    bp, tp, _ = x_prompt.shape
    bs_, ts, _ = x_sample.shape
    past_len = page_table.shape[1] * cache_ckv.shape[2]
    W = _prep_weights(w_in, mla_w_uq, mla_w_uk, mla_w_uv, w_branch, w_out)
    pad_small = lambda v: jnp.pad(v, ((0, 0), (0, LANES - v.shape[-1])))[:, None, :]
    P = dict(ret_gn_g=ret_gn_g, mla_q_norm=mla_q_norm, mla_kv_norm=mla_kv_norm, sg_ln_g=sg_ln_g, sg_ln_b=sg_ln_b,
             gd_conv_w=gd_conv_w, gd_norm_g=gd_norm_g, ln_g=ln_g, ln_b=ln_b,
             gd_av=pad_small(-jnp.exp(gd_a_log)), gd_dtb=pad_small(gd_dt_bias),
             alpha=float((2.0 * depth) ** 0.25))
    P['sg_wt_p'], P['sg_bs_p'] = _sg_mix_weights(sg_w, sg_b, min(tp, SG_CHUNK))
    P['sg_wt_s'], P['sg_bs_s'] = _sg_mix_weights(sg_w, sg_b, min(ts, SG_CHUNK))

    pos_p = jnp.arange(tp)
    pos_s = past_len + jnp.arange(ts)
    tab_rows_s = 512
    tabs_p = _rope_tables(pos_p, RET_DK, RET_HEADS, tp) + _rope_tables(pos_p, MLA_ROPE, MLA_HEADS, tp)
    tabs_s = _rope_tables(pos_s, RET_DK, RET_HEADS, tab_rows_s) + _rope_tables(pos_s, MLA_ROPE, MLA_HEADS, tab_rows_s)

    bufp = jnp.pad(state_conv, ((0, 0), (0, 0), (ts - (GD_CONV - 1), 0), (0, 0))).reshape(depth, bs_ * ts, GD_CONV_CH)
    sample = dict(state_ret=state_ret, state_delta=state_delta, bufp=bufp, page_table=page_table,
                  cache_ckv=cache_ckv, cache_kpet=jnp.swapaxes(cache_kpe, 2, 3))

    xp = x_prompt.reshape(bp * tp, D_MODEL)
    xs = x_sample.reshape(bs_ * ts, D_MODEL)
    outs_p, outs_s = [], []
    for l in range(depth):
        xp, *rest = _layer_stream(xp, l, W, P, B=bp, T=tp, tabs=tabs_p, sample=None)
        outs_p.append(rest)
        xs, *rest = _layer_stream(xs, l, W, P, B=bs_, T=ts, tabs=tabs_s, sample=sample)
        outs_s.append(rest)

    def stack(outs, k, shape):
        return jnp.stack([o[k].reshape(shape) for o in outs])

    return (xp.reshape(bp, tp, D_MODEL), xs.reshape(bs_, ts, D_MODEL),
            stack(outs_p, 0, (bp, tp, KV_LORA)), stack(outs_p, 1, (bp, tp, MLA_ROPE)),
            stack(outs_p, 2, (bp, RET_HEADS, RET_DK, RET_DV)), stack(outs_p, 3, (bp, GD_HEADS, GD_DK, GD_DV)),
            stack(outs_p, 4, (bp, GD_CONV - 1, GD_CONV_CH)),
            stack(outs_s, 0, (bs_, ts, KV_LORA)), stack(outs_s, 1, (bs_, ts, MLA_ROPE)),
            stack(outs_s, 2, (bs_, RET_HEADS, RET_DK, RET_DV)), stack(outs_s, 3, (bs_, GD_HEADS, GD_DK, GD_DV)),
            stack(outs_s, 4, (bs_, GD_CONV - 1, GD_CONV_CH)), stack(outs_s, 5, (bs_, ts, SG_W)))
```

```python
import functools
import math

import numpy as np
import jax
import jax.numpy as jnp
from jax import lax
from jax.experimental import pallas as pl
from jax.experimental.pallas import tpu as pltpu

F32 = jnp.float32
BF16 = jnp.bfloat16

D_MODEL = 1024
RET_HEADS, RET_DK, RET_DV = 4, 64, 128
RET_W = RET_HEADS * RET_DV
MLA_HEADS, MLA_NOPE, MLA_ROPE, MLA_V = 8, 64, 32, 64
Q_LORA, KV_LORA = 384, 256
MLA_W = MLA_HEADS * MLA_V
MLA_SCALE = (MLA_NOPE + MLA_ROPE) ** -0.5
SG_GROUPS, SG_GDIM, SG_CHUNK = 4, 128, 128
SG_W = SG_GROUPS * SG_GDIM
GD_HEADS, GD_DK, GD_DV, GD_CONV = 4, 128, 128, 4
GD_W = GD_HEADS * GD_DV
GD_CONV_CH = 2 * GD_HEADS * GD_DK + GD_W
N_BRANCH, BRANCH_W = 4, 512
LIN_CHUNK = 64
ROPE_BASE = 10000.0
EPS = 1e-6
NEG = -1e30
IN_SPLITS = (RET_HEADS * RET_DK, RET_HEADS * RET_DK, RET_W, RET_W,
             Q_LORA, KV_LORA, MLA_ROPE, MLA_W,
             SG_W, SG_W, SG_W,
             GD_CONV_CH, GD_HEADS, GD_HEADS, GD_W,
             N_BRANCH * D_MODEL)
_OFF = np.concatenate([[0], np.cumsum(IN_SPLITS)]).tolist()

LANES = 128
KCAT_W = KV_LORA + LANES
ATT_TK = 256
ATT_BQ = 256
ATT_LAG = 5
ATT_UNROLL = 2
GD_RB = 256
DEC_GROUP = 32
DEC_LOOK = 3
GD_BASE = 8
QSCALE = MLA_SCALE * math.log2(math.e)
VMEM_LIMIT = 56 * 1024 * 1024

_TB = (((1,), (1,)), ((), ()))
_TA = (((0,), (0,)), ((), ()))


def _dot(a, b):
    return jnp.dot(a, b, preferred_element_type=F32)


def _dot_tb(a, b):
    return lax.dot_general(a, b, _TB, preferred_element_type=F32)


def _dot_ta(a, b):
    return lax.dot_general(a, b, _TA, preferred_element_type=F32)


def _split2(a):
    hi = a.astype(BF16)
    lo = (a - hi.astype(F32)).astype(BF16)
    return hi, lo


def _dot3(a, b):
    ah, al = _split2(a)
    bh, bl = _split2(b)
    return _dot(ah, bh) + (_dot(ah, bl) + _dot(al, bh))


def _dot_mask(mask_bf16, x):
    hi = x.astype(BF16)
    r1 = x - hi.astype(F32)
    mid = r1.astype(BF16)
    lo = (r1 - mid.astype(F32)).astype(BF16)
    return _dot(mask_bf16, hi) + (_dot(mask_bf16, mid) + _dot(mask_bf16, lo))


def _silu(x):
    return x * jax.nn.sigmoid(x)


def _gelu(x):
    return 0.5 * x * (1.0 + lax.erf(x * np.float32(math.sqrt(0.5))))


def _rope(x, cos, sin_signed, d):
    w = x.shape[-1]
    half = d // 2
    lane = lax.broadcasted_iota(jnp.int32, x.shape, 1)
    first = (lane % d) < half
    rot = jnp.where(first, pltpu.roll(x, w - half, 1), pltpu.roll(x, half, 1))
    return x * cos + rot * sin_signed


def _const_spec(shape):
    nd = len(shape)
    return pl.BlockSpec(shape, lambda *_: (0,) * nd, pipeline_mode=pl.Buffered(1))


def _layer_spec(a, layer):
    nd = a.ndim - 1
    return pl.BlockSpec((None,) + a.shape[1:], lambda *_: (layer,) + (0,) * nd, pipeline_mode=pl.Buffered(1))


def _params(sem):
    return pltpu.CompilerParams(dimension_semantics=sem, vmem_limit_bytes=VMEM_LIMIT)


def _ret_kernel(*refs, R, NS, RB, cdec, has_s0):
    if has_s0:
        (x_ref, w_ref, cos_ref, sin_ref, gn_ref, dmat_ref, qdec_ref, kdec_ref, s0_ref,
         y_ref, sfin_ref) = refs
    else:
        (x_ref, w_ref, cos_ref, sin_ref, gn_ref, dmat_ref, qdec_ref, kdec_ref,
         y_ref, sfin_ref) = refs
    C = R // NS
    h = _dot(x_ref[...].astype(BF16), w_ref[...])
    cos = cos_ref[...]
    sin = sin_ref[...]
    hq = RET_HEADS * RET_DK
    q = _rope(h[:, 0:hq], cos, sin, RET_DK)
    k = _rope(h[:, hq:2 * hq], cos, sin, RET_DK) * (RET_DK ** -0.5)
    v = h[:, 2 * hq:2 * hq + RET_W]
    g = _silu(h[:, 2 * hq + RET_W:2 * hq + 2 * RET_W])

    if has_s0:
        sfin_ref[...] = s0_ref[...]
    else:
        @pl.when(pl.program_id(1) == 0)
        def _():
            sfin_ref[...] = jnp.zeros(sfin_ref.shape, F32)

    probs = [(c, hh) for c in range(RB // R) for hh in range(RET_HEADS)]
    qh, kh, vh, sc, kv = {}, {}, {}, {}, {}
    for pr in probs:
        c, hh = pr
        rs = slice(c * R, (c + 1) * R)
        qh[pr] = q[rs, hh * RET_DK:(hh + 1) * RET_DK].astype(BF16)
        kh[pr] = k[rs, hh * RET_DK:(hh + 1) * RET_DK].astype(BF16)
        vh[pr] = v[rs, hh * RET_DV:(hh + 1) * RET_DV]
        sc[pr] = _dot_tb(qh[pr], kh[pr])
        vk = (vh[pr] * kdec_ref[hh]).astype(BF16)
        for s in range(NS):
            kv[pr, s] = _dot_ta(kh[pr][s * C:(s + 1) * C], vk[s * C:(s + 1) * C])
    sprev = {}
    for pr in probs:
        c, hh = pr
        for s in range(NS):
            sidx = c * NS + s if has_s0 else 0
            st = sfin_ref[sidx, hh]
            sprev[pr, s] = st.astype(BF16)
            sfin_ref[sidx, hh] = st * cdec[hh] + kv[pr, s]
    for pr in probs:
        c, hh = pr
        rs = slice(c * R, (c + 1) * R)
        vs = slice(hh * RET_DV, (hh + 1) * RET_DV)
        o = _dot((sc[pr] * dmat_ref[hh]).astype(BF16), vh[pr].astype(BF16))
        qd = qdec_ref[hh]
        parts = [_dot(qh[pr][s * C:(s + 1) * C], sprev[pr, s]) * qd[s * C:(s + 1) * C] for s in range(NS)]
        o = o + (parts[0] if NS == 1 else jnp.concatenate(parts, axis=0))
        mu = jnp.mean(o, axis=-1, keepdims=True)
        d = o - mu
        var = jnp.mean(d * d, axis=-1, keepdims=True)
        on = d * lax.rsqrt(var + EPS) * gn_ref[:, vs]
        y_ref[rs, vs] = (g[rs, vs] * on).astype(y_ref.dtype)


def _ret_consts(R, NS):
    C = R // NS
    lg = np.log1p(-np.exp2(-5.0 - np.arange(RET_HEADS, dtype=np.float64)))
    idx = np.arange(R)
    tok = idx % C
    same = (idx[:, None] // C) == (idx[None, :] // C)
    rel = tok[:, None] - tok[None, :]
    low = same & (rel >= 0)
    dmat = np.where(low[None], np.exp(lg[:, None, None] * np.where(low, rel, 0)[None]), 0.0)
    qdec = np.exp(lg[:, None] * (tok[None, :] + 1.0))
    kdec = np.exp(lg[:, None] * (C - 1.0 - tok[None, :]))
    cdec = tuple(float(v) for v in np.exp(lg * C))
    bc = lambda a: jnp.asarray(np.broadcast_to(a[:, :, None], (RET_HEADS, R, LANES)), F32)
    return jnp.asarray(dmat, F32), bc(qdec), bc(kdec), cdec


def _ret_call(x, w, cos, sin, gn, s0, *, B, T, layer, out_dtype):
    N = x.shape[0]
    has_s0 = s0 is not None
    R = LIN_CHUNK
    if has_s0:
        C = math.gcd(T, LIN_CHUNK)
        assert C == T, "sample stream is a single chunk per sequence"
        NS = R // C
        RB = 2 * R
        nb, nt = N // RB, 1
        nsb = RB // C
    else:
        NS, RB = 1, 512
        assert T % RB == 0 and math.gcd(T, LIN_CHUNK) == R
        nb, nt = B, T // RB
        nsb = 1
    dmat, qdec, kdec, cdec = _ret_consts(R, NS)
    tr = cos.shape[0] // RB
    hq = RET_HEADS * RET_DK
    in_specs = [
        pl.BlockSpec((RB, D_MODEL), lambda b, t: (b * nt + t, 0)),
        _layer_spec(w, layer),
        pl.BlockSpec((RB, hq), lambda b, t: ((b * nt + t) % tr, 0)),
        pl.BlockSpec((RB, hq), lambda b, t: ((b * nt + t) % tr, 0)),
        _const_spec(gn.shape), _const_spec(dmat.shape), _const_spec(qdec.shape), _const_spec(kdec.shape),
    ]
    args = [x, w, cos, sin, gn, dmat, qdec, kdec]
    st_spec = pl.BlockSpec((nsb, RET_HEADS, RET_DK, RET_DV), lambda b, t: (b, 0, 0, 0))
    if has_s0:
        in_specs.append(pl.BlockSpec((None, nsb, RET_HEADS, RET_DK, RET_DV), lambda b, t: (layer, b, 0, 0, 0)))
        args.append(s0)
    nseq = N // T
    return pl.pallas_call(
        functools.partial(_ret_kernel, R=R, NS=NS, RB=RB, cdec=cdec, has_s0=has_s0),
        grid=(nb, nt),
        in_specs=in_specs,
        out_specs=[pl.BlockSpec((RB, RET_W), lambda b, t: (b * nt + t, 0)), st_spec],
        out_shape=[jax.ShapeDtypeStruct((N, RET_W), out_dtype),
                   jax.ShapeDtypeStruct((nseq, RET_HEADS, RET_DK, RET_DV), F32)],
        compiler_params=_params(("arbitrary", "arbitrary")),
        name="ret",
    )(*args)


def _mlap_kernel(x_ref, w_ref, wuq_ref, qn_ref, kvn_ref, cos_ref, sin_ref, *outs, TM, emit_kt):
    if emit_kt:
        qnope_ref, qpe_ref, ckv_ref, kpe_ref, gm_ref, kt_ref, vb_ref = outs
    else:
        qnope_ref, qpe_ref, ckv_ref, kpe_ref, gm_ref = outs
    h = _dot(x_ref[...].astype(BF16), w_ref[...])
    cq = h[:, 0:Q_LORA]
    cq = cq * lax.rsqrt(jnp.mean(cq * cq, axis=-1, keepdims=True) + EPS) * qn_ref[...]
    qf = _dot(cq.astype(BF16), wuq_ref[...])
    nw = MLA_HEADS * MLA_NOPE
    qnope_ref[...] = (qf[:, 0:nw] * QSCALE).astype(qnope_ref.dtype)
    cos = cos_ref[...]
    sin = sin_ref[...]
    qpe_ref[...] = _rope(qf[:, nw:nw + MLA_HEADS * MLA_ROPE], cos, sin, MLA_ROPE) * QSCALE
    c0 = Q_LORA
    kv = h[:, c0:c0 + KV_LORA]
    ckv = kv * lax.rsqrt(jnp.mean(kv * kv, axis=-1, keepdims=True) + EPS) * kvn_ref[...]
    ckv_ref[...] = ckv
    g0 = c0 + KV_LORA
    gm_ref[...] = _silu(h[:, g0:g0 + MLA_W]).astype(gm_ref.dtype)
    p0 = g0 + MLA_W
    kblk = _rope(h[:, p0:p0 + LANES], cos[:, 0:LANES], sin[:, 0:LANES], MLA_ROPE)
    kpe_ref[...] = kblk[:, 0:MLA_ROPE]
    if emit_kt:
        rep = kblk
        for i in range(1, LANES // MLA_ROPE):
            rep = rep + pltpu.roll(kblk, i * MLA_ROPE, 1)
        kct = jnp.concatenate([ckv, rep], axis=1).T
        for i in range(TM // ATT_TK):
            kt_ref[i] = kct[:, i * ATT_TK:(i + 1) * ATT_TK].astype(BF16)
        vb_ref[...] = ckv.astype(BF16)


def _mlap_call(x, w, wuq, qn, kvn, cos, sin, *, layer, act_dtype, emit_kt):
    N = x.shape[0]
    TM = 512
    assert N % TM == 0 and TM % ATT_TK == 0
    tr = cos.shape[0] // TM
    row = lambda wd: pl.BlockSpec((TM, wd), lambda i: (i, 0))
    tab = pl.BlockSpec((TM, MLA_HEADS * MLA_ROPE), lambda i: (i % tr, 0))
    nw = MLA_HEADS * MLA_NOPE
    out_specs = [row(nw), row(MLA_HEADS * MLA_ROPE), row(KV_LORA), row(MLA_ROPE), row(MLA_W)]
    out_shape = [jax.ShapeDtypeStruct((N, nw), act_dtype),
                 jax.ShapeDtypeStruct((N, MLA_HEADS * MLA_ROPE), F32),
                 jax.ShapeDtypeStruct((N, KV_LORA), F32),
                 jax.ShapeDtypeStruct((N, MLA_ROPE), F32),
                 jax.ShapeDtypeStruct((N, MLA_W), act_dtype)]
    if emit_kt:
        out_specs += [pl.BlockSpec((TM // ATT_TK, KCAT_W, ATT_TK), lambda i: (i, 0, 0)), row(KV_LORA)]
        out_shape += [jax.ShapeDtypeStruct((N // ATT_TK, KCAT_W, ATT_TK), BF16),
                      jax.ShapeDtypeStruct((N, KV_LORA), BF16)]
    return pl.pallas_call(
        functools.partial(_mlap_kernel, TM=TM, emit_kt=emit_kt),
        grid=(N // TM,),
        in_specs=[row(D_MODEL), _layer_spec(w, layer), _layer_spec(wuq, layer), _const_spec(qn.shape),
                  _const_spec(kvn.shape), tab, tab],
        out_specs=out_specs,
        out_shape=out_shape,
        compiler_params=_params(("arbitrary",)),
        name="mlap",
    )(x, w, wuq, qn, kvn, cos, sin)


def _attn_prefill_kernel(qn_ref, qpe_ref, kt_ref, v_ref, wuk_ref, gm_ref, wuv_ref, y_ref,
                         q_s, m_s, l_s, acc_s, *, BQ, TK):
    i = pl.program_id(1)
    H = MLA_HEADS
    lane = lax.broadcasted_iota(jnp.int32, (BQ, LANES), 1)
    for hh in range(H):
        ql = _dot(qn_ref[:, hh * MLA_NOPE:(hh + 1) * MLA_NOPE], wuk_ref[hh])
        q_s[hh * BQ:(hh + 1) * BQ, 0:KV_LORA] = ql.astype(BF16)
        per = LANES // MLA_ROPE
        blk = qpe_ref[:, (hh // per) * LANES:(hh // per + 1) * LANES]
        off = (hh % per) * MLA_ROPE
        keep = (lane >= off) & (lane < off + MLA_ROPE)
        q_s[hh * BQ:(hh + 1) * BQ, KV_LORA:KCAT_W] = jnp.where(keep, blk, 0.0).astype(BF16)
    m_s[...] = jnp.full(m_s.shape, NEG, F32)
    l_s[...] = jnp.zeros(l_s.shape, F32)
    acc_s[...] = jnp.zeros(acc_s.shape, F32)

    def steps(blocks):
        kts = [kt_ref[j] for j, _ in blocks]
        vbs = [v_ref[pl.ds(pl.multiple_of(j * TK, TK), TK), :] for j, _ in blocks]
        row = lax.broadcasted_iota(jnp.int32, (BQ, TK), 0)
        col = lax.broadcasted_iota(jnp.int32, (BQ, TK), 1)
        items = [(bi, hh) for bi in range(len(blocks)) for hh in range(H)]
        scores = {}
        for t in range(len(items) + ATT_LAG):
            if t < len(items):
                bi, hh = items[t]
                scores[t] = _dot(q_s[hh * BQ:(hh + 1) * BQ, :], kts[bi])
            if t < ATT_LAG:
                continue
            bi, hh = items[t - ATT_LAG]
            rs = slice(hh * BQ, (hh + 1) * BQ)
            s = scores.pop(t - ATT_LAG)
            if blocks[bi][1]:
                s = jnp.where(blocks[bi][0] * TK + col <= i * BQ + row, s, NEG)
            m_prev = m_s[rs, :]
            m_new = jnp.maximum(m_prev, jnp.max(s, axis=-1, keepdims=True))
            alpha = jnp.exp2(m_prev - m_new)
            p = jnp.exp2(s - jnp.concatenate([m_new] * (TK // LANES), axis=1))
            l_s[rs, :] = alpha[:, 0:1] * l_s[rs, :] + jnp.sum(p, axis=-1, keepdims=True)
            acc_s[rs, :] = (jnp.concatenate([alpha] * (KV_LORA // LANES), axis=1) * acc_s[rs, :]
                            + _dot(p.astype(BF16), vbs[bi]))
            m_s[rs, :] = m_new

    last = (i * BQ + BQ - 1) // TK
    npair = last // ATT_UNROLL

    def body(jj, carry):
        steps([(jj * ATT_UNROLL + u, False) for u in range(ATT_UNROLL)])
        return carry

    lax.fori_loop(0, npair, body, 0)

    def tail(j, carry):
        steps([(j, False)])
        return carry

    lax.fori_loop(npair * ATT_UNROLL, last, tail, 0)
    steps([(last, True)])

    for p2 in range(H // 2):
        halves = []
        for hh in (2 * p2, 2 * p2 + 1):
            rs = slice(hh * BQ, (hh + 1) * BQ)
            halves.append((acc_s[rs, :] / l_s[rs, :]).astype(BF16))
        yb = _dot(jnp.concatenate(halves, axis=1), wuv_ref[p2])
        y_ref[:, p2 * LANES:(p2 + 1) * LANES] = (gm_ref[:, p2 * LANES:(p2 + 1) * LANES].astype(F32) * yb).astype(y_ref.dtype)


def _attn_prefill_call(qn, qpe, kt, vb, wuk, gm, wuv, *, B, T, layer):
    N = qn.shape[0]
    BQ, TK = ATT_BQ, ATT_TK
    assert T % TK == 0 and T % BQ == 0
    nq = T // BQ
    H = MLA_HEADS
    row = lambda wd: pl.BlockSpec((BQ, wd), lambda b, i: (b * nq + i, 0))
    return pl.pallas_call(
        functools.partial(_attn_prefill_kernel, BQ=BQ, TK=TK),
        grid=(B, nq),
        in_specs=[row(H * MLA_NOPE), row(H * MLA_ROPE),
                  pl.BlockSpec((T // TK, KCAT_W, TK), lambda b, i: (b, 0, 0)),
                  pl.BlockSpec((T, KV_LORA), lambda b, i: (b, 0)),
                  _layer_spec(wuk, layer), row(MLA_W), _layer_spec(wuv, layer)],
        out_specs=row(MLA_W),
        out_shape=jax.ShapeDtypeStruct((N, MLA_W), BF16),
        scratch_shapes=[pltpu.VMEM((H * BQ, KCAT_W), BF16), pltpu.VMEM((H * BQ, LANES), F32),
                        pltpu.VMEM((H * BQ, 1), F32), pltpu.VMEM((H * BQ, KV_LORA), F32)],
        compiler_params=_params(("arbitrary", "arbitrary")),
        name="attn_prefill",
    )(qn, qpe, kt, vb, wuk, gm, wuv)


def _attn_decode_kernel(pt_ref, qn_ref, qpe_ref, ckvn_ref, kpen_ref, gm_ref, wuk_ref, wuv_ref,
                        cckv_ref, ckpe_ref, y_ref,
                        kbuf, pbuf, sem, *, T, G, NSTEP, NSUB, LOOK, layer, PAGE):
    b = pl.program_id(0)
    nb = pl.num_programs(0)
    H = MLA_HEADS

    def copies(bb, jj):
        out = []
        for g in range(G):
            page = pt_ref[bb, jj * G + g]
            out.append(pltpu.make_async_copy(cckv_ref.at[layer, page], kbuf.at[jj, g], sem.at[0, jj]))
            out.append(pltpu.make_async_copy(ckpe_ref.at[layer, page], pbuf.at[jj, g], sem.at[1, jj]))
        return out

    @pl.when(b == 0)
    def _():
        for jj in range(LOOK):
            for c in copies(b, jj):
                c.start()

    ql = jnp.concatenate([_dot(qn_ref[:, hh * MLA_NOPE:(hh + 1) * MLA_NOPE].astype(BF16), wuk_ref[hh])
                          for hh in range(H)], axis=0).astype(BF16)
    qp = jnp.concatenate([qpe_ref[:, hh * MLA_ROPE:(hh + 1) * MLA_ROPE] for hh in range(H)], axis=0).astype(BF16)
    kn = ckvn_ref[...].astype(BF16)
    s = _dot_tb(ql, kn) + _dot_tb(qp, kpen_ref[...].astype(BF16))
    row = lax.broadcasted_iota(jnp.int32, s.shape, 0)
    col = lax.broadcasted_iota(jnp.int32, s.shape, 1)
    s = jnp.where(col <= row % T, s, NEG)
    m = jnp.max(s, axis=-1, keepdims=True)
    p = jnp.exp2(s - m)
    l = jnp.sum(p, axis=-1, keepdims=True)
    acc = _dot(p.astype(BF16), kn)

    GS = G // NSUB
    for j in range(NSTEP):
        jn = j + LOOK
        if jn < NSTEP:
            for c in copies(b, jn):
                c.start()
        else:
            @pl.when(b + 1 < nb)
            def _():
                for c in copies(b + 1, jn - NSTEP):
                    c.start()
        for c in copies(b, j):
            c.wait()
        kbs, scores = [], []
        for u in range(NSUB):
            kb = kbuf[j, u * GS:(u + 1) * GS].reshape(GS * PAGE, KV_LORA).astype(BF16)
            pcat = jnp.concatenate([pbuf[j, u * GS + g] for g in range(GS)], axis=1).astype(BF16)
            kbs.append(kb)
            scores.append(_dot_tb(ql, kb) + _dot(qp, pcat))
        for u in range(NSUB):
            kb, s = kbs[u], scores[u]
            m_new = jnp.maximum(m, jnp.max(s, axis=-1, keepdims=True))
            alpha = jnp.exp2(m - m_new)
            p = jnp.exp2(s - m_new)
            l = alpha * l + jnp.sum(p, axis=-1, keepdims=True)
            acc = alpha * acc + _dot(p.astype(BF16), kb)
            m = m_new

    o = acc / l
    for p2 in range(H // 2):
        lhs = jnp.concatenate([o[(2 * p2) * T:(2 * p2 + 1) * T], o[(2 * p2 + 1) * T:(2 * p2 + 2) * T]], axis=1)
        yb = _dot(lhs.astype(BF16), wuv_ref[p2])
        y_ref[:, p2 * LANES:(p2 + 1) * LANES] = gm_ref[:, p2 * LANES:(p2 + 1) * LANES] * yb


def _attn_decode_call(page_table, qn, qpe, ckv, kpe, gm, wuk, wuv, cache_ckv, cache_kpet, *, B, T, layer):
    N = qn.shape[0]
    n_pages = page_table.shape[1]
    PAGE = cache_ckv.shape[2]
    G = math.gcd(n_pages, DEC_GROUP)
    if n_pages // G < 2 and G % 2 == 0:
        G //= 2
    NSTEP = n_pages // G
    assert NSTEP >= 2, "the page groups of one sample are double buffered"
    page_bytes = PAGE * (KV_LORA + MLA_ROPE) * 4
    assert n_pages * page_bytes <= VMEM_LIMIT // 2, "one sample's pages must fit the VMEM buffers"
    LOOK = min(DEC_LOOK, NSTEP - 1)
    NSUB = math.gcd(G, 4)
    H = MLA_HEADS
    row = lambda wd: pl.BlockSpec((T, wd), lambda b, pt: (b, 0))
    grid_spec = pltpu.PrefetchScalarGridSpec(
        num_scalar_prefetch=1,
        grid=(B,),
        in_specs=[row(H * MLA_NOPE), row(H * MLA_ROPE), row(KV_LORA), row(MLA_ROPE), row(MLA_W),
                  _layer_spec(wuk, layer), _layer_spec(wuv, layer),
                  pl.BlockSpec(memory_space=pl.ANY), pl.BlockSpec(memory_space=pl.ANY)],
        out_specs=row(MLA_W),
        scratch_shapes=[pltpu.VMEM((NSTEP, G, PAGE, KV_LORA), F32), pltpu.VMEM((NSTEP, G, MLA_ROPE, PAGE), F32),
                        pltpu.SemaphoreType.DMA((2, NSTEP))],
    )
    return pl.pallas_call(
        functools.partial(_attn_decode_kernel, T=T, G=G, NSTEP=NSTEP, NSUB=NSUB, LOOK=LOOK, layer=layer, PAGE=PAGE),
        grid_spec=grid_spec,
        out_shape=jax.ShapeDtypeStruct((N, MLA_W), F32),
        compiler_params=_params(("arbitrary",)),
        name="attn_decode",
    )(page_table, qn, qpe, ckv, kpe, gm, wuk, wuv, cache_ckv, cache_kpet)


def _sg_kernel(x_ref, w_ref, lng_ref, lnb_ref, wt_ref, bs_ref, y_ref, *rest, TM, emit_v):
    h = _dot(x_ref[...].astype(BF16), w_ref[...])
    u = _gelu(h[:, 0:SG_W])
    gv = _gelu(h[:, SG_W:2 * SG_W])
    mu = jnp.mean(gv, axis=-1, keepdims=True)
    d = gv - mu
    var = jnp.mean(d * d, axis=-1, keepdims=True)
    v = d * lax.rsqrt(var + EPS) * lng_ref[...] + lnb_ref[...]
    if emit_v:
        rest[0][...] = v
    gs = _silu(h[:, 2 * SG_W:3 * SG_W])
    vb = v.astype(BF16)
    for c in range(TM // SG_CHUNK):
        rows = slice(c * SG_CHUNK, (c + 1) * SG_CHUNK)
        for g in range(SG_GROUPS):
            cols = slice(g * SG_GDIM, (g + 1) * SG_GDIM)
            s = _dot(wt_ref[g], vb[rows, cols]) + bs_ref[g]
            y_ref[rows, cols] = (gs[rows, cols] * (u[rows, cols] * s)).astype(y_ref.dtype)


def _sg_call(x, w, lng, lnb, wt, bs, *, layer, emit_v, out_dtype):
    N = x.shape[0]
    TM = 512
    assert N % TM == 0
    row = lambda wd: pl.BlockSpec((TM, wd), lambda i: (i, 0))
    out_specs = [row(SG_W)]
    out_shape = [jax.ShapeDtypeStruct((N, SG_W), out_dtype)]
    if emit_v:
        out_specs.append(row(SG_W))
        out_shape.append(jax.ShapeDtypeStruct((N, SG_W), F32))
    return pl.pallas_call(
        functools.partial(_sg_kernel, TM=TM, emit_v=emit_v),
        grid=(N // TM,),
        in_specs=[row(D_MODEL), _layer_spec(w, layer), _const_spec(lng.shape), _const_spec(lnb.shape),
                  _layer_spec(wt, layer), _layer_spec(bs, layer)],
        out_specs=out_specs,
        out_shape=out_shape,
        compiler_params=_params(("arbitrary",)),
        name="sg",
    )(x, w, lng, lnb, wt, bs)


def _gd_kernel(*refs, R, NS, RB, has_s0):
    if has_s0:
        (x_ref, w_ref, cw_ref, av_ref, dtb_ref, ng_ref, msk_ref, bufp_ref, s0_ref,
         y_ref, sfin_ref, conv_ref) = refs
    else:
        (x_ref, w_ref, cw_ref, av_ref, dtb_ref, ng_ref, msk_ref,
         y_ref, sfin_ref, conv_ref, carry_s) = refs
    C = R // NS
    NLEV = _gd_levels(C)
    CH = GD_CONV_CH
    h = _dot(x_ref[...].astype(BF16), w_ref[...])
    xq = h[:, 0:CH]
    y = xq * cw_ref[GD_CONV - 1:GD_CONV, :]
    if has_s0:
        x3 = xq.reshape(RB // 8, 8, CH)
        b3 = bufp_ref[...].reshape(RB // 8, 8, CH)
        tpos = lax.broadcasted_iota(jnp.int32, x3.shape, 1)
        for k in range(1, GD_CONV):
            sh = jnp.where(tpos >= k, pltpu.roll(x3, k, 1), pltpu.roll(b3, k, 1))
            y = y + sh.reshape(RB, CH) * cw_ref[GD_CONV - 1 - k:GD_CONV - k, :]
        for s in range(RB // C):
            conv_ref[s] = xq[s * C + C - (GD_CONV - 1):s * C + C, :]
    else:
        @pl.when(pl.program_id(1) == 0)
        def _():
            carry_s[...] = jnp.zeros(carry_s.shape, F32)
        cprev = carry_s[...]
        row8 = lax.broadcasted_iota(jnp.int32, (8, CH), 0)
        for k in range(1, GD_CONV):
            big = pltpu.roll(xq, k, 0)
            head = jnp.where(row8 < k, pltpu.roll(cprev, k, 0), big[0:8])
            sh = jnp.concatenate([head, big[8:]], axis=0)
            y = y + sh * cw_ref[GD_CONV - 1 - k:GD_CONV - k, :]
        carry_s[...] = xq[RB - 8:RB]
        conv_ref[0] = xq[RB - (GD_CONV - 1):RB, :]
    qkv = _silu(y)
    hw = GD_HEADS * GD_DK
    qn, kn = [], []
    for hh in range(GD_HEADS):
        qb = qkv[:, hh * GD_DK:(hh + 1) * GD_DK]
        qn.append(qb * lax.rsqrt(jnp.sum(qb * qb, axis=-1, keepdims=True) + EPS) * (GD_DK ** -0.5))
        kb = qkv[:, hw + hh * GD_DK:hw + (hh + 1) * GD_DK]
        kn.append(kb * lax.rsqrt(jnp.sum(kb * kb, axis=-1, keepdims=True) + EPS))
    vv = qkv[:, 2 * hw:2 * hw + GD_W]
    gate = _silu(h[:, CH:CH + GD_W])
    ab = h[:, CH + GD_W:CH + GD_W + LANES]
    lane = lax.broadcasted_iota(jnp.int32, ab.shape, 1)
    gb = jnp.where(lane < GD_HEADS, av_ref[...] * jax.nn.softplus(ab + dtb_ref[...]), jax.nn.sigmoid(ab))

    if has_s0:
        sfin_ref[...] = s0_ref[...]
    else:
        @pl.when(pl.program_id(1) == 0)
        def _():
            sfin_ref[...] = jnp.zeros(sfin_ref.shape, F32)

    lowf = msk_ref[0]
    low = lowf > 0.0
    strictf = msk_ref[1]
    lowb = lowf.astype(BF16)
    sameb = msk_ref[2].astype(BF16)
    ri = lax.broadcasted_iota(jnp.int32, (R, R), 0)
    ci = lax.broadcasted_iota(jnp.int32, (R, R), 1)
    eye = (ri == ci).astype(F32)

    nch = RB // R
    probs = [(c, hh) for c in range(nch) for hh in range(GD_HEADS)]
    gbc = [gb[c * R:(c + 1) * R] for c in range(nch)]
    gc4 = [_dot_mask(lowb, g) for g in gbc]
    gl4 = [_dot_mask(sameb, g) for g in gbc]
    qh, kh, qkk = {}, {}, {}
    for pr in probs:
        c, hh = pr
        rs = slice(c * R, (c + 1) * R)
        qh[pr], kh[pr] = qn[hh][rs], kn[hh][rs]
        khb = kh[pr].astype(BF16)
        qkk[pr] = _dot_tb(jnp.concatenate([qh[pr].astype(BF16), khb], axis=0), khb)
    gcb, glb, beta, gam, qk, amat, mp, pinv = {}, {}, {}, {}, {}, {}, {}, {}
    for pr in probs:
        c, hh = pr
        gcb[pr] = jnp.broadcast_to(gc4[c][:, hh:hh + 1], (R, LANES))
        glb[pr] = jnp.broadcast_to(gl4[c][:, hh:hh + 1], (R, LANES))
        beta[pr] = jnp.broadcast_to(gbc[c][:, GD_HEADS + hh:GD_HEADS + hh + 1], (R, LANES))
        gi = gcb[pr][:, 0:R]
        gam[pr] = jnp.where(low, jnp.exp(jnp.where(low, gi - gi.T, 0.0)), 0.0)
        qk[pr] = (qkk[pr][0:R] * gam[pr]).astype(BF16)
        amat[pr] = strictf * (beta[pr][:, 0:R] * qkk[pr][R:2 * R] * gam[pr])
        mp[pr] = -(msk_ref[3] * amat[pr])
        pinv[pr] = eye + mp[pr]
    for _ in range(int(math.log2(min(C, GD_BASE))) - 1):
        for pr in probs:
            mp[pr] = _dot3(mp[pr], mp[pr])
        for pr in probs:
            pinv[pr] = pinv[pr] + _dot3(pinv[pr], mp[pr])
    for lev in range(NLEV):
        wl = {pr: _dot3(msk_ref[4 + lev] * amat[pr], pinv[pr]) for pr in probs}
        for pr in probs:
            pinv[pr] = pinv[pr] - _dot3(pinv[pr], wl[pr])
    pre = {}
    for pr in probs:
        c, hh = pr
        vh = vv[c * R:(c + 1) * R, hh * GD_DK:(hh + 1) * GD_DK]
        eg = jnp.exp(gcb[pr])
        sol = _dot3(pinv[pr], jnp.concatenate([vh * beta[pr], kh[pr] * (beta[pr] * eg)], axis=1))
        pre[pr] = (sol[:, 0:GD_DV], sol[:, GD_DV:GD_DV + GD_DK].astype(BF16),
                   (qh[pr] * eg).astype(BF16), (kh[pr] * jnp.exp(glb[pr] - gcb[pr])).astype(BF16),
                   jnp.exp(glb[pr]))

    for c in range(nch):
        rs = slice(c * R, (c + 1) * R)
        seqs = [(hh, s) for hh in range(GD_HEADS) for s in range(NS)]
        st, sb, ws, os_ = {}, {}, {}, {}
        for hh, s in seqs:
            sidx = c * NS + s if has_s0 else 0
            st[hh, s] = sfin_ref[sidx, hh]
            sb[hh, s] = st[hh, s].astype(BF16)
        for hh, s in seqs:
            rows = slice(s * C, (s + 1) * C)
            u, wb, qd, kd, el = pre[c, hh]
            ws[hh, s] = _dot(wb[rows], sb[hh, s])
            os_[hh, s] = _dot(qd[rows], sb[hh, s])
        vnew = {}
        for hh in range(GD_HEADS):
            u = pre[c, hh][0]
            vns = [u[s * C:(s + 1) * C] - ws[hh, s] for s in range(NS)]
            vnew[hh] = vns[0] if NS == 1 else jnp.concatenate(vns, axis=0)
        for hh, s in seqs:
            rows = slice(s * C, (s + 1) * C)
            kd, el = pre[c, hh][3], pre[c, hh][4]
            sidx = c * NS + s if has_s0 else 0
            sfin_ref[sidx, hh] = (st[hh, s] * el[s * C:s * C + 1, :]
                                  + _dot_ta(kd[rows], vnew[hh][rows].astype(BF16)))
        for hh in range(GD_HEADS):
            sl = slice(hh * GD_DK, (hh + 1) * GD_DK)
            oi = [os_[hh, s] for s in range(NS)]
            o = (oi[0] if NS == 1 else jnp.concatenate(oi, axis=0)) + _dot(qk[c, hh], vnew[hh].astype(BF16))
            on = o * lax.rsqrt(jnp.mean(o * o, axis=-1, keepdims=True) + EPS) * ng_ref[...]
            y_ref[rs, sl] = (gate[rs, sl] * on).astype(y_ref.dtype)


def _gd_levels(C):
    base = min(C, GD_BASE)
    assert C % base == 0 and (C // base) & (C // base - 1) == 0 and base & (base - 1) == 0
    return int(math.log2(C // base))


def _gd_masks(R, NS):
    C = R // NS
    idx = np.arange(R)
    blk = lambda s: (idx[:, None] // s) == (idx[None, :] // s)
    same = blk(C)
    low = same & (idx[:, None] >= idx[None, :])
    strict = same & (idx[:, None] > idx[None, :])
    base = min(C, GD_BASE)
    masks = [low, strict, same, blk(base)]
    for lev in range(_gd_levels(C)):
        s = base * 2 ** (lev + 1)
        masks.append(blk(s) & ~blk(s // 2))
    return jnp.asarray(np.stack(masks).astype(np.float32))


def _gd_call(x, w, cw, av, dtb, ng, bufp, s0, *, B, T, layer, out_dtype):
    N = x.shape[0]
    has_s0 = s0 is not None
    R = LIN_CHUNK
    CH = GD_CONV_CH
    if has_s0:
        C = math.gcd(T, LIN_CHUNK)
        assert C == T and T == 8 and T >= GD_CONV - 1
        NS = R // C
        RB = 2 * R
        nb, nt = N // RB, 1
        nsb = RB // C
    else:
        NS, RB = 1, GD_RB
        assert T % RB == 0 and math.gcd(T, LIN_CHUNK) == R
        nb, nt = B, T // RB
        nsb = 1
    msk = _gd_masks(R, NS)
    rowspec = lambda wd: pl.BlockSpec((RB, wd), lambda b, t: (b * nt + t, 0))
    in_specs = [rowspec(D_MODEL), _layer_spec(w, layer), _layer_spec(cw, layer), _const_spec(av.shape),
                _const_spec(dtb.shape), _const_spec(ng.shape), _const_spec(msk.shape)]
    args = [x, w, cw, av, dtb, ng, msk]
    st_spec = pl.BlockSpec((nsb, GD_HEADS, GD_DK, GD_DV), lambda b, t: (b, 0, 0, 0))
    scratch = []
    if has_s0:
        in_specs += [pl.BlockSpec((None, RB, CH), lambda b, t: (layer, b * nt + t, 0)),
                     pl.BlockSpec((None, nsb, GD_HEADS, GD_DK, GD_DV), lambda b, t: (layer, b, 0, 0, 0))]
        args += [bufp, s0]
    else:
        scratch.append(pltpu.VMEM((8, CH), F32))
    nseq = N // T
    return pl.pallas_call(
        functools.partial(_gd_kernel, R=R, NS=NS, RB=RB, has_s0=has_s0),
        grid=(nb, nt),
        in_specs=in_specs,
        out_specs=[rowspec(GD_W), st_spec,
                   pl.BlockSpec((nsb, GD_CONV - 1, CH), lambda b, t: (b, 0, 0))],
        out_shape=[jax.ShapeDtypeStruct((N, GD_W), out_dtype),
                   jax.ShapeDtypeStruct((nseq, GD_HEADS, GD_DK, GD_DV), F32),
                   jax.ShapeDtypeStruct((nseq, GD_CONV - 1, CH), F32)],
        scratch_shapes=scratch,
        compiler_params=_params(("arbitrary", "arbitrary")),
        name="gd",
    )(*args)


def _back_kernel(x_ref, ya_ref, yb_ref, yc_ref, yd_ref, wm_ref, wb_ref, wo_ref, lg_ref, lb_ref, o_ref, *, alpha):
    x = x_ref[...]
    gates = jax.nn.sigmoid(_dot(x.astype(BF16), wm_ref[...]))
    merged = None
    for i, yr in enumerate((ya_ref, yb_ref, yc_ref, yd_ref)):
        t = gates[:, i * D_MODEL:(i + 1) * D_MODEL] * _dot(yr[...].astype(BF16), wb_ref[i])
        merged = t if merged is None else merged + t
    z = alpha * x + _dot(merged.astype(BF16), wo_ref[...])
    mu = jnp.mean(z, axis=-1, keepdims=True)
    d = z - mu
    var = jnp.mean(d * d, axis=-1, keepdims=True)
    o_ref[...] = d * lax.rsqrt(var + EPS) * lg_ref[...] + lb_ref[...]


def _back_call(x, ya, yb, yc, yd, wm, wb, wo, lg, lb, *, layer, alpha):
    N = x.shape[0]
    TM = 256
    assert N % TM == 0
    row = lambda wd: pl.BlockSpec((TM, wd), lambda i: (i, 0))
    return pl.pallas_call(
        functools.partial(_back_kernel, alpha=alpha),
        grid=(N // TM,),
        in_specs=[row(D_MODEL), row(BRANCH_W), row(BRANCH_W), row(BRANCH_W), row(BRANCH_W),
                  _layer_spec(wm, layer), _layer_spec(wb, layer), _layer_spec(wo, layer),
                  _const_spec(lg.shape), _const_spec(lb.shape)],
        out_specs=row(D_MODEL),
        out_shape=jax.ShapeDtypeStruct((N, D_MODEL), F32),
        compiler_params=_params(("arbitrary",)),
        name="back",
    )(x, ya, yb, yc, yd, wm, wb, wo, lg, lb)


def _rope_tables(pos, d, reps, rows):
    inv = ROPE_BASE ** (-jnp.arange(0, d, 2, dtype=F32) / d)
    ang = pos.astype(F32)[:, None] * inv[None, :]
    c, s = jnp.cos(ang), jnp.sin(ang)
    cos = jnp.tile(jnp.concatenate([c, c], axis=-1), (rows // pos.shape[0], reps))
    sin = jnp.tile(jnp.concatenate([-s, s], axis=-1), (rows // pos.shape[0], reps))
    return cos, sin


def _prep_weights(w_in, mla_w_uq, mla_w_uk, mla_w_uv, w_branch, w_out):
    o = _OFF
    cols = lambda a, b: w_in[:, :, o[a]:o[b]]
    depth = w_in.shape[0]
    zpad = lambda n: jnp.zeros((depth, D_MODEL, n), w_in.dtype)
    w_ret = cols(0, 4).astype(BF16)
    w_mla = jnp.concatenate([cols(4, 6), cols(7, 8), cols(6, 7), zpad(LANES - MLA_ROPE)], axis=-1).astype(BF16)
    w_sg = cols(8, 11).astype(BF16)
    w_gd = jnp.concatenate([cols(11, 12), cols(14, 15), cols(12, 14), zpad(LANES - 2 * GD_HEADS)], axis=-1).astype(BF16)
    w_merge = cols(15, 16).astype(BF16)
    uq = mla_w_uq.reshape(depth, Q_LORA, MLA_HEADS, MLA_NOPE + MLA_ROPE)
    w_uq = jnp.concatenate([uq[..., :MLA_NOPE].reshape(depth, Q_LORA, -1),
                            uq[..., MLA_NOPE:].reshape(depth, Q_LORA, -1)], axis=-1).astype(BF16)
    w_uk = jnp.transpose(mla_w_uk, (0, 2, 3, 1)).astype(BF16)
    uv = mla_w_uv.reshape(depth, KV_LORA, MLA_HEADS // 2, 2, MLA_V)
    z = jnp.zeros((depth, KV_LORA, MLA_HEADS // 2, MLA_V), mla_w_uv.dtype)
    top = jnp.concatenate([uv[:, :, :, 0], z], axis=-1)
    bot = jnp.concatenate([z, uv[:, :, :, 1]], axis=-1)
    w_uv = jnp.transpose(jnp.concatenate([top, bot], axis=1), (0, 2, 1, 3)).astype(BF16)
    return dict(w_ret=w_ret, w_mla=w_mla, w_sg=w_sg, w_gd=w_gd, w_merge=w_merge, w_uq=w_uq, w_uk=w_uk, w_uv=w_uv,
                w_branch=w_branch.astype(BF16), w_out=w_out.astype(BF16))


def _sg_mix_weights(sg_w, sg_b, C):
    reps = SG_CHUNK // C
    w = jnp.tril(sg_w[:, :, :C, :C])
    if reps > 1:
        eye = jnp.eye(reps, dtype=w.dtype)
        w = jnp.einsum('ab,dgij->dgaibj', eye, w).reshape(w.shape[0], SG_GROUPS, SG_CHUNK, SG_CHUNK)
    b = jnp.tile(sg_b[:, :, :C], (1, 1, reps))
    bs = jnp.broadcast_to(b[..., None], b.shape + (SG_GDIM,))
    return w.astype(BF16), bs.astype(F32)


def _layer_stream(x, l, W, P, *, B, T, tabs, sample):
    cosr, sinr, cosm, sinm = tabs
    act = F32 if sample is not None else BF16
    ya, s_ret = _ret_call(x, W['w_ret'], cosr, sinr, P['ret_gn_g'][l][None],
                          sample['state_ret'] if sample else None, B=B, T=T, layer=l, out_dtype=act)
    mo = _mlap_call(x, W['w_mla'], W['w_uq'], P['mla_q_norm'][l][None],
                    P['mla_kv_norm'][l][None], cosm, sinm, layer=l, act_dtype=act, emit_kt=sample is None)
    qn, qpe, ckv, kpe, gm = mo[:5]
    if sample:
        yb = _attn_decode_call(sample['page_table'], qn, qpe, ckv, kpe, gm, W['w_uk'], W['w_uv'],
                               sample['cache_ckv'], sample['cache_kpet'], B=B, T=T, layer=l)
    else:
        yb = _attn_prefill_call(qn, qpe, mo[5], mo[6], W['w_uk'], gm, W['w_uv'], B=B, T=T, layer=l)
    wt, bs = (P['sg_wt_s'], P['sg_bs_s']) if sample else (P['sg_wt_p'], P['sg_bs_p'])
    sg_out = _sg_call(x, W['w_sg'], P['sg_ln_g'][l][None], P['sg_ln_b'][l][None], wt, bs,
                      layer=l, emit_v=sample is not None, out_dtype=act)
    yc = sg_out[0]
    sgv = sg_out[1] if sample else None
    yd, s_gd, conv_new = _gd_call(x, W['w_gd'], P['gd_conv_w'], P['gd_av'][l], P['gd_dtb'][l],
                                  P['gd_norm_g'][l][None],
                                  sample['bufp'] if sample else None,
                                  sample['state_delta'] if sample else None, B=B, T=T, layer=l, out_dtype=act)
    xn = _back_call(x, ya, yb, yc, yd, W['w_merge'], W['w_branch'], W['w_out'],
                    P['ln_g'][l][None], P['ln_b'][l][None], layer=l, alpha=P['alpha'])
    return xn, ckv, kpe, s_ret, s_gd, conv_new, sgv


def kernel(x_prompt, x_sample, cache_ckv, cache_kpe, state_ret, state_delta, state_conv, page_table,
           w_in, ret_gn_g, mla_q_norm, mla_w_uq, mla_kv_norm, mla_w_uk, mla_w_uv,
           sg_ln_g, sg_ln_b, sg_w, sg_b, gd_conv_w, gd_a_log, gd_dt_bias, gd_norm_g,
           w_branch, w_out, ln_g, ln_b):
    depth = w_in.shape[0]
    bp, tp, _ = x_prompt.shape
    bs_, ts, _ = x_sample.shape
    past_len = page_table.shape[1] * cache_ckv.shape[2]
    W = _prep_weights(w_in, mla_w_uq, mla_w_uk, mla_w_uv, w_branch, w_out)
    pad_small = lambda v: jnp.pad(v, ((0, 0), (0, LANES - v.shape[-1])))[:, None, :]
    P = dict(ret_gn_g=ret_gn_g, mla_q_norm=mla_q_norm, mla_kv_norm=mla_kv_norm, sg_ln_g=sg_ln_g, sg_ln_b=sg_ln_b,
             gd_conv_w=gd_conv_w, gd_norm_g=gd_norm_g, ln_g=ln_g, ln_b=ln_b,
             gd_av=pad_small(-jnp.exp(gd_a_log)), gd_dtb=pad_small(gd_dt_bias),
             alpha=float((2.0 * depth) ** 0.25))
    P['sg_wt_p'], P['sg_bs_p'] = _sg_mix_weights(sg_w, sg_b, min(tp, SG_CHUNK))
    P['sg_wt_s'], P['sg_bs_s'] = _sg_mix_weights(sg_w, sg_b, min(ts, SG_CHUNK))

    pos_p = jnp.arange(tp)
    pos_s = past_len + jnp.arange(ts)
    tab_rows_s = 512
    tabs_p = _rope_tables(pos_p, RET_DK, RET_HEADS, tp) + _rope_tables(pos_p, MLA_ROPE, MLA_HEADS, tp)
    tabs_s = _rope_tables(pos_s, RET_DK, RET_HEADS, tab_rows_s) + _rope_tables(pos_s, MLA_ROPE, MLA_HEADS, tab_rows_s)

    bufp = jnp.pad(state_conv, ((0, 0), (0, 0), (ts - (GD_CONV - 1), 0), (0, 0))).reshape(depth, bs_ * ts, GD_CONV_CH)
    sample = dict(state_ret=state_ret, state_delta=state_delta, bufp=bufp, page_table=page_table,
                  cache_ckv=cache_ckv, cache_kpet=jnp.swapaxes(cache_kpe, 2, 3))

    xp = x_prompt.reshape(bp * tp, D_MODEL)
    xs = x_sample.reshape(bs_ * ts, D_MODEL)
    outs_p, outs_s = [], []
    for l in range(depth):
        xp, *rest = _layer_stream(xp, l, W, P, B=bp, T=tp, tabs=tabs_p, sample=None)
        outs_p.append(rest)
        xs, *rest = _layer_stream(xs, l, W, P, B=bs_, T=ts, tabs=tabs_s, sample=sample)
        outs_s.append(rest)

    def stack(outs, k, shape):
        return jnp.stack([o[k].reshape(shape) for o in outs])

    return (xp.reshape(bp, tp, D_MODEL), xs.reshape(bs_, ts, D_MODEL),
            stack(outs_p, 0, (bp, tp, KV_LORA)), stack(outs_p, 1, (bp, tp, MLA_ROPE)),
            stack(outs_p, 2, (bp, RET_HEADS, RET_DK, RET_DV)), stack(outs_p, 3, (bp, GD_HEADS, GD_DK, GD_DV)),
            stack(outs_p, 4, (bp, GD_CONV - 1, GD_CONV_CH)),
            stack(outs_s, 0, (bs_, ts, KV_LORA)), stack(outs_s, 1, (bs_, ts, MLA_ROPE)),
            stack(outs_s, 2, (bs_, RET_HEADS, RET_DK, RET_DV)), stack(outs_s, 3, (bs_, GD_HEADS, GD_DK, GD_DV)),
            stack(outs_s, 4, (bs_, GD_CONV - 1, GD_CONV_CH)), stack(outs_s, 5, (bs_, ts, SG_W)))
```

```python
import functools
import math

import numpy as np
import jax
import jax.numpy as jnp
from jax import lax
from jax.experimental import pallas as pl
from jax.experimental.pallas import tpu as pltpu

F32 = jnp.float32
BF16 = jnp.bfloat16

D_MODEL = 1024
RET_HEADS, RET_DK, RET_DV = 4, 64, 128
RET_W = RET_HEADS * RET_DV
MLA_HEADS, MLA_NOPE, MLA_ROPE, MLA_V = 8, 64, 32, 64
Q_LORA, KV_LORA = 384, 256
MLA_W = MLA_HEADS * MLA_V
MLA_SCALE = (MLA_NOPE + MLA_ROPE) ** -0.5
SG_GROUPS, SG_GDIM, SG_CHUNK = 4, 128, 128
SG_W = SG_GROUPS * SG_GDIM
GD_HEADS, GD_DK, GD_DV, GD_CONV = 4, 128, 128, 4
GD_W = GD_HEADS * GD_DV
GD_CONV_CH = 2 * GD_HEADS * GD_DK + GD_W
N_BRANCH, BRANCH_W = 4, 512
LIN_CHUNK = 64
ROPE_BASE = 10000.0
EPS = 1e-6
NEG = -1e30
IN_SPLITS = (RET_HEADS * RET_DK, RET_HEADS * RET_DK, RET_W, RET_W,
             Q_LORA, KV_LORA, MLA_ROPE, MLA_W,
             SG_W, SG_W, SG_W,
             GD_CONV_CH, GD_HEADS, GD_HEADS, GD_W,
             N_BRANCH * D_MODEL)
_OFF = np.concatenate([[0], np.cumsum(IN_SPLITS)]).tolist()

LANES = 128
KCAT_W = KV_LORA + LANES
ATT_TK = 256
ATT_BQ = 256
ATT_LAG = 5
ATT_UNROLL = 2
GD_RB = 512
DEC_GROUP = 32
DEC_LOOK = 3
GD_BASE = 8
QSCALE = MLA_SCALE * math.log2(math.e)
VMEM_LIMIT = 56 * 1024 * 1024

_TB = (((1,), (1,)), ((), ()))
_TA = (((0,), (0,)), ((), ()))


def _dot(a, b):
    return jnp.dot(a, b, preferred_element_type=F32)


def _dot_tb(a, b):
    return lax.dot_general(a, b, _TB, preferred_element_type=F32)


def _dot_ta(a, b):
    return lax.dot_general(a, b, _TA, preferred_element_type=F32)


def _dot_mask(mask_bf16, x):
    hi = x.astype(BF16)
    r1 = x - hi.astype(F32)
    mid = r1.astype(BF16)
    lo = (r1 - mid.astype(F32)).astype(BF16)
    return _dot(mask_bf16, hi) + (_dot(mask_bf16, mid) + _dot(mask_bf16, lo))


def _silu(x):
    return x * jax.nn.sigmoid(x)


def _gelu(x):
    return 0.5 * x * (1.0 + lax.erf(x * np.float32(math.sqrt(0.5))))


def _rope(x, cos, sin_signed, d):
    w = x.shape[-1]
    half = d // 2
    lane = lax.broadcasted_iota(jnp.int32, x.shape, 1)
    first = (lane % d) < half
    rot = jnp.where(first, pltpu.roll(x, w - half, 1), pltpu.roll(x, half, 1))
    return x * cos + rot * sin_signed


def _const_spec(shape):
    nd = len(shape)
    return pl.BlockSpec(shape, lambda *_: (0,) * nd, pipeline_mode=pl.Buffered(1))


def _layer_spec(a, layer):
    nd = a.ndim - 1
    return pl.BlockSpec((None,) + a.shape[1:], lambda *_: (layer,) + (0,) * nd, pipeline_mode=pl.Buffered(1))


def _params(sem):
    return pltpu.CompilerParams(dimension_semantics=sem, vmem_limit_bytes=VMEM_LIMIT)


def _ret_kernel(*refs, R, NS, RB, cdec, has_s0):
    if has_s0:
        (x_ref, w_ref, cos_ref, sin_ref, gn_ref, dmat_ref, qdec_ref, kdec_ref, s0_ref,
         y_ref, sfin_ref) = refs
    else:
        (x_ref, w_ref, cos_ref, sin_ref, gn_ref, dmat_ref, qdec_ref, kdec_ref,
         y_ref, sfin_ref) = refs
    C = R // NS
    h = _dot(x_ref[...].astype(BF16), w_ref[...])
    cos = cos_ref[...]
    sin = sin_ref[...]
    hq = RET_HEADS * RET_DK
    q = _rope(h[:, 0:hq], cos, sin, RET_DK)
    k = _rope(h[:, hq:2 * hq], cos, sin, RET_DK) * (RET_DK ** -0.5)
    v = h[:, 2 * hq:2 * hq + RET_W]
    g = _silu(h[:, 2 * hq + RET_W:2 * hq + 2 * RET_W])

    if has_s0:
        sfin_ref[...] = s0_ref[...]
    else:
        @pl.when(pl.program_id(1) == 0)
        def _():
            sfin_ref[...] = jnp.zeros(sfin_ref.shape, F32)

    probs = [(c, hh) for c in range(RB // R) for hh in range(RET_HEADS)]
    qh, kh, vh, sc, kv = {}, {}, {}, {}, {}
    for pr in probs:
        c, hh = pr
        rs = slice(c * R, (c + 1) * R)
        qh[pr] = q[rs, hh * RET_DK:(hh + 1) * RET_DK].astype(BF16)
        kh[pr] = k[rs, hh * RET_DK:(hh + 1) * RET_DK].astype(BF16)
        vh[pr] = v[rs, hh * RET_DV:(hh + 1) * RET_DV]
        sc[pr] = _dot_tb(qh[pr], kh[pr])
        vk = (vh[pr] * kdec_ref[hh]).astype(BF16)
        for s in range(NS):
            kv[pr, s] = _dot_ta(kh[pr][s * C:(s + 1) * C], vk[s * C:(s + 1) * C])
    sprev = {}
    for pr in probs:
        c, hh = pr
        for s in range(NS):
            sidx = c * NS + s if has_s0 else 0
            st = sfin_ref[sidx, hh]
            sprev[pr, s] = st.astype(BF16)
            sfin_ref[sidx, hh] = st * cdec[hh] + kv[pr, s]
    for pr in probs:
        c, hh = pr
        rs = slice(c * R, (c + 1) * R)
        vs = slice(hh * RET_DV, (hh + 1) * RET_DV)
        o = _dot((sc[pr] * dmat_ref[hh]).astype(BF16), vh[pr].astype(BF16))
        qd = qdec_ref[hh]
        parts = [_dot(qh[pr][s * C:(s + 1) * C], sprev[pr, s]) * qd[s * C:(s + 1) * C] for s in range(NS)]
        o = o + (parts[0] if NS == 1 else jnp.concatenate(parts, axis=0))
        mu = jnp.mean(o, axis=-1, keepdims=True)
        d = o - mu
        var = jnp.mean(d * d, axis=-1, keepdims=True)
        on = d * lax.rsqrt(var + EPS) * gn_ref[:, vs]
        y_ref[rs, vs] = (g[rs, vs] * on).astype(y_ref.dtype)


def _ret_consts(R, NS):
    C = R // NS
    lg = np.log1p(-np.exp2(-5.0 - np.arange(RET_HEADS, dtype=np.float64)))
    idx = np.arange(R)
    tok = idx % C
    same = (idx[:, None] // C) == (idx[None, :] // C)
    rel = tok[:, None] - tok[None, :]
    low = same & (rel >= 0)
    dmat = np.where(low[None], np.exp(lg[:, None, None] * np.where(low, rel, 0)[None]), 0.0)
    qdec = np.exp(lg[:, None] * (tok[None, :] + 1.0))
    kdec = np.exp(lg[:, None] * (C - 1.0 - tok[None, :]))
    cdec = tuple(float(v) for v in np.exp(lg * C))
    bc = lambda a: jnp.asarray(np.broadcast_to(a[:, :, None], (RET_HEADS, R, LANES)), F32)
    return jnp.asarray(dmat, F32), bc(qdec), bc(kdec), cdec


def _ret_call(x, w, cos, sin, gn, s0, *, B, T, layer, out_dtype):
    N = x.shape[0]
    has_s0 = s0 is not None
    R = LIN_CHUNK
    if has_s0:
        C = math.gcd(T, LIN_CHUNK)
        assert C == T, "sample stream is a single chunk per sequence"
        NS = R // C
        RB = 2 * R
        nb, nt = N // RB, 1
        nsb = RB // C
    else:
        NS, RB = 1, 512
        assert T % RB == 0 and math.gcd(T, LIN_CHUNK) == R
        nb, nt = B, T // RB
        nsb = 1
    dmat, qdec, kdec, cdec = _ret_consts(R, NS)
    tr = cos.shape[0] // RB
    hq = RET_HEADS * RET_DK
    in_specs = [
        pl.BlockSpec((RB, D_MODEL), lambda b, t: (b * nt + t, 0)),
        _layer_spec(w, layer),
        pl.BlockSpec((RB, hq), lambda b, t: ((b * nt + t) % tr, 0)),
        pl.BlockSpec((RB, hq), lambda b, t: ((b * nt + t) % tr, 0)),
        _const_spec(gn.shape), _const_spec(dmat.shape), _const_spec(qdec.shape), _const_spec(kdec.shape),
    ]
    args = [x, w, cos, sin, gn, dmat, qdec, kdec]
    st_spec = pl.BlockSpec((nsb, RET_HEADS, RET_DK, RET_DV), lambda b, t: (b, 0, 0, 0))
    if has_s0:
        in_specs.append(pl.BlockSpec((None, nsb, RET_HEADS, RET_DK, RET_DV), lambda b, t: (layer, b, 0, 0, 0)))
        args.append(s0)
    nseq = N // T
    return pl.pallas_call(
        functools.partial(_ret_kernel, R=R, NS=NS, RB=RB, cdec=cdec, has_s0=has_s0),
        grid=(nb, nt),
        in_specs=in_specs,
        out_specs=[pl.BlockSpec((RB, RET_W), lambda b, t: (b * nt + t, 0)), st_spec],
        out_shape=[jax.ShapeDtypeStruct((N, RET_W), out_dtype),
                   jax.ShapeDtypeStruct((nseq, RET_HEADS, RET_DK, RET_DV), F32)],
        compiler_params=_params(("arbitrary", "arbitrary")),
        name="ret",
    )(*args)


def _mlap_kernel(x_ref, w_ref, wuq_ref, qn_ref, kvn_ref, cos_ref, sin_ref, *outs, TM, emit_kt):
    if emit_kt:
        qnope_ref, qpe_ref, ckv_ref, kpe_ref, gm_ref, kt_ref, vb_ref = outs
    else:
        qnope_ref, qpe_ref, ckv_ref, kpe_ref, gm_ref = outs
    h = _dot(x_ref[...].astype(BF16), w_ref[...])
    cq = h[:, 0:Q_LORA]
    cq = cq * lax.rsqrt(jnp.mean(cq * cq, axis=-1, keepdims=True) + EPS) * qn_ref[...]
    qf = _dot(cq.astype(BF16), wuq_ref[...])
    nw = MLA_HEADS * MLA_NOPE
    qnope_ref[...] = (qf[:, 0:nw] * QSCALE).astype(qnope_ref.dtype)
    cos = cos_ref[...]
    sin = sin_ref[...]
    qpe_ref[...] = _rope(qf[:, nw:nw + MLA_HEADS * MLA_ROPE], cos, sin, MLA_ROPE) * QSCALE
    c0 = Q_LORA
    kv = h[:, c0:c0 + KV_LORA]
    ckv = kv * lax.rsqrt(jnp.mean(kv * kv, axis=-1, keepdims=True) + EPS) * kvn_ref[...]
    ckv_ref[...] = ckv
    g0 = c0 + KV_LORA
    gm_ref[...] = _silu(h[:, g0:g0 + MLA_W]).astype(gm_ref.dtype)
    p0 = g0 + MLA_W
    kblk = _rope(h[:, p0:p0 + LANES], cos[:, 0:LANES], sin[:, 0:LANES], MLA_ROPE)
    kpe_ref[...] = kblk[:, 0:MLA_ROPE]
    if emit_kt:
        rep = kblk
        for i in range(1, LANES // MLA_ROPE):
            rep = rep + pltpu.roll(kblk, i * MLA_ROPE, 1)
        kct = jnp.concatenate([ckv, rep], axis=1).T
        for i in range(TM // ATT_TK):
            kt_ref[i] = kct[:, i * ATT_TK:(i + 1) * ATT_TK].astype(BF16)
        vb_ref[...] = ckv.astype(BF16)


def _mlap_call(x, w, wuq, qn, kvn, cos, sin, *, layer, act_dtype, emit_kt):
    N = x.shape[0]
    TM = 512
    assert N % TM == 0 and TM % ATT_TK == 0
    tr = cos.shape[0] // TM
    row = lambda wd: pl.BlockSpec((TM, wd), lambda i: (i, 0))
    tab = pl.BlockSpec((TM, MLA_HEADS * MLA_ROPE), lambda i: (i % tr, 0))
    nw = MLA_HEADS * MLA_NOPE
    out_specs = [row(nw), row(MLA_HEADS * MLA_ROPE), row(KV_LORA), row(MLA_ROPE), row(MLA_W)]
    out_shape = [jax.ShapeDtypeStruct((N, nw), act_dtype),
                 jax.ShapeDtypeStruct((N, MLA_HEADS * MLA_ROPE), F32),
                 jax.ShapeDtypeStruct((N, KV_LORA), F32),
                 jax.ShapeDtypeStruct((N, MLA_ROPE), F32),
                 jax.ShapeDtypeStruct((N, MLA_W), act_dtype)]
    if emit_kt:
        out_specs += [pl.BlockSpec((TM // ATT_TK, KCAT_W, ATT_TK), lambda i: (i, 0, 0)), row(KV_LORA)]
        out_shape += [jax.ShapeDtypeStruct((N // ATT_TK, KCAT_W, ATT_TK), BF16),
                      jax.ShapeDtypeStruct((N, KV_LORA), BF16)]
    return pl.pallas_call(
        functools.partial(_mlap_kernel, TM=TM, emit_kt=emit_kt),
        grid=(N // TM,),
        in_specs=[row(D_MODEL), _layer_spec(w, layer), _layer_spec(wuq, layer), _const_spec(qn.shape),
                  _const_spec(kvn.shape), tab, tab],
        out_specs=out_specs,
        out_shape=out_shape,
        compiler_params=_params(("arbitrary",)),
        name="mlap",
    )(x, w, wuq, qn, kvn, cos, sin)


def _attn_prefill_kernel(qn_ref, qpe_ref, kt_ref, v_ref, wuk_ref, gm_ref, wuv_ref, y_ref,
                         q_s, m_s, l_s, acc_s, *, BQ, TK):
    i = pl.program_id(1)
    H = MLA_HEADS
    lane = lax.broadcasted_iota(jnp.int32, (BQ, LANES), 1)
    for hh in range(H):
        ql = _dot(qn_ref[:, hh * MLA_NOPE:(hh + 1) * MLA_NOPE], wuk_ref[hh])
        q_s[hh * BQ:(hh + 1) * BQ, 0:KV_LORA] = ql.astype(BF16)
        per = LANES // MLA_ROPE
        blk = qpe_ref[:, (hh // per) * LANES:(hh // per + 1) * LANES]
        off = (hh % per) * MLA_ROPE
        keep = (lane >= off) & (lane < off + MLA_ROPE)
        q_s[hh * BQ:(hh + 1) * BQ, KV_LORA:KCAT_W] = jnp.where(keep, blk, 0.0).astype(BF16)
    m_s[...] = jnp.full(m_s.shape, NEG, F32)
    l_s[...] = jnp.zeros(l_s.shape, F32)
    acc_s[...] = jnp.zeros(acc_s.shape, F32)

    def steps(blocks):
        kts = [kt_ref[j] for j, _ in blocks]
        vbs = [v_ref[pl.ds(pl.multiple_of(j * TK, TK), TK), :] for j, _ in blocks]
        row = lax.broadcasted_iota(jnp.int32, (BQ, TK), 0)
        col = lax.broadcasted_iota(jnp.int32, (BQ, TK), 1)
        items = [(bi, hh) for bi in range(len(blocks)) for hh in range(H)]
        scores = {}
        for t in range(len(items) + ATT_LAG):
            if t < len(items):
                bi, hh = items[t]
                scores[t] = _dot(q_s[hh * BQ:(hh + 1) * BQ, :], kts[bi])
            if t < ATT_LAG:
                continue
            bi, hh = items[t - ATT_LAG]
            rs = slice(hh * BQ, (hh + 1) * BQ)
            s = scores.pop(t - ATT_LAG)
            if blocks[bi][1]:
                s = jnp.where(blocks[bi][0] * TK + col <= i * BQ + row, s, NEG)
            m_prev = m_s[rs, :]
            m_new = jnp.maximum(m_prev, jnp.max(s, axis=-1, keepdims=True))
            alpha = jnp.exp2(m_prev - m_new)
            p = jnp.exp2(s - jnp.concatenate([m_new] * (TK // LANES), axis=1))
            l_s[rs, :] = alpha[:, 0:1] * l_s[rs, :] + jnp.sum(p, axis=-1, keepdims=True)
            acc_s[rs, :] = (jnp.concatenate([alpha] * (KV_LORA // LANES), axis=1) * acc_s[rs, :]
                            + _dot(p.astype(BF16), vbs[bi]))
            m_s[rs, :] = m_new

    last = (i * BQ + BQ - 1) // TK
    npair = last // ATT_UNROLL

    def body(jj, carry):
        steps([(jj * ATT_UNROLL + u, False) for u in range(ATT_UNROLL)])
        return carry

    lax.fori_loop(0, npair, body, 0)

    def tail(j, carry):
        steps([(j, False)])
        return carry

    lax.fori_loop(npair * ATT_UNROLL, last, tail, 0)
    steps([(last, True)])

    for p2 in range(H // 2):
        halves = []
        for hh in (2 * p2, 2 * p2 + 1):
            rs = slice(hh * BQ, (hh + 1) * BQ)
            halves.append((acc_s[rs, :] / l_s[rs, :]).astype(BF16))
        yb = _dot(jnp.concatenate(halves, axis=1), wuv_ref[p2])
        y_ref[:, p2 * LANES:(p2 + 1) * LANES] = (gm_ref[:, p2 * LANES:(p2 + 1) * LANES].astype(F32) * yb).astype(y_ref.dtype)


def _attn_prefill_call(qn, qpe, kt, vb, wuk, gm, wuv, *, B, T, layer):
    N = qn.shape[0]
    BQ, TK = ATT_BQ, ATT_TK
    assert T % TK == 0 and T % BQ == 0
    nq = T // BQ
    H = MLA_HEADS
    row = lambda wd: pl.BlockSpec((BQ, wd), lambda b, i: (b * nq + i, 0))
    return pl.pallas_call(
        functools.partial(_attn_prefill_kernel, BQ=BQ, TK=TK),
        grid=(B, nq),
        in_specs=[row(H * MLA_NOPE), row(H * MLA_ROPE),
                  pl.BlockSpec((T // TK, KCAT_W, TK), lambda b, i: (b, 0, 0)),
                  pl.BlockSpec((T, KV_LORA), lambda b, i: (b, 0)),
                  _layer_spec(wuk, layer), row(MLA_W), _layer_spec(wuv, layer)],
        out_specs=row(MLA_W),
        out_shape=jax.ShapeDtypeStruct((N, MLA_W), BF16),
        scratch_shapes=[pltpu.VMEM((H * BQ, KCAT_W), BF16), pltpu.VMEM((H * BQ, LANES), F32),
                        pltpu.VMEM((H * BQ, 1), F32), pltpu.VMEM((H * BQ, KV_LORA), F32)],
        compiler_params=_params(("arbitrary", "arbitrary")),
        name="attn_prefill",
    )(qn, qpe, kt, vb, wuk, gm, wuv)


def _attn_decode_kernel(pt_ref, qn_ref, qpe_ref, ckvn_ref, kpen_ref, gm_ref, wuk_ref, wuv_ref,
                        cckv_ref, ckpe_ref, y_ref,
                        kbuf, pbuf, sem, *, T, G, NSTEP, NSUB, LOOK, layer, PAGE):
    b = pl.program_id(0)
    nb = pl.num_programs(0)
    H = MLA_HEADS

    def copies(bb, jj):
        out = []
        for g in range(G):
            page = pt_ref[bb, jj * G + g]
            out.append(pltpu.make_async_copy(cckv_ref.at[layer, page], kbuf.at[jj, g], sem.at[0, jj]))
            out.append(pltpu.make_async_copy(ckpe_ref.at[layer, page], pbuf.at[jj, g], sem.at[1, jj]))
        return out

    @pl.when(b == 0)
    def _():
        for jj in range(LOOK):
            for c in copies(b, jj):
                c.start()

    ql = jnp.concatenate([_dot(qn_ref[:, hh * MLA_NOPE:(hh + 1) * MLA_NOPE].astype(BF16), wuk_ref[hh])
                          for hh in range(H)], axis=0).astype(BF16)
    qp = jnp.concatenate([qpe_ref[:, hh * MLA_ROPE:(hh + 1) * MLA_ROPE] for hh in range(H)], axis=0).astype(BF16)
    kn = ckvn_ref[...].astype(BF16)
    s = _dot_tb(ql, kn) + _dot_tb(qp, kpen_ref[...].astype(BF16))
    row = lax.broadcasted_iota(jnp.int32, s.shape, 0)
    col = lax.broadcasted_iota(jnp.int32, s.shape, 1)
    s = jnp.where(col <= row % T, s, NEG)
    m = jnp.max(s, axis=-1, keepdims=True)
    p = jnp.exp2(s - m)
    l = jnp.sum(p, axis=-1, keepdims=True)
    acc = _dot(p.astype(BF16), kn)

    GS = G // NSUB
    for j in range(NSTEP):
        jn = j + LOOK
        if jn < NSTEP:
            for c in copies(b, jn):
                c.start()
        else:
            @pl.when(b + 1 < nb)
            def _():
                for c in copies(b + 1, jn - NSTEP):
                    c.start()
        for c in copies(b, j):
            c.wait()
        kbs, scores = [], []
        for u in range(NSUB):
            kb = kbuf[j, u * GS:(u + 1) * GS].reshape(GS * PAGE, KV_LORA).astype(BF16)
            pcat = jnp.concatenate([pbuf[j, u * GS + g] for g in range(GS)], axis=1).astype(BF16)
            kbs.append(kb)
            scores.append(_dot_tb(ql, kb) + _dot(qp, pcat))
        for u in range(NSUB):
            kb, s = kbs[u], scores[u]
            m_new = jnp.maximum(m, jnp.max(s, axis=-1, keepdims=True))
            alpha = jnp.exp2(m - m_new)
            p = jnp.exp2(s - m_new)
            l = alpha * l + jnp.sum(p, axis=-1, keepdims=True)
            acc = alpha * acc + _dot(p.astype(BF16), kb)
            m = m_new

    o = acc / l
    for p2 in range(H // 2):
        lhs = jnp.concatenate([o[(2 * p2) * T:(2 * p2 + 1) * T], o[(2 * p2 + 1) * T:(2 * p2 + 2) * T]], axis=1)
        yb = _dot(lhs.astype(BF16), wuv_ref[p2])
        y_ref[:, p2 * LANES:(p2 + 1) * LANES] = gm_ref[:, p2 * LANES:(p2 + 1) * LANES] * yb


def _attn_decode_call(page_table, qn, qpe, ckv, kpe, gm, wuk, wuv, cache_ckv, cache_kpet, *, B, T, layer):
    N = qn.shape[0]
    n_pages = page_table.shape[1]
    PAGE = cache_ckv.shape[2]
    G = math.gcd(n_pages, DEC_GROUP)
    if n_pages // G < 2 and G % 2 == 0:
        G //= 2
    NSTEP = n_pages // G
    assert NSTEP >= 2, "the page groups of one sample are double buffered"
    page_bytes = PAGE * (KV_LORA + MLA_ROPE) * 4
    assert n_pages * page_bytes <= VMEM_LIMIT // 2, "one sample's pages must fit the VMEM buffers"
    LOOK = min(DEC_LOOK, NSTEP - 1)
    NSUB = math.gcd(G, 4)
    H = MLA_HEADS
    row = lambda wd: pl.BlockSpec((T, wd), lambda b, pt: (b, 0))
    grid_spec = pltpu.PrefetchScalarGridSpec(
        num_scalar_prefetch=1,
        grid=(B,),
        in_specs=[row(H * MLA_NOPE), row(H * MLA_ROPE), row(KV_LORA), row(MLA_ROPE), row(MLA_W),
                  _layer_spec(wuk, layer), _layer_spec(wuv, layer),
                  pl.BlockSpec(memory_space=pl.ANY), pl.BlockSpec(memory_space=pl.ANY)],
        out_specs=row(MLA_W),
        scratch_shapes=[pltpu.VMEM((NSTEP, G, PAGE, KV_LORA), F32), pltpu.VMEM((NSTEP, G, MLA_ROPE, PAGE), F32),
                        pltpu.SemaphoreType.DMA((2, NSTEP))],
    )
    return pl.pallas_call(
        functools.partial(_attn_decode_kernel, T=T, G=G, NSTEP=NSTEP, NSUB=NSUB, LOOK=LOOK, layer=layer, PAGE=PAGE),
        grid_spec=grid_spec,
        out_shape=jax.ShapeDtypeStruct((N, MLA_W), F32),
        compiler_params=_params(("arbitrary",)),
        name="attn_decode",
    )(page_table, qn, qpe, ckv, kpe, gm, wuk, wuv, cache_ckv, cache_kpet)


def _sg_kernel(x_ref, w_ref, lng_ref, lnb_ref, wt_ref, bs_ref, y_ref, *rest, TM, emit_v):
    h = _dot(x_ref[...].astype(BF16), w_ref[...])
    u = _gelu(h[:, 0:SG_W])
    gv = _gelu(h[:, SG_W:2 * SG_W])
    mu = jnp.mean(gv, axis=-1, keepdims=True)
    d = gv - mu
    var = jnp.mean(d * d, axis=-1, keepdims=True)
    v = d * lax.rsqrt(var + EPS) * lng_ref[...] + lnb_ref[...]
    if emit_v:
        rest[0][...] = v
    gs = _silu(h[:, 2 * SG_W:3 * SG_W])
    vb = v.astype(BF16)
    for c in range(TM // SG_CHUNK):
        rows = slice(c * SG_CHUNK, (c + 1) * SG_CHUNK)
        for g in range(SG_GROUPS):
            cols = slice(g * SG_GDIM, (g + 1) * SG_GDIM)
            s = _dot(wt_ref[g], vb[rows, cols]) + bs_ref[g]
            y_ref[rows, cols] = (gs[rows, cols] * (u[rows, cols] * s)).astype(y_ref.dtype)


def _sg_call(x, w, lng, lnb, wt, bs, *, layer, emit_v, out_dtype):
    N = x.shape[0]
    TM = 512
    assert N % TM == 0
    row = lambda wd: pl.BlockSpec((TM, wd), lambda i: (i, 0))
    out_specs = [row(SG_W)]
    out_shape = [jax.ShapeDtypeStruct((N, SG_W), out_dtype)]
    if emit_v:
        out_specs.append(row(SG_W))
        out_shape.append(jax.ShapeDtypeStruct((N, SG_W), F32))
    return pl.pallas_call(
        functools.partial(_sg_kernel, TM=TM, emit_v=emit_v),
        grid=(N // TM,),
        in_specs=[row(D_MODEL), _layer_spec(w, layer), _const_spec(lng.shape), _const_spec(lnb.shape),
                  _layer_spec(wt, layer), _layer_spec(bs, layer)],
        out_specs=out_specs,
        out_shape=out_shape,
        compiler_params=_params(("arbitrary",)),
        name="sg",
    )(x, w, lng, lnb, wt, bs)


def _gd_kernel(*refs, R, NS, RB, has_s0):
    if has_s0:
        (x_ref, w_ref, cw_ref, av_ref, dtb_ref, ng_ref, msk_ref, bufp_ref, s0_ref,
         y_ref, sfin_ref, conv_ref) = refs
    else:
        (x_ref, w_ref, cw_ref, av_ref, dtb_ref, ng_ref, msk_ref,
         y_ref, sfin_ref, conv_ref, carry_s) = refs
    C = R // NS
    NLEV = _gd_levels(C)
    CH = GD_CONV_CH
    h = _dot(x_ref[...].astype(BF16), w_ref[...])
    xq = h[:, 0:CH]
    y = xq * cw_ref[GD_CONV - 1:GD_CONV, :]
    if has_s0:
        x3 = xq.reshape(RB // 8, 8, CH)
        b3 = bufp_ref[...].reshape(RB // 8, 8, CH)
        tpos = lax.broadcasted_iota(jnp.int32, x3.shape, 1)
        for k in range(1, GD_CONV):
            sh = jnp.where(tpos >= k, pltpu.roll(x3, k, 1), pltpu.roll(b3, k, 1))
            y = y + sh.reshape(RB, CH) * cw_ref[GD_CONV - 1 - k:GD_CONV - k, :]
        for s in range(RB // C):
            conv_ref[s] = xq[s * C + C - (GD_CONV - 1):s * C + C, :]
    else:
        @pl.when(pl.program_id(1) == 0)
        def _():
            carry_s[0:8, :] = jnp.zeros((8, CH), F32)
        carry_s[8:8 + RB, :] = xq
        for k in range(1, GD_CONV):
            y = y + carry_s[8 - k:8 - k + RB, :] * cw_ref[GD_CONV - 1 - k:GD_CONV - k, :]
        carry_s[0:8, :] = xq[RB - 8:RB]
        conv_ref[0] = xq[RB - (GD_CONV - 1):RB, :]
    qkv = _silu(y)
    hw = GD_HEADS * GD_DK
    qn, kn = [], []
    for hh in range(GD_HEADS):
        qb = qkv[:, hh * GD_DK:(hh + 1) * GD_DK]
        qn.append(qb * lax.rsqrt(jnp.sum(qb * qb, axis=-1, keepdims=True) + EPS) * (GD_DK ** -0.5))
        kb = qkv[:, hw + hh * GD_DK:hw + (hh + 1) * GD_DK]
        kn.append(kb * lax.rsqrt(jnp.sum(kb * kb, axis=-1, keepdims=True) + EPS))
    vv = qkv[:, 2 * hw:2 * hw + GD_W]
    gate = _silu(h[:, CH:CH + GD_W])
    ab = h[:, CH + GD_W:CH + GD_W + LANES]
    lane = lax.broadcasted_iota(jnp.int32, ab.shape, 1)
    gb = jnp.where(lane < GD_HEADS, av_ref[...] * jax.nn.softplus(ab + dtb_ref[...]), jax.nn.sigmoid(ab))

    if has_s0:
        sfin_ref[...] = s0_ref[...]
    else:
        @pl.when(pl.program_id(1) == 0)
        def _():
            sfin_ref[...] = jnp.zeros(sfin_ref.shape, F32)

    lowf = msk_ref[0]
    low = lowf > 0.0
    strictf = msk_ref[1]
    lowb = lowf.astype(BF16)
    sameb = msk_ref[2].astype(BF16)
    ri = lax.broadcasted_iota(jnp.int32, (R, R), 0)
    ci = lax.broadcasted_iota(jnp.int32, (R, R), 1)
    eye = (ri == ci).astype(F32)

    nch = RB // R
    probs = [(c, hh) for c in range(nch) for hh in range(GD_HEADS)]
    gbc = [gb[c * R:(c + 1) * R] for c in range(nch)]
    gc4 = [_dot_mask(lowb, g) for g in gbc]
    gl4 = [_dot_mask(sameb, g) for g in gbc]
    qh, kh, qkk = {}, {}, {}
    for pr in probs:
        c, hh = pr
        rs = slice(c * R, (c + 1) * R)
        qh[pr], kh[pr] = qn[hh][rs], kn[hh][rs]
        khb = kh[pr].astype(BF16)
        qkk[pr] = _dot_tb(jnp.concatenate([qh[pr].astype(BF16), khb], axis=0), khb)
    gcb, glb, beta, gam, qk, amat, mp, pinv = {}, {}, {}, {}, {}, {}, {}, {}
    for pr in probs:
        c, hh = pr
        gcb[pr] = jnp.broadcast_to(gc4[c][:, hh:hh + 1], (R, LANES))
        glb[pr] = jnp.broadcast_to(gl4[c][:, hh:hh + 1], (R, LANES))
        beta[pr] = jnp.broadcast_to(gbc[c][:, GD_HEADS + hh:GD_HEADS + hh + 1], (R, LANES))
        gi = gcb[pr][:, 0:R]
        gam[pr] = jnp.where(low, jnp.exp(jnp.where(low, gi - gi.T, 0.0)), 0.0)
        qk[pr] = (qkk[pr][0:R] * gam[pr]).astype(BF16)
        amat[pr] = strictf * (beta[pr][:, 0:R] * qkk[pr][R:2 * R] * gam[pr])
        mp[pr] = -(msk_ref[3] * amat[pr])
        pinv[pr] = eye + mp[pr]
    for _ in range(int(math.log2(min(C, GD_BASE))) - 1):
        for pr in probs:
            mpb = mp[pr].astype(BF16)
            mp[pr] = _dot(mpb, mpb)
        for pr in probs:
            pinv[pr] = pinv[pr] + _dot(pinv[pr].astype(BF16), mp[pr].astype(BF16))
    for lev in range(NLEV):
        wl = {pr: _dot((msk_ref[4 + lev] * amat[pr]).astype(BF16), pinv[pr].astype(BF16)) for pr in probs}
        for pr in probs:
            pinv[pr] = pinv[pr] - _dot(pinv[pr].astype(BF16), wl[pr].astype(BF16))
    pre = {}
    for pr in probs:
        c, hh = pr
        vh = vv[c * R:(c + 1) * R, hh * GD_DK:(hh + 1) * GD_DK]
        eg = jnp.exp(gcb[pr])
        rhs = jnp.concatenate([vh * beta[pr], kh[pr] * (beta[pr] * eg)], axis=1)
        sol = _dot(pinv[pr].astype(BF16), rhs.astype(BF16))
        pre[pr] = (sol[:, 0:GD_DV], sol[:, GD_DV:GD_DV + GD_DK].astype(BF16),
                   (qh[pr] * eg).astype(BF16), (kh[pr] * jnp.exp(glb[pr] - gcb[pr])).astype(BF16),
                   jnp.exp(glb[pr]))

    for c in range(nch):
        rs = slice(c * R, (c + 1) * R)
        seqs = [(hh, s) for hh in range(GD_HEADS) for s in range(NS)]
        st, sb, ws, os_ = {}, {}, {}, {}
        for hh, s in seqs:
            sidx = c * NS + s if has_s0 else 0
            st[hh, s] = sfin_ref[sidx, hh]
            sb[hh, s] = st[hh, s].astype(BF16)
        for hh, s in seqs:
            rows = slice(s * C, (s + 1) * C)
            u, wb, qd, kd, el = pre[c, hh]
            ws[hh, s] = _dot(wb[rows], sb[hh, s])
            os_[hh, s] = _dot(qd[rows], sb[hh, s])
        vnew = {}
        for hh in range(GD_HEADS):
            u = pre[c, hh][0]
            vns = [u[s * C:(s + 1) * C] - ws[hh, s] for s in range(NS)]
            vnew[hh] = vns[0] if NS == 1 else jnp.concatenate(vns, axis=0)
        for hh, s in seqs:
            rows = slice(s * C, (s + 1) * C)
            kd, el = pre[c, hh][3], pre[c, hh][4]
            sidx = c * NS + s if has_s0 else 0
            sfin_ref[sidx, hh] = (st[hh, s] * el[s * C:s * C + 1, :]
                                  + _dot_ta(kd[rows], vnew[hh][rows].astype(BF16)))
        for hh in range(GD_HEADS):
            sl = slice(hh * GD_DK, (hh + 1) * GD_DK)
            oi = [os_[hh, s] for s in range(NS)]
            o = (oi[0] if NS == 1 else jnp.concatenate(oi, axis=0)) + _dot(qk[c, hh], vnew[hh].astype(BF16))
            on = o * lax.rsqrt(jnp.mean(o * o, axis=-1, keepdims=True) + EPS) * ng_ref[...]
            y_ref[rs, sl] = (gate[rs, sl] * on).astype(y_ref.dtype)


def _gd_levels(C):
    base = min(C, GD_BASE)
    assert C % base == 0 and (C // base) & (C // base - 1) == 0 and base & (base - 1) == 0
    return int(math.log2(C // base))


def _gd_masks(R, NS):
    C = R // NS
    idx = np.arange(R)
    blk = lambda s: (idx[:, None] // s) == (idx[None, :] // s)
    same = blk(C)
    low = same & (idx[:, None] >= idx[None, :])
    strict = same & (idx[:, None] > idx[None, :])
    base = min(C, GD_BASE)
    masks = [low, strict, same, blk(base)]
    for lev in range(_gd_levels(C)):
        s = base * 2 ** (lev + 1)
        masks.append(blk(s) & ~blk(s // 2))
    return jnp.asarray(np.stack(masks).astype(np.float32))


def _gd_call(x, w, cw, av, dtb, ng, bufp, s0, *, B, T, layer, out_dtype):
    N = x.shape[0]
    has_s0 = s0 is not None
    R = LIN_CHUNK
    CH = GD_CONV_CH
    if has_s0:
        C = math.gcd(T, LIN_CHUNK)
        assert C == T and T == 8 and T >= GD_CONV - 1
        NS = R // C
        RB = 2 * R
        nb, nt = N // RB, 1
        nsb = RB // C
    else:
        NS, RB = 1, GD_RB
        assert T % RB == 0 and math.gcd(T, LIN_CHUNK) == R
        nb, nt = B, T // RB
        nsb = 1
    msk = _gd_masks(R, NS)
    rowspec = lambda wd: pl.BlockSpec((RB, wd), lambda b, t: (b * nt + t, 0))
    in_specs = [rowspec(D_MODEL), _layer_spec(w, layer), _layer_spec(cw, layer), _const_spec(av.shape),
                _const_spec(dtb.shape), _const_spec(ng.shape), _const_spec(msk.shape)]
    args = [x, w, cw, av, dtb, ng, msk]
    st_spec = pl.BlockSpec((nsb, GD_HEADS, GD_DK, GD_DV), lambda b, t: (b, 0, 0, 0))
    scratch = []
    if has_s0:
        in_specs += [pl.BlockSpec((None, RB, CH), lambda b, t: (layer, b * nt + t, 0)),
                     pl.BlockSpec((None, nsb, GD_HEADS, GD_DK, GD_DV), lambda b, t: (layer, b, 0, 0, 0))]
        args += [bufp, s0]
    else:
        scratch.append(pltpu.VMEM((8 + RB, CH), F32))
    nseq = N // T
    return pl.pallas_call(
        functools.partial(_gd_kernel, R=R, NS=NS, RB=RB, has_s0=has_s0),
        grid=(nb, nt),
        in_specs=in_specs,
        out_specs=[rowspec(GD_W), st_spec,
                   pl.BlockSpec((nsb, GD_CONV - 1, CH), lambda b, t: (b, 0, 0))],
        out_shape=[jax.ShapeDtypeStruct((N, GD_W), out_dtype),
                   jax.ShapeDtypeStruct((nseq, GD_HEADS, GD_DK, GD_DV), F32),
                   jax.ShapeDtypeStruct((nseq, GD_CONV - 1, CH), F32)],
        scratch_shapes=scratch,
        compiler_params=_params(("arbitrary", "arbitrary")),
        name="gd",
    )(*args)


def _back_kernel(x_ref, ya_ref, yb_ref, yc_ref, yd_ref, wm_ref, wb_ref, wo_ref, lg_ref, lb_ref, o_ref, *, alpha):
    x = x_ref[...]
    gates = jax.nn.sigmoid(_dot(x.astype(BF16), wm_ref[...]))
    merged = None
    for i, yr in enumerate((ya_ref, yb_ref, yc_ref, yd_ref)):
        t = gates[:, i * D_MODEL:(i + 1) * D_MODEL] * _dot(yr[...].astype(BF16), wb_ref[i])
        merged = t if merged is None else merged + t
    z = alpha * x + _dot(merged.astype(BF16), wo_ref[...])
    mu = jnp.mean(z, axis=-1, keepdims=True)
    d = z - mu
    var = jnp.mean(d * d, axis=-1, keepdims=True)
    o_ref[...] = d * lax.rsqrt(var + EPS) * lg_ref[...] + lb_ref[...]


def _back_call(x, ya, yb, yc, yd, wm, wb, wo, lg, lb, *, layer, alpha):
    N = x.shape[0]
    TM = 256
    assert N % TM == 0
    row = lambda wd: pl.BlockSpec((TM, wd), lambda i: (i, 0))
    return pl.pallas_call(
        functools.partial(_back_kernel, alpha=alpha),
        grid=(N // TM,),
        in_specs=[row(D_MODEL), row(BRANCH_W), row(BRANCH_W), row(BRANCH_W), row(BRANCH_W),
                  _layer_spec(wm, layer), _layer_spec(wb, layer), _layer_spec(wo, layer),
                  _const_spec(lg.shape), _const_spec(lb.shape)],
        out_specs=row(D_MODEL),
        out_shape=jax.ShapeDtypeStruct((N, D_MODEL), F32),
        compiler_params=_params(("arbitrary",)),
        name="back",
    )(x, ya, yb, yc, yd, wm, wb, wo, lg, lb)


def _rope_tables(pos, d, reps, rows):
    inv = ROPE_BASE ** (-jnp.arange(0, d, 2, dtype=F32) / d)
    ang = pos.astype(F32)[:, None] * inv[None, :]
    c, s = jnp.cos(ang), jnp.sin(ang)
    cos = jnp.tile(jnp.concatenate([c, c], axis=-1), (rows // pos.shape[0], reps))
    sin = jnp.tile(jnp.concatenate([-s, s], axis=-1), (rows // pos.shape[0], reps))
    return cos, sin


def _prep_weights(w_in, mla_w_uq, mla_w_uk, mla_w_uv, w_branch, w_out):
    o = _OFF
    cols = lambda a, b: w_in[:, :, o[a]:o[b]]
    depth = w_in.shape[0]
    zpad = lambda n: jnp.zeros((depth, D_MODEL, n), w_in.dtype)
    w_ret = cols(0, 4).astype(BF16)
    w_mla = jnp.concatenate([cols(4, 6), cols(7, 8), cols(6, 7), zpad(LANES - MLA_ROPE)], axis=-1).astype(BF16)
    w_sg = cols(8, 11).astype(BF16)
    w_gd = jnp.concatenate([cols(11, 12), cols(14, 15), cols(12, 14), zpad(LANES - 2 * GD_HEADS)], axis=-1).astype(BF16)
    w_merge = cols(15, 16).astype(BF16)
    uq = mla_w_uq.reshape(depth, Q_LORA, MLA_HEADS, MLA_NOPE + MLA_ROPE)
    w_uq = jnp.concatenate([uq[..., :MLA_NOPE].reshape(depth, Q_LORA, -1),
                            uq[..., MLA_NOPE:].reshape(depth, Q_LORA, -1)], axis=-1).astype(BF16)
    w_uk = jnp.transpose(mla_w_uk, (0, 2, 3, 1)).astype(BF16)
    uv = mla_w_uv.reshape(depth, KV_LORA, MLA_HEADS // 2, 2, MLA_V)
    z = jnp.zeros((depth, KV_LORA, MLA_HEADS // 2, MLA_V), mla_w_uv.dtype)
    top = jnp.concatenate([uv[:, :, :, 0], z], axis=-1)
    bot = jnp.concatenate([z, uv[:, :, :, 1]], axis=-1)
    w_uv = jnp.transpose(jnp.concatenate([top, bot], axis=1), (0, 2, 1, 3)).astype(BF16)
    return dict(w_ret=w_ret, w_mla=w_mla, w_sg=w_sg, w_gd=w_gd, w_merge=w_merge, w_uq=w_uq, w_uk=w_uk, w_uv=w_uv,
                w_branch=w_branch.astype(BF16), w_out=w_out.astype(BF16))


def _sg_mix_weights(sg_w, sg_b, C):
    reps = SG_CHUNK // C
    w = jnp.tril(sg_w[:, :, :C, :C])
    if reps > 1:
        eye = jnp.eye(reps, dtype=w.dtype)
        w = jnp.einsum('ab,dgij->dgaibj', eye, w).reshape(w.shape[0], SG_GROUPS, SG_CHUNK, SG_CHUNK)
    b = jnp.tile(sg_b[:, :, :C], (1, 1, reps))
    bs = jnp.broadcast_to(b[..., None], b.shape + (SG_GDIM,))
    return w.astype(BF16), bs.astype(F32)


def _layer_stream(x, l, W, P, *, B, T, tabs, sample):
    cosr, sinr, cosm, sinm = tabs
    act = F32 if sample is not None else BF16
    ya, s_ret = _ret_call(x, W['w_ret'], cosr, sinr, P['ret_gn_g'][l][None],
                          sample['state_ret'] if sample else None, B=B, T=T, layer=l, out_dtype=act)
    mo = _mlap_call(x, W['w_mla'], W['w_uq'], P['mla_q_norm'][l][None],
                    P['mla_kv_norm'][l][None], cosm, sinm, layer=l, act_dtype=act, emit_kt=sample is None)
    qn, qpe, ckv, kpe, gm = mo[:5]
    if sample:
        yb = _attn_decode_call(sample['page_table'], qn, qpe, ckv, kpe, gm, W['w_uk'], W['w_uv'],
                               sample['cache_ckv'], sample['cache_kpet'], B=B, T=T, layer=l)
    else:
        yb = _attn_prefill_call(qn, qpe, mo[5], mo[6], W['w_uk'], gm, W['w_uv'], B=B, T=T, layer=l)
    wt, bs = (P['sg_wt_s'], P['sg_bs_s']) if sample else (P['sg_wt_p'], P['sg_bs_p'])
    sg_out = _sg_call(x, W['w_sg'], P['sg_ln_g'][l][None], P['sg_ln_b'][l][None], wt, bs,
                      layer=l, emit_v=sample is not None, out_dtype=act)
    yc = sg_out[0]
    sgv = sg_out[1] if sample else None
    yd, s_gd, conv_new = _gd_call(x, W['w_gd'], P['gd_conv_w'], P['gd_av'][l], P['gd_dtb'][l],
                                  P['gd_norm_g'][l][None],
                                  sample['bufp'] if sample else None,
                                  sample['state_delta'] if sample else None, B=B, T=T, layer=l, out_dtype=act)
    xn = _back_call(x, ya, yb, yc, yd, W['w_merge'], W['w_branch'], W['w_out'],
                    P['ln_g'][l][None], P['ln_b'][l][None], layer=l, alpha=P['alpha'])
    return xn, ckv, kpe, s_ret, s_gd, conv_new, sgv


def kernel(x_prompt, x_sample, cache_ckv, cache_kpe, state_ret, state_delta, state_conv, page_table,
           w_in, ret_gn_g, mla_q_norm, mla_w_uq, mla_kv_norm, mla_w_uk, mla_w_uv,
           sg_ln_g, sg_ln_b, sg_w, sg_b, gd_conv_w, gd_a_log, gd_dt_bias, gd_norm_g,
           w_branch, w_out, ln_g, ln_b):
    depth = w_in.shape[0]
    bp, tp, _ = x_prompt.shape
    bs_, ts, _ = x_sample.shape
    past_len = page_table.shape[1] * cache_ckv.shape[2]
    W = _prep_weights(w_in, mla_w_uq, mla_w_uk, mla_w_uv, w_branch, w_out)
    pad_small = lambda v: jnp.pad(v, ((0, 0), (0, LANES - v.shape[-1])))[:, None, :]
    P = dict(ret_gn_g=ret_gn_g, mla_q_norm=mla_q_norm, mla_kv_norm=mla_kv_norm, sg_ln_g=sg_ln_g, sg_ln_b=sg_ln_b,
             gd_conv_w=gd_conv_w, gd_norm_g=gd_norm_g, ln_g=ln_g, ln_b=ln_b,
             gd_av=pad_small(-jnp.exp(gd_a_log)), gd_dtb=pad_small(gd_dt_bias),
             alpha=float((2.0 * depth) ** 0.25))
    P['sg_wt_p'], P['sg_bs_p'] = _sg_mix_weights(sg_w, sg_b, min(tp, SG_CHUNK))
    P['sg_wt_s'], P['sg_bs_s'] = _sg_mix_weights(sg_w, sg_b, min(ts, SG_CHUNK))

    pos_p = jnp.arange(tp)
    pos_s = past_len + jnp.arange(ts)
    tab_rows_s = 512
    tabs_p = _rope_tables(pos_p, RET_DK, RET_HEADS, tp) + _rope_tables(pos_p, MLA_ROPE, MLA_HEADS, tp)
    tabs_s = _rope_tables(pos_s, RET_DK, RET_HEADS, tab_rows_s) + _rope_tables(pos_s, MLA_ROPE, MLA_HEADS, tab_rows_s)

    bufp = jnp.pad(state_conv, ((0, 0), (0, 0), (ts - (GD_CONV - 1), 0), (0, 0))).reshape(depth, bs_ * ts, GD_CONV_CH)
    sample = dict(state_ret=state_ret, state_delta=state_delta, bufp=bufp, page_table=page_table,
                  cache_ckv=cache_ckv, cache_kpet=jnp.swapaxes(cache_kpe, 2, 3))

    xp = x_prompt.reshape(bp * tp, D_MODEL)
    xs = x_sample.reshape(bs_ * ts, D_MODEL)
    outs_p, outs_s = [], []
    for l in range(depth):
        xp, *rest = _layer_stream(xp, l, W, P, B=bp, T=tp, tabs=tabs_p, sample=None)
        outs_p.append(rest)
        xs, *rest = _layer_stream(xs, l, W, P, B=bs_, T=ts, tabs=tabs_s, sample=sample)
        outs_s.append(rest)

    def stack(outs, k, shape):
        return jnp.stack([o[k].reshape(shape) for o in outs])

    return (xp.reshape(bp, tp, D_MODEL), xs.reshape(bs_, ts, D_MODEL),
            stack(outs_p, 0, (bp, tp, KV_LORA)), stack(outs_p, 1, (bp, tp, MLA_ROPE)),
            stack(outs_p, 2, (bp, RET_HEADS, RET_DK, RET_DV)), stack(outs_p, 3, (bp, GD_HEADS, GD_DK, GD_DV)),
            stack(outs_p, 4, (bp, GD_CONV - 1, GD_CONV_CH)),
            stack(outs_s, 0, (bs_, ts, KV_LORA)), stack(outs_s, 1, (bs_, ts, MLA_ROPE)),
            stack(outs_s, 2, (bs_, RET_HEADS, RET_DK, RET_DV)), stack(outs_s, 3, (bs_, GD_HEADS, GD_DK, GD_DV)),
            stack(outs_s, 4, (bs_, GD_CONV - 1, GD_CONV_CH)), stack(outs_s, 5, (bs_, ts, SG_W)))
```

```python
import functools
import math

import numpy as np
import jax
import jax.numpy as jnp
from jax import lax
from jax.experimental import pallas as pl
from jax.experimental.pallas import tpu as pltpu

F32 = jnp.float32
BF16 = jnp.bfloat16

D_MODEL = 1024
RET_HEADS, RET_DK, RET_DV = 4, 64, 128
RET_W = RET_HEADS * RET_DV
MLA_HEADS, MLA_NOPE, MLA_ROPE, MLA_V = 8, 64, 32, 64
Q_LORA, KV_LORA = 384, 256
MLA_W = MLA_HEADS * MLA_V
MLA_SCALE = (MLA_NOPE + MLA_ROPE) ** -0.5
SG_GROUPS, SG_GDIM, SG_CHUNK = 4, 128, 128
SG_W = SG_GROUPS * SG_GDIM
GD_HEADS, GD_DK, GD_DV, GD_CONV = 4, 128, 128, 4
GD_W = GD_HEADS * GD_DV
GD_CONV_CH = 2 * GD_HEADS * GD_DK + GD_W
N_BRANCH, BRANCH_W = 4, 512
LIN_CHUNK = 64
ROPE_BASE = 10000.0
EPS = 1e-6
NEG = -1e30
IN_SPLITS = (RET_HEADS * RET_DK, RET_HEADS * RET_DK, RET_W, RET_W,
             Q_LORA, KV_LORA, MLA_ROPE, MLA_W,
             SG_W, SG_W, SG_W,
             GD_CONV_CH, GD_HEADS, GD_HEADS, GD_W,
             N_BRANCH * D_MODEL)
_OFF = np.concatenate([[0], np.cumsum(IN_SPLITS)]).tolist()

LANES = 128
KCAT_W = KV_LORA + LANES
ATT_TK = 256
ATT_BQ = 256
ATT_LAG = 6
ATT_UNROLL = 3
GD_RB = 512
DEC_GROUP = 32
DEC_LOOK = 3
GD_BASE = 8
QSCALE = MLA_SCALE * math.log2(math.e)
VMEM_LIMIT = 56 * 1024 * 1024

_TB = (((1,), (1,)), ((), ()))
_TA = (((0,), (0,)), ((), ()))


def _dot(a, b):
    return jnp.dot(a, b, preferred_element_type=F32)


def _dot_tb(a, b):
    return lax.dot_general(a, b, _TB, preferred_element_type=F32)


def _dot_ta(a, b):
    return lax.dot_general(a, b, _TA, preferred_element_type=F32)


def _dot_mask(mask_bf16, x):
    hi = x.astype(BF16)
    r1 = x - hi.astype(F32)
    mid = r1.astype(BF16)
    lo = (r1 - mid.astype(F32)).astype(BF16)
    return _dot(mask_bf16, hi) + (_dot(mask_bf16, mid) + _dot(mask_bf16, lo))


def _silu(x):
    return x * jax.nn.sigmoid(x)


def _gelu(x):
    return 0.5 * x * (1.0 + lax.erf(x * np.float32(math.sqrt(0.5))))


def _rope(x, cos, sin_signed, d):
    w = x.shape[-1]
    half = d // 2
    lane = lax.broadcasted_iota(jnp.int32, x.shape, 1)
    first = (lane % d) < half
    rot = jnp.where(first, pltpu.roll(x, w - half, 1), pltpu.roll(x, half, 1))
    return x * cos + rot * sin_signed


def _const_spec(shape):
    nd = len(shape)
    return pl.BlockSpec(shape, lambda *_: (0,) * nd, pipeline_mode=pl.Buffered(1))


def _layer_spec(a, layer):
    nd = a.ndim - 1
    return pl.BlockSpec((None,) + a.shape[1:], lambda *_: (layer,) + (0,) * nd, pipeline_mode=pl.Buffered(1))


def _params(sem):
    return pltpu.CompilerParams(dimension_semantics=sem, vmem_limit_bytes=VMEM_LIMIT)


def _ret_kernel(*refs, R, NS, RB, cdec, has_s0):
    if has_s0:
        (x_ref, w_ref, cos_ref, sin_ref, gn_ref, dmat_ref, qdec_ref, kdec_ref, s0_ref,
         y_ref, sfin_ref) = refs
    else:
        (x_ref, w_ref, cos_ref, sin_ref, gn_ref, dmat_ref, qdec_ref, kdec_ref,
         y_ref, sfin_ref) = refs
    C = R // NS
    h = _dot(x_ref[...].astype(BF16), w_ref[...])
    cos = cos_ref[...]
    sin = sin_ref[...]
    hq = RET_HEADS * RET_DK
    q = _rope(h[:, 0:hq], cos, sin, RET_DK)
    k = _rope(h[:, hq:2 * hq], cos, sin, RET_DK) * (RET_DK ** -0.5)
    v = h[:, 2 * hq:2 * hq + RET_W]
    g = _silu(h[:, 2 * hq + RET_W:2 * hq + 2 * RET_W])

    if has_s0:
        sfin_ref[...] = s0_ref[...]
    else:
        @pl.when(pl.program_id(1) == 0)
        def _():
            sfin_ref[...] = jnp.zeros(sfin_ref.shape, F32)

    probs = [(c, hh) for c in range(RB // R) for hh in range(RET_HEADS)]
    qh, kh, vh, sc, kv = {}, {}, {}, {}, {}
    for pr in probs:
        c, hh = pr
        rs = slice(c * R, (c + 1) * R)
        qh[pr] = q[rs, hh * RET_DK:(hh + 1) * RET_DK].astype(BF16)
        kh[pr] = k[rs, hh * RET_DK:(hh + 1) * RET_DK].astype(BF16)
        vh[pr] = v[rs, hh * RET_DV:(hh + 1) * RET_DV]
        sc[pr] = _dot_tb(qh[pr], kh[pr])
        vk = (vh[pr] * kdec_ref[hh]).astype(BF16)
        for s in range(NS):
            kv[pr, s] = _dot_ta(kh[pr][s * C:(s + 1) * C], vk[s * C:(s + 1) * C])
    sprev = {}
    for pr in probs:
        c, hh = pr
        for s in range(NS):
            sidx = c * NS + s if has_s0 else 0
            st = sfin_ref[sidx, hh]
            sprev[pr, s] = st.astype(BF16)
            sfin_ref[sidx, hh] = st * cdec[hh] + kv[pr, s]
    for pr in probs:
        c, hh = pr
        rs = slice(c * R, (c + 1) * R)
        vs = slice(hh * RET_DV, (hh + 1) * RET_DV)
        o = _dot((sc[pr] * dmat_ref[hh]).astype(BF16), vh[pr].astype(BF16))
        qd = qdec_ref[hh]
        parts = [_dot(qh[pr][s * C:(s + 1) * C], sprev[pr, s]) * qd[s * C:(s + 1) * C] for s in range(NS)]
        o = o + (parts[0] if NS == 1 else jnp.concatenate(parts, axis=0))
        mu = jnp.mean(o, axis=-1, keepdims=True)
        d = o - mu
        var = jnp.mean(d * d, axis=-1, keepdims=True)
        on = d * lax.rsqrt(var + EPS) * gn_ref[:, vs]
        y_ref[rs, vs] = (g[rs, vs] * on).astype(y_ref.dtype)


def _ret_consts(R, NS):
    C = R // NS
    lg = np.log1p(-np.exp2(-5.0 - np.arange(RET_HEADS, dtype=np.float64)))
    idx = np.arange(R)
    tok = idx % C
    same = (idx[:, None] // C) == (idx[None, :] // C)
    rel = tok[:, None] - tok[None, :]
    low = same & (rel >= 0)
    dmat = np.where(low[None], np.exp(lg[:, None, None] * np.where(low, rel, 0)[None]), 0.0)
    qdec = np.exp(lg[:, None] * (tok[None, :] + 1.0))
    kdec = np.exp(lg[:, None] * (C - 1.0 - tok[None, :]))
    cdec = tuple(float(v) for v in np.exp(lg * C))
    bc = lambda a: jnp.asarray(np.broadcast_to(a[:, :, None], (RET_HEADS, R, LANES)), F32)
    return jnp.asarray(dmat, F32), bc(qdec), bc(kdec), cdec


def _ret_call(x, w, cos, sin, gn, s0, *, B, T, layer, out_dtype):
    N = x.shape[0]
    has_s0 = s0 is not None
    R = LIN_CHUNK
    if has_s0:
        C = math.gcd(T, LIN_CHUNK)
        assert C == T, "sample stream is a single chunk per sequence"
        NS = R // C
        RB = 2 * R
        nb, nt = N // RB, 1
        nsb = RB // C
    else:
        NS, RB = 1, 512
        assert T % RB == 0 and math.gcd(T, LIN_CHUNK) == R
        nb, nt = B, T // RB
        nsb = 1
    dmat, qdec, kdec, cdec = _ret_consts(R, NS)
    tr = cos.shape[0] // RB
    hq = RET_HEADS * RET_DK
    in_specs = [
        pl.BlockSpec((RB, D_MODEL), lambda b, t: (b * nt + t, 0)),
        _layer_spec(w, layer),
        pl.BlockSpec((RB, hq), lambda b, t: ((b * nt + t) % tr, 0)),
        pl.BlockSpec((RB, hq), lambda b, t: ((b * nt + t) % tr, 0)),
        _const_spec(gn.shape), _const_spec(dmat.shape), _const_spec(qdec.shape), _const_spec(kdec.shape),
    ]
    args = [x, w, cos, sin, gn, dmat, qdec, kdec]
    st_spec = pl.BlockSpec((nsb, RET_HEADS, RET_DK, RET_DV), lambda b, t: (b, 0, 0, 0))
    if has_s0:
        in_specs.append(pl.BlockSpec((None, nsb, RET_HEADS, RET_DK, RET_DV), lambda b, t: (layer, b, 0, 0, 0)))
        args.append(s0)
    nseq = N // T
    return pl.pallas_call(
        functools.partial(_ret_kernel, R=R, NS=NS, RB=RB, cdec=cdec, has_s0=has_s0),
        grid=(nb, nt),
        in_specs=in_specs,
        out_specs=[pl.BlockSpec((RB, RET_W), lambda b, t: (b * nt + t, 0)), st_spec],
        out_shape=[jax.ShapeDtypeStruct((N, RET_W), out_dtype),
                   jax.ShapeDtypeStruct((nseq, RET_HEADS, RET_DK, RET_DV), F32)],
        compiler_params=_params(("arbitrary", "arbitrary")),
        name="ret",
    )(*args)


def _mlap_kernel(x_ref, w_ref, wuq_ref, qn_ref, kvn_ref, cos_ref, sin_ref, *outs, TM, emit_kt):
    if emit_kt:
        qnope_ref, qpe_ref, ckv_ref, kpe_ref, gm_ref, kt_ref, vb_ref = outs
    else:
        qnope_ref, qpe_ref, ckv_ref, kpe_ref, gm_ref = outs
    h = _dot(x_ref[...].astype(BF16), w_ref[...])
    cq = h[:, 0:Q_LORA]
    cq = cq * lax.rsqrt(jnp.mean(cq * cq, axis=-1, keepdims=True) + EPS) * qn_ref[...]
    qf = _dot(cq.astype(BF16), wuq_ref[...])
    nw = MLA_HEADS * MLA_NOPE
    qnope_ref[...] = (qf[:, 0:nw] * QSCALE).astype(qnope_ref.dtype)
    cos = cos_ref[...]
    sin = sin_ref[...]
    qpe_ref[...] = _rope(qf[:, nw:nw + MLA_HEADS * MLA_ROPE], cos, sin, MLA_ROPE) * QSCALE
    c0 = Q_LORA
    kv = h[:, c0:c0 + KV_LORA]
    ckv = kv * lax.rsqrt(jnp.mean(kv * kv, axis=-1, keepdims=True) + EPS) * kvn_ref[...]
    ckv_ref[...] = ckv
    g0 = c0 + KV_LORA
    gm_ref[...] = _silu(h[:, g0:g0 + MLA_W]).astype(gm_ref.dtype)
    p0 = g0 + MLA_W
    kblk = _rope(h[:, p0:p0 + LANES], cos[:, 0:LANES], sin[:, 0:LANES], MLA_ROPE)
    kpe_ref[...] = kblk[:, 0:MLA_ROPE]
    if emit_kt:
        rep = kblk
        for i in range(1, LANES // MLA_ROPE):
            rep = rep + pltpu.roll(kblk, i * MLA_ROPE, 1)
        kct = jnp.concatenate([ckv, rep], axis=1).T
        for i in range(TM // ATT_TK):
            kt_ref[i] = kct[:, i * ATT_TK:(i + 1) * ATT_TK].astype(BF16)
        vb_ref[...] = ckv.astype(BF16)


def _mlap_call(x, w, wuq, qn, kvn, cos, sin, *, layer, act_dtype, emit_kt):
    N = x.shape[0]
    TM = 512
    assert N % TM == 0 and TM % ATT_TK == 0
    tr = cos.shape[0] // TM
    row = lambda wd: pl.BlockSpec((TM, wd), lambda i: (i, 0))
    tab = pl.BlockSpec((TM, MLA_HEADS * MLA_ROPE), lambda i: (i % tr, 0))
    nw = MLA_HEADS * MLA_NOPE
    out_specs = [row(nw), row(MLA_HEADS * MLA_ROPE), row(KV_LORA), row(MLA_ROPE), row(MLA_W)]
    out_shape = [jax.ShapeDtypeStruct((N, nw), act_dtype),
                 jax.ShapeDtypeStruct((N, MLA_HEADS * MLA_ROPE), F32),
                 jax.ShapeDtypeStruct((N, KV_LORA), F32),
                 jax.ShapeDtypeStruct((N, MLA_ROPE), F32),
                 jax.ShapeDtypeStruct((N, MLA_W), act_dtype)]
    if emit_kt:
        out_specs += [pl.BlockSpec((TM // ATT_TK, KCAT_W, ATT_TK), lambda i: (i, 0, 0)), row(KV_LORA)]
        out_shape += [jax.ShapeDtypeStruct((N // ATT_TK, KCAT_W, ATT_TK), BF16),
                      jax.ShapeDtypeStruct((N, KV_LORA), BF16)]
    return pl.pallas_call(
        functools.partial(_mlap_kernel, TM=TM, emit_kt=emit_kt),
        grid=(N // TM,),
        in_specs=[row(D_MODEL), _layer_spec(w, layer), _layer_spec(wuq, layer), _const_spec(qn.shape),
                  _const_spec(kvn.shape), tab, tab],
        out_specs=out_specs,
        out_shape=out_shape,
        compiler_params=_params(("arbitrary",)),
        name="mlap",
    )(x, w, wuq, qn, kvn, cos, sin)


def _attn_prefill_kernel(qn_ref, qpe_ref, kt_ref, v_ref, wuk_ref, gm_ref, wuv_ref, y_ref,
                         q_s, m_s, l_s, acc_s, *, BQ, TK):
    i = pl.program_id(1)
    H = MLA_HEADS
    lane = lax.broadcasted_iota(jnp.int32, (BQ, LANES), 1)
    for hh in range(H):
        ql = _dot(qn_ref[:, hh * MLA_NOPE:(hh + 1) * MLA_NOPE], wuk_ref[hh])
        q_s[hh * BQ:(hh + 1) * BQ, 0:KV_LORA] = ql.astype(BF16)
        per = LANES // MLA_ROPE
        blk = qpe_ref[:, (hh // per) * LANES:(hh // per + 1) * LANES]
        off = (hh % per) * MLA_ROPE
        keep = (lane >= off) & (lane < off + MLA_ROPE)
        q_s[hh * BQ:(hh + 1) * BQ, KV_LORA:KCAT_W] = jnp.where(keep, blk, 0.0).astype(BF16)
    m_s[...] = jnp.full(m_s.shape, NEG, F32)
    l_s[...] = jnp.zeros(l_s.shape, F32)
    acc_s[...] = jnp.zeros(acc_s.shape, F32)

    def steps(blocks):
        kts = [kt_ref[j] for j, _ in blocks]
        vbs = [v_ref[pl.ds(pl.multiple_of(j * TK, TK), TK), :] for j, _ in blocks]
        row = lax.broadcasted_iota(jnp.int32, (BQ, TK), 0)
        col = lax.broadcasted_iota(jnp.int32, (BQ, TK), 1)
        items = [(bi, hh) for bi in range(len(blocks)) for hh in range(H)]
        scores = {}
        for t in range(len(items) + ATT_LAG):
            if t < len(items):
                bi, hh = items[t]
                scores[t] = _dot(q_s[hh * BQ:(hh + 1) * BQ, :], kts[bi])
            if t < ATT_LAG:
                continue
            bi, hh = items[t - ATT_LAG]
            rs = slice(hh * BQ, (hh + 1) * BQ)
            s = scores.pop(t - ATT_LAG)
            if blocks[bi][1]:
                s = jnp.where(blocks[bi][0] * TK + col <= i * BQ + row, s, NEG)
            m_prev = m_s[rs, :]
            m_new = jnp.maximum(m_prev, jnp.max(s, axis=-1, keepdims=True))
            alpha = jnp.exp2(m_prev - m_new)
            p = jnp.exp2(s - jnp.concatenate([m_new] * (TK // LANES), axis=1))
            l_s[rs, :] = alpha[:, 0:1] * l_s[rs, :] + jnp.sum(p, axis=-1, keepdims=True)
            acc_s[rs, :] = (jnp.concatenate([alpha] * (KV_LORA // LANES), axis=1) * acc_s[rs, :]
                            + _dot(p.astype(BF16), vbs[bi]))
            m_s[rs, :] = m_new

    last = (i * BQ + BQ - 1) // TK
    npair = last // ATT_UNROLL

    def body(jj, carry):
        steps([(jj * ATT_UNROLL + u, False) for u in range(ATT_UNROLL)])
        return carry

    lax.fori_loop(0, npair, body, 0)

    def tail(j, carry):
        steps([(j, False)])
        return carry

    lax.fori_loop(npair * ATT_UNROLL, last, tail, 0)
    steps([(last, True)])

    for p2 in range(H // 2):
        halves = []
        for hh in (2 * p2, 2 * p2 + 1):
            rs = slice(hh * BQ, (hh + 1) * BQ)
            halves.append((acc_s[rs, :] / l_s[rs, :]).astype(BF16))
        yb = _dot(jnp.concatenate(halves, axis=1), wuv_ref[p2])
        y_ref[:, p2 * LANES:(p2 + 1) * LANES] = (gm_ref[:, p2 * LANES:(p2 + 1) * LANES].astype(F32) * yb).astype(y_ref.dtype)


def _attn_prefill_call(qn, qpe, kt, vb, wuk, gm, wuv, *, B, T, layer):
    N = qn.shape[0]
    BQ, TK = ATT_BQ, ATT_TK
    assert T % TK == 0 and T % BQ == 0
    nq = T // BQ
    H = MLA_HEADS
    row = lambda wd: pl.BlockSpec((BQ, wd), lambda b, i: (b * nq + i, 0))
    return pl.pallas_call(
        functools.partial(_attn_prefill_kernel, BQ=BQ, TK=TK),
        grid=(B, nq),
        in_specs=[row(H * MLA_NOPE), row(H * MLA_ROPE),
                  pl.BlockSpec((T // TK, KCAT_W, TK), lambda b, i: (b, 0, 0)),
                  pl.BlockSpec((T, KV_LORA), lambda b, i: (b, 0)),
                  _layer_spec(wuk, layer), row(MLA_W), _layer_spec(wuv, layer)],
        out_specs=row(MLA_W),
        out_shape=jax.ShapeDtypeStruct((N, MLA_W), BF16),
        scratch_shapes=[pltpu.VMEM((H * BQ, KCAT_W), BF16), pltpu.VMEM((H * BQ, LANES), F32),
                        pltpu.VMEM((H * BQ, 1), F32), pltpu.VMEM((H * BQ, KV_LORA), F32)],
        compiler_params=_params(("arbitrary", "arbitrary")),
        name="attn_prefill",
    )(qn, qpe, kt, vb, wuk, gm, wuv)


def _attn_decode_kernel(pt_ref, qn_ref, qpe_ref, ckvn_ref, kpen_ref, gm_ref, wuk_ref, wuv_ref,
                        cckv_ref, ckpe_ref, y_ref,
                        kbuf, pbuf, sem, *, T, G, NSTEP, NSUB, LOOK, layer, PAGE):
    b = pl.program_id(0)
    nb = pl.num_programs(0)
    H = MLA_HEADS

    def copies(bb, jj):
        out = []
        for g in range(G):
            page = pt_ref[bb, jj * G + g]
            out.append(pltpu.make_async_copy(cckv_ref.at[layer, page], kbuf.at[jj, g], sem.at[0, jj]))
            out.append(pltpu.make_async_copy(ckpe_ref.at[layer, page], pbuf.at[jj, g], sem.at[1, jj]))
        return out

    def start_all(cs):
        for i, c in enumerate(cs):
            c.start(priority=(i // 2) % 2)

    @pl.when(b == 0)
    def _():
        for jj in range(LOOK):
            start_all(copies(b, jj))

    ql = jnp.concatenate([_dot(qn_ref[:, hh * MLA_NOPE:(hh + 1) * MLA_NOPE].astype(BF16), wuk_ref[hh])
                          for hh in range(H)], axis=0).astype(BF16)
    qp = jnp.concatenate([qpe_ref[:, hh * MLA_ROPE:(hh + 1) * MLA_ROPE] for hh in range(H)], axis=0).astype(BF16)
    kn = ckvn_ref[...].astype(BF16)
    s = _dot_tb(ql, kn) + _dot_tb(qp, kpen_ref[...].astype(BF16))
    row = lax.broadcasted_iota(jnp.int32, s.shape, 0)
    col = lax.broadcasted_iota(jnp.int32, s.shape, 1)
    s = jnp.where(col <= row % T, s, NEG)
    m = jnp.max(s, axis=-1, keepdims=True)
    p = jnp.exp2(s - m)
    l = jnp.sum(p, axis=-1, keepdims=True)
    acc = _dot(p.astype(BF16), kn)

    GS = G // NSUB
    for j in range(NSTEP):
        jn = j + LOOK
        if jn < NSTEP:
            start_all(copies(b, jn))
        else:
            @pl.when(b + 1 < nb)
            def _():
                start_all(copies(b + 1, jn - NSTEP))
        for c in copies(b, j):
            c.wait()
        kbs, scores = [], []
        for u in range(NSUB):
            kb = kbuf[j, u * GS:(u + 1) * GS].reshape(GS * PAGE, KV_LORA).astype(BF16)
            pcat = jnp.concatenate([pbuf[j, u * GS + g] for g in range(GS)], axis=1).astype(BF16)
            kbs.append(kb)
            scores.append(_dot_tb(ql, kb) + _dot(qp, pcat))
        for u in range(NSUB):
            kb, s = kbs[u], scores[u]
            m_new = jnp.maximum(m, jnp.max(s, axis=-1, keepdims=True))
            alpha = jnp.exp2(m - m_new)
            p = jnp.exp2(s - m_new)
            l = alpha * l + jnp.sum(p, axis=-1, keepdims=True)
            acc = alpha * acc + _dot(p.astype(BF16), kb)
            m = m_new

    o = acc / l
    for p2 in range(H // 2):
        lhs = jnp.concatenate([o[(2 * p2) * T:(2 * p2 + 1) * T], o[(2 * p2 + 1) * T:(2 * p2 + 2) * T]], axis=1)
        yb = _dot(lhs.astype(BF16), wuv_ref[p2])
        y_ref[:, p2 * LANES:(p2 + 1) * LANES] = gm_ref[:, p2 * LANES:(p2 + 1) * LANES] * yb


def _attn_decode_call(page_table, qn, qpe, ckv, kpe, gm, wuk, wuv, cache_ckv, cache_kpet, *, B, T, layer):
    N = qn.shape[0]
    n_pages = page_table.shape[1]
    PAGE = cache_ckv.shape[2]
    G = math.gcd(n_pages, DEC_GROUP)
    if n_pages // G < 2 and G % 2 == 0:
        G //= 2
    NSTEP = n_pages // G
    assert NSTEP >= 2, "the page groups of one sample are double buffered"
    page_bytes = PAGE * (KV_LORA + MLA_ROPE) * 4
    assert n_pages * page_bytes <= VMEM_LIMIT // 2, "one sample's pages must fit the VMEM buffers"
    LOOK = min(DEC_LOOK, NSTEP - 1)
    NSUB = math.gcd(G, 4)
    H = MLA_HEADS
    row = lambda wd: pl.BlockSpec((T, wd), lambda b, pt: (b, 0))
    grid_spec = pltpu.PrefetchScalarGridSpec(
        num_scalar_prefetch=1,
        grid=(B,),
        in_specs=[row(H * MLA_NOPE), row(H * MLA_ROPE), row(KV_LORA), row(MLA_ROPE), row(MLA_W),
                  _layer_spec(wuk, layer), _layer_spec(wuv, layer),
                  pl.BlockSpec(memory_space=pl.ANY), pl.BlockSpec(memory_space=pl.ANY)],
        out_specs=row(MLA_W),
        scratch_shapes=[pltpu.VMEM((NSTEP, G, PAGE, KV_LORA), F32), pltpu.VMEM((NSTEP, G, MLA_ROPE, PAGE), F32),
                        pltpu.SemaphoreType.DMA((2, NSTEP))],
    )
    return pl.pallas_call(
        functools.partial(_attn_decode_kernel, T=T, G=G, NSTEP=NSTEP, NSUB=NSUB, LOOK=LOOK, layer=layer, PAGE=PAGE),
        grid_spec=grid_spec,
        out_shape=jax.ShapeDtypeStruct((N, MLA_W), F32),
        compiler_params=_params(("arbitrary",)),
        name="attn_decode",
    )(page_table, qn, qpe, ckv, kpe, gm, wuk, wuv, cache_ckv, cache_kpet)


def _sg_kernel(x_ref, w_ref, lng_ref, lnb_ref, wt_ref, bs_ref, y_ref, *rest, TM, emit_v):
    h = _dot(x_ref[...].astype(BF16), w_ref[...])
    u = _gelu(h[:, 0:SG_W])
    gv = _gelu(h[:, SG_W:2 * SG_W])
    mu = jnp.mean(gv, axis=-1, keepdims=True)
    d = gv - mu
    var = jnp.mean(d * d, axis=-1, keepdims=True)
    v = d * lax.rsqrt(var + EPS) * lng_ref[...] + lnb_ref[...]
    if emit_v:
        rest[0][...] = v
    gs = _silu(h[:, 2 * SG_W:3 * SG_W])
    vb = v.astype(BF16)
    for c in range(TM // SG_CHUNK):
        rows = slice(c * SG_CHUNK, (c + 1) * SG_CHUNK)
        for g in range(SG_GROUPS):
            cols = slice(g * SG_GDIM, (g + 1) * SG_GDIM)
            s = _dot(wt_ref[g], vb[rows, cols]) + bs_ref[g]
            y_ref[rows, cols] = (gs[rows, cols] * (u[rows, cols] * s)).astype(y_ref.dtype)


def _sg_call(x, w, lng, lnb, wt, bs, *, layer, emit_v, out_dtype):
    N = x.shape[0]
    TM = 512
    assert N % TM == 0
    row = lambda wd: pl.BlockSpec((TM, wd), lambda i: (i, 0))
    out_specs = [row(SG_W)]
    out_shape = [jax.ShapeDtypeStruct((N, SG_W), out_dtype)]
    if emit_v:
        out_specs.append(row(SG_W))
        out_shape.append(jax.ShapeDtypeStruct((N, SG_W), F32))
    return pl.pallas_call(
        functools.partial(_sg_kernel, TM=TM, emit_v=emit_v),
        grid=(N // TM,),
        in_specs=[row(D_MODEL), _layer_spec(w, layer), _const_spec(lng.shape), _const_spec(lnb.shape),
                  _layer_spec(wt, layer), _layer_spec(bs, layer)],
        out_specs=out_specs,
        out_shape=out_shape,
        compiler_params=_params(("arbitrary",)),
        name="sg",
    )(x, w, lng, lnb, wt, bs)


def _gd_kernel(*refs, R, NS, RB, has_s0):
    if has_s0:
        (x_ref, w_ref, cw_ref, av_ref, dtb_ref, ng_ref, msk_ref, bufp_ref, s0_ref,
         y_ref, sfin_ref, conv_ref) = refs
    else:
        (x_ref, w_ref, cw_ref, av_ref, dtb_ref, ng_ref, msk_ref,
         y_ref, sfin_ref, conv_ref, carry_s) = refs
    C = R // NS
    NLEV = _gd_levels(C)
    CH = GD_CONV_CH
    h = _dot(x_ref[...].astype(BF16), w_ref[...])
    xq = h[:, 0:CH]
    y = xq * cw_ref[GD_CONV - 1:GD_CONV, :]
    if has_s0:
        x3 = xq.reshape(RB // 8, 8, CH)
        b3 = bufp_ref[...].reshape(RB // 8, 8, CH)
        tpos = lax.broadcasted_iota(jnp.int32, x3.shape, 1)
        for k in range(1, GD_CONV):
            sh = jnp.where(tpos >= k, pltpu.roll(x3, k, 1), pltpu.roll(b3, k, 1))
            y = y + sh.reshape(RB, CH) * cw_ref[GD_CONV - 1 - k:GD_CONV - k, :]
        for s in range(RB // C):
            conv_ref[s] = xq[s * C + C - (GD_CONV - 1):s * C + C, :]
    else:
        @pl.when(pl.program_id(1) == 0)
        def _():
            carry_s[0:8, :] = jnp.zeros((8, CH), F32)
        carry_s[8:8 + RB, :] = xq
        for k in range(1, GD_CONV):
            y = y + carry_s[8 - k:8 - k + RB, :] * cw_ref[GD_CONV - 1 - k:GD_CONV - k, :]
        carry_s[0:8, :] = xq[RB - 8:RB]
        conv_ref[0] = xq[RB - (GD_CONV - 1):RB, :]
    qkv = _silu(y)
    hw = GD_HEADS * GD_DK
    qn, kn = [], []
    for hh in range(GD_HEADS):
        qb = qkv[:, hh * GD_DK:(hh + 1) * GD_DK]
        qn.append(qb * lax.rsqrt(jnp.sum(qb * qb, axis=-1, keepdims=True) + EPS) * (GD_DK ** -0.5))
        kb = qkv[:, hw + hh * GD_DK:hw + (hh + 1) * GD_DK]
        kn.append(kb * lax.rsqrt(jnp.sum(kb * kb, axis=-1, keepdims=True) + EPS))
    vv = qkv[:, 2 * hw:2 * hw + GD_W]
    gate = _silu(h[:, CH:CH + GD_W])
    ab = h[:, CH + GD_W:CH + GD_W + LANES]
    lane = lax.broadcasted_iota(jnp.int32, ab.shape, 1)
    gb = jnp.where(lane < GD_HEADS, av_ref[...] * jax.nn.softplus(ab + dtb_ref[...]), jax.nn.sigmoid(ab))

    if has_s0:
        sfin_ref[...] = s0_ref[...]
    else:
        @pl.when(pl.program_id(1) == 0)
        def _():
            sfin_ref[...] = jnp.zeros(sfin_ref.shape, F32)

    lowf = msk_ref[0]
    low = lowf > 0.0
    strictf = msk_ref[1]
    lowb = lowf.astype(BF16)
    sameb = msk_ref[2].astype(BF16)
    ri = lax.broadcasted_iota(jnp.int32, (R, R), 0)
    ci = lax.broadcasted_iota(jnp.int32, (R, R), 1)
    eye = (ri == ci).astype(F32)

    nch = RB // R
    probs = [(c, hh) for c in range(nch) for hh in range(GD_HEADS)]
    gbc = [gb[c * R:(c + 1) * R] for c in range(nch)]
    gc4 = [_dot_mask(lowb, g) for g in gbc]
    gl4 = [_dot_mask(sameb, g) for g in gbc]
    qh, kh, qkk = {}, {}, {}
    for pr in probs:
        c, hh = pr
        rs = slice(c * R, (c + 1) * R)
        qh[pr], kh[pr] = qn[hh][rs], kn[hh][rs]
        khb = kh[pr].astype(BF16)
        qkk[pr] = _dot_tb(jnp.concatenate([qh[pr].astype(BF16), khb], axis=0), khb)
    gcb, glb, beta, gam, qk, amat, mp, pinv = {}, {}, {}, {}, {}, {}, {}, {}
    for pr in probs:
        c, hh = pr
        gcb[pr] = jnp.broadcast_to(gc4[c][:, hh:hh + 1], (R, LANES))
        glb[pr] = jnp.broadcast_to(gl4[c][:, hh:hh + 1], (R, LANES))
        beta[pr] = jnp.broadcast_to(gbc[c][:, GD_HEADS + hh:GD_HEADS + hh + 1], (R, LANES))
        gi = gcb[pr][:, 0:R]
        gam[pr] = jnp.where(low, jnp.exp(jnp.where(low, gi - gi.T, 0.0)), 0.0)
        qk[pr] = (qkk[pr][0:R] * gam[pr]).astype(BF16)
        amat[pr] = strictf * (beta[pr][:, 0:R] * qkk[pr][R:2 * R] * gam[pr])
        mp[pr] = -(msk_ref[3] * amat[pr])
        pinv[pr] = eye + mp[pr]
    for _ in range(int(math.log2(min(C, GD_BASE))) - 1):
        for pr in probs:
            mpb = mp[pr].astype(BF16)
            mp[pr] = _dot(mpb, mpb)
        for pr in probs:
            pinv[pr] = pinv[pr] + _dot(pinv[pr].astype(BF16), mp[pr].astype(BF16))
    for lev in range(NLEV):
        wl = {pr: _dot((msk_ref[4 + lev] * amat[pr]).astype(BF16), pinv[pr].astype(BF16)) for pr in probs}
        for pr in probs:
            pinv[pr] = pinv[pr] - _dot(pinv[pr].astype(BF16), wl[pr].astype(BF16))
    pre = {}
    for pr in probs:
        c, hh = pr
        vh = vv[c * R:(c + 1) * R, hh * GD_DK:(hh + 1) * GD_DK]
        eg = jnp.exp(gcb[pr])
        rhs = jnp.concatenate([vh * beta[pr], kh[pr] * (beta[pr] * eg)], axis=1)
        sol = _dot(pinv[pr].astype(BF16), rhs.astype(BF16))
        pre[pr] = (sol[:, 0:GD_DV], sol[:, GD_DV:GD_DV + GD_DK].astype(BF16),
                   (qh[pr] * eg).astype(BF16), (kh[pr] * jnp.exp(glb[pr] - gcb[pr])).astype(BF16),
                   jnp.exp(glb[pr]))

    for c in range(nch):
        rs = slice(c * R, (c + 1) * R)
        seqs = [(hh, s) for hh in range(GD_HEADS) for s in range(NS)]
        st, sb, ws, os_ = {}, {}, {}, {}
        for hh, s in seqs:
            sidx = c * NS + s if has_s0 else 0
            st[hh, s] = sfin_ref[sidx, hh]
            sb[hh, s] = st[hh, s].astype(BF16)
        for hh, s in seqs:
            rows = slice(s * C, (s + 1) * C)
            u, wb, qd, kd, el = pre[c, hh]
            ws[hh, s] = _dot(wb[rows], sb[hh, s])
            os_[hh, s] = _dot(qd[rows], sb[hh, s])
        vnew = {}
        for hh in range(GD_HEADS):
            u = pre[c, hh][0]
            vns = [u[s * C:(s + 1) * C] - ws[hh, s] for s in range(NS)]
            vnew[hh] = vns[0] if NS == 1 else jnp.concatenate(vns, axis=0)
        for hh, s in seqs:
            rows = slice(s * C, (s + 1) * C)
            kd, el = pre[c, hh][3], pre[c, hh][4]
            sidx = c * NS + s if has_s0 else 0
            sfin_ref[sidx, hh] = (st[hh, s] * el[s * C:s * C + 1, :]
                                  + _dot_ta(kd[rows], vnew[hh][rows].astype(BF16)))
        for hh in range(GD_HEADS):
            sl = slice(hh * GD_DK, (hh + 1) * GD_DK)
            oi = [os_[hh, s] for s in range(NS)]
            o = (oi[0] if NS == 1 else jnp.concatenate(oi, axis=0)) + _dot(qk[c, hh], vnew[hh].astype(BF16))
            on = o * lax.rsqrt(jnp.mean(o * o, axis=-1, keepdims=True) + EPS) * ng_ref[...]
            y_ref[rs, sl] = (gate[rs, sl] * on).astype(y_ref.dtype)


def _gd_levels(C):
    base = min(C, GD_BASE)
    assert C % base == 0 and (C // base) & (C // base - 1) == 0 and base & (base - 1) == 0
    return int(math.log2(C // base))


def _gd_masks(R, NS):
    C = R // NS
    idx = np.arange(R)
    blk = lambda s: (idx[:, None] // s) == (idx[None, :] // s)
    same = blk(C)
    low = same & (idx[:, None] >= idx[None, :])
    strict = same & (idx[:, None] > idx[None, :])
    base = min(C, GD_BASE)
    masks = [low, strict, same, blk(base)]
    for lev in range(_gd_levels(C)):
        s = base * 2 ** (lev + 1)
        masks.append(blk(s) & ~blk(s // 2))
    return jnp.asarray(np.stack(masks).astype(np.float32))


def _gd_call(x, w, cw, av, dtb, ng, bufp, s0, *, B, T, layer, out_dtype):
    N = x.shape[0]
    has_s0 = s0 is not None
    R = LIN_CHUNK
    CH = GD_CONV_CH
    if has_s0:
        C = math.gcd(T, LIN_CHUNK)
        assert C == T and T == 8 and T >= GD_CONV - 1
        NS = R // C
        RB = 2 * R
        nb, nt = N // RB, 1
        nsb = RB // C
    else:
        NS, RB = 1, GD_RB
        assert T % RB == 0 and math.gcd(T, LIN_CHUNK) == R
        nb, nt = B, T // RB
        nsb = 1
    msk = _gd_masks(R, NS)
    rowspec = lambda wd: pl.BlockSpec((RB, wd), lambda b, t: (b * nt + t, 0))
    in_specs = [rowspec(D_MODEL), _layer_spec(w, layer), _layer_spec(cw, layer), _const_spec(av.shape),
                _const_spec(dtb.shape), _const_spec(ng.shape), _const_spec(msk.shape)]
    args = [x, w, cw, av, dtb, ng, msk]
    st_spec = pl.BlockSpec((nsb, GD_HEADS, GD_DK, GD_DV), lambda b, t: (b, 0, 0, 0))
    scratch = []
    if has_s0:
        in_specs += [pl.BlockSpec((None, RB, CH), lambda b, t: (layer, b * nt + t, 0)),
                     pl.BlockSpec((None, nsb, GD_HEADS, GD_DK, GD_DV), lambda b, t: (layer, b, 0, 0, 0))]
        args += [bufp, s0]
    else:
        scratch.append(pltpu.VMEM((8 + RB, CH), F32))
    nseq = N // T
    return pl.pallas_call(
        functools.partial(_gd_kernel, R=R, NS=NS, RB=RB, has_s0=has_s0),
        grid=(nb, nt),
        in_specs=in_specs,
        out_specs=[rowspec(GD_W), st_spec,
                   pl.BlockSpec((nsb, GD_CONV - 1, CH), lambda b, t: (b, 0, 0))],
        out_shape=[jax.ShapeDtypeStruct((N, GD_W), out_dtype),
                   jax.ShapeDtypeStruct((nseq, GD_HEADS, GD_DK, GD_DV), F32),
                   jax.ShapeDtypeStruct((nseq, GD_CONV - 1, CH), F32)],
        scratch_shapes=scratch,
        compiler_params=_params(("arbitrary", "arbitrary")),
        name="gd",
    )(*args)


def _back_kernel(x_ref, ya_ref, yb_ref, yc_ref, yd_ref, wm_ref, wb_ref, wo_ref, lg_ref, lb_ref, o_ref, *, alpha):
    x = x_ref[...]
    gates = jax.nn.sigmoid(_dot(x.astype(BF16), wm_ref[...]))
    merged = None
    for i, yr in enumerate((ya_ref, yb_ref, yc_ref, yd_ref)):
        t = gates[:, i * D_MODEL:(i + 1) * D_MODEL] * _dot(yr[...].astype(BF16), wb_ref[i])
        merged = t if merged is None else merged + t
    z = alpha * x + _dot(merged.astype(BF16), wo_ref[...])
    mu = jnp.mean(z, axis=-1, keepdims=True)
    d = z - mu
    var = jnp.mean(d * d, axis=-1, keepdims=True)
    o_ref[...] = d * lax.rsqrt(var + EPS) * lg_ref[...] + lb_ref[...]


def _back_call(x, ya, yb, yc, yd, wm, wb, wo, lg, lb, *, layer, alpha):
    N = x.shape[0]
    TM = 256
    assert N % TM == 0
    row = lambda wd: pl.BlockSpec((TM, wd), lambda i: (i, 0))
    return pl.pallas_call(
        functools.partial(_back_kernel, alpha=alpha),
        grid=(N // TM,),
        in_specs=[row(D_MODEL), row(BRANCH_W), row(BRANCH_W), row(BRANCH_W), row(BRANCH_W),
                  _layer_spec(wm, layer), _layer_spec(wb, layer), _layer_spec(wo, layer),
                  _const_spec(lg.shape), _const_spec(lb.shape)],
        out_specs=row(D_MODEL),
        out_shape=jax.ShapeDtypeStruct((N, D_MODEL), F32),
        compiler_params=_params(("arbitrary",)),
        name="back",
    )(x, ya, yb, yc, yd, wm, wb, wo, lg, lb)


def _rope_tables(pos, d, reps, rows):
    inv = ROPE_BASE ** (-jnp.arange(0, d, 2, dtype=F32) / d)
    ang = pos.astype(F32)[:, None] * inv[None, :]
    c, s = jnp.cos(ang), jnp.sin(ang)
    cos = jnp.tile(jnp.concatenate([c, c], axis=-1), (rows // pos.shape[0], reps))
    sin = jnp.tile(jnp.concatenate([-s, s], axis=-1), (rows // pos.shape[0], reps))
    return cos, sin


def _prep_weights(w_in, mla_w_uq, mla_w_uk, mla_w_uv, w_branch, w_out):
    o = _OFF
    cols = lambda a, b: w_in[:, :, o[a]:o[b]]
    depth = w_in.shape[0]
    zpad = lambda n: jnp.zeros((depth, D_MODEL, n), w_in.dtype)
    w_ret = cols(0, 4).astype(BF16)
    w_mla = jnp.concatenate([cols(4, 6), cols(7, 8), cols(6, 7), zpad(LANES - MLA_ROPE)], axis=-1).astype(BF16)
    w_sg = cols(8, 11).astype(BF16)
    w_gd = jnp.concatenate([cols(11, 12), cols(14, 15), cols(12, 14), zpad(LANES - 2 * GD_HEADS)], axis=-1).astype(BF16)
    w_merge = cols(15, 16).astype(BF16)
    uq = mla_w_uq.reshape(depth, Q_LORA, MLA_HEADS, MLA_NOPE + MLA_ROPE)
    w_uq = jnp.concatenate([uq[..., :MLA_NOPE].reshape(depth, Q_LORA, -1),
                            uq[..., MLA_NOPE:].reshape(depth, Q_LORA, -1)], axis=-1).astype(BF16)
    w_uk = jnp.transpose(mla_w_uk, (0, 2, 3, 1)).astype(BF16)
    uv = mla_w_uv.reshape(depth, KV_LORA, MLA_HEADS // 2, 2, MLA_V)
    z = jnp.zeros((depth, KV_LORA, MLA_HEADS // 2, MLA_V), mla_w_uv.dtype)
    top = jnp.concatenate([uv[:, :, :, 0], z], axis=-1)
    bot = jnp.concatenate([z, uv[:, :, :, 1]], axis=-1)
    w_uv = jnp.transpose(jnp.concatenate([top, bot], axis=1), (0, 2, 1, 3)).astype(BF16)
    return dict(w_ret=w_ret, w_mla=w_mla, w_sg=w_sg, w_gd=w_gd, w_merge=w_merge, w_uq=w_uq, w_uk=w_uk, w_uv=w_uv,
                w_branch=w_branch.astype(BF16), w_out=w_out.astype(BF16))


def _sg_mix_weights(sg_w, sg_b, C):
    reps = SG_CHUNK // C
    w = jnp.tril(sg_w[:, :, :C, :C])
    if reps > 1:
        eye = jnp.eye(reps, dtype=w.dtype)
        w = jnp.einsum('ab,dgij->dgaibj', eye, w).reshape(w.shape[0], SG_GROUPS, SG_CHUNK, SG_CHUNK)
    b = jnp.tile(sg_b[:, :, :C], (1, 1, reps))
    bs = jnp.broadcast_to(b[..., None], b.shape + (SG_GDIM,))
    return w.astype(BF16), bs.astype(F32)


def _layer_stream(x, l, W, P, *, B, T, tabs, sample):
    cosr, sinr, cosm, sinm = tabs
    act = F32 if sample is not None else BF16
    ya, s_ret = _ret_call(x, W['w_ret'], cosr, sinr, P['ret_gn_g'][l][None],
                          sample['state_ret'] if sample else None, B=B, T=T, layer=l, out_dtype=act)
    mo = _mlap_call(x, W['w_mla'], W['w_uq'], P['mla_q_norm'][l][None],
                    P['mla_kv_norm'][l][None], cosm, sinm, layer=l, act_dtype=act, emit_kt=sample is None)
    qn, qpe, ckv, kpe, gm = mo[:5]
    if sample:
        yb = _attn_decode_call(sample['page_table'], qn, qpe, ckv, kpe, gm, W['w_uk'], W['w_uv'],
                               sample['cache_ckv'], sample['cache_kpet'], B=B, T=T, layer=l)
    else:
        yb = _attn_prefill_call(qn, qpe, mo[5], mo[6], W['w_uk'], gm, W['w_uv'], B=B, T=T, layer=l)
    wt, bs = (P['sg_wt_s'], P['sg_bs_s']) if sample else (P['sg_wt_p'], P['sg_bs_p'])
    sg_out = _sg_call(x, W['w_sg'], P['sg_ln_g'][l][None], P['sg_ln_b'][l][None], wt, bs,
                      layer=l, emit_v=sample is not None, out_dtype=act)
    yc = sg_out[0]
    sgv = sg_out[1] if sample else None
    yd, s_gd, conv_new = _gd_call(x, W['w_gd'], P['gd_conv_w'], P['gd_av'][l], P['gd_dtb'][l],
                                  P['gd_norm_g'][l][None],
                                  sample['bufp'] if sample else None,
                                  sample['state_delta'] if sample else None, B=B, T=T, layer=l, out_dtype=act)
    xn = _back_call(x, ya, yb, yc, yd, W['w_merge'], W['w_branch'], W['w_out'],
                    P['ln_g'][l][None], P['ln_b'][l][None], layer=l, alpha=P['alpha'])
    return xn, ckv, kpe, s_ret, s_gd, conv_new, sgv


def kernel(x_prompt, x_sample, cache_ckv, cache_kpe, state_ret, state_delta, state_conv, page_table,
           w_in, ret_gn_g, mla_q_norm, mla_w_uq, mla_kv_norm, mla_w_uk, mla_w_uv,
           sg_ln_g, sg_ln_b, sg_w, sg_b, gd_conv_w, gd_a_log, gd_dt_bias, gd_norm_g,
           w_branch, w_out, ln_g, ln_b):
    depth = w_in.shape[0]
    bp, tp, _ = x_prompt.shape
    bs_, ts, _ = x_sample.shape
    past_len = page_table.shape[1] * cache_ckv.shape[2]
    W = _prep_weights(w_in, mla_w_uq, mla_w_uk, mla_w_uv, w_branch, w_out)
    pad_small = lambda v: jnp.pad(v, ((0, 0), (0, LANES - v.shape[-1])))[:, None, :]
    P = dict(ret_gn_g=ret_gn_g, mla_q_norm=mla_q_norm, mla_kv_norm=mla_kv_norm, sg_ln_g=sg_ln_g, sg_ln_b=sg_ln_b,
             gd_conv_w=gd_conv_w, gd_norm_g=gd_norm_g, ln_g=ln_g, ln_b=ln_b,
             gd_av=pad_small(-jnp.exp(gd_a_log)), gd_dtb=pad_small(gd_dt_bias),
             alpha=float((2.0 * depth) ** 0.25))
    P['sg_wt_p'], P['sg_bs_p'] = _sg_mix_weights(sg_w, sg_b, min(tp, SG_CHUNK))
    P['sg_wt_s'], P['sg_bs_s'] = _sg_mix_weights(sg_w, sg_b, min(ts, SG_CHUNK))

    pos_p = jnp.arange(tp)
    pos_s = past_len + jnp.arange(ts)
    tab_rows_s = 512
    tabs_p = _rope_tables(pos_p, RET_DK, RET_HEADS, tp) + _rope_tables(pos_p, MLA_ROPE, MLA_HEADS, tp)
    tabs_s = _rope_tables(pos_s, RET_DK, RET_HEADS, tab_rows_s) + _rope_tables(pos_s, MLA_ROPE, MLA_HEADS, tab_rows_s)

    bufp = jnp.pad(state_conv, ((0, 0), (0, 0), (ts - (GD_CONV - 1), 0), (0, 0))).reshape(depth, bs_ * ts, GD_CONV_CH)
    sample = dict(state_ret=state_ret, state_delta=state_delta, bufp=bufp, page_table=page_table,
                  cache_ckv=cache_ckv, cache_kpet=jnp.swapaxes(cache_kpe, 2, 3))

    xp = x_prompt.reshape(bp * tp, D_MODEL)
    xs = x_sample.reshape(bs_ * ts, D_MODEL)
    outs_p, outs_s = [], []
    for l in range(depth):
        xp, *rest = _layer_stream(xp, l, W, P, B=bp, T=tp, tabs=tabs_p, sample=None)
        outs_p.append(rest)
        xs, *rest = _layer_stream(xs, l, W, P, B=bs_, T=ts, tabs=tabs_s, sample=sample)
        outs_s.append(rest)

    def stack(outs, k, shape):
        return jnp.stack([o[k].reshape(shape) for o in outs])

    return (xp.reshape(bp, tp, D_MODEL), xs.reshape(bs_, ts, D_MODEL),
            stack(outs_p, 0, (bp, tp, KV_LORA)), stack(outs_p, 1, (bp, tp, MLA_ROPE)),
            stack(outs_p, 2, (bp, RET_HEADS, RET_DK, RET_DV)), stack(outs_p, 3, (bp, GD_HEADS, GD_DK, GD_DV)),
            stack(outs_p, 4, (bp, GD_CONV - 1, GD_CONV_CH)),
            stack(outs_s, 0, (bs_, ts, KV_LORA)), stack(outs_s, 1, (bs_, ts, MLA_ROPE)),
            stack(outs_s, 2, (bs_, RET_HEADS, RET_DK, RET_DV)), stack(outs_s, 3, (bs_, GD_HEADS, GD_DK, GD_DV)),
            stack(outs_s, 4, (bs_, GD_CONV - 1, GD_CONV_CH)), stack(outs_s, 5, (bs_, ts, SG_W)))
```

```python
import functools
import math

import numpy as np
import jax
import jax.numpy as jnp
from jax import lax
from jax.experimental import pallas as pl
from jax.experimental.pallas import tpu as pltpu

F32 = jnp.float32
BF16 = jnp.bfloat16

D_MODEL = 1024
RET_HEADS, RET_DK, RET_DV = 4, 64, 128
RET_W = RET_HEADS * RET_DV
MLA_HEADS, MLA_NOPE, MLA_ROPE, MLA_V = 8, 64, 32, 64
Q_LORA, KV_LORA = 384, 256
MLA_W = MLA_HEADS * MLA_V
MLA_SCALE = (MLA_NOPE + MLA_ROPE) ** -0.5
SG_GROUPS, SG_GDIM, SG_CHUNK = 4, 128, 128
SG_W = SG_GROUPS * SG_GDIM
GD_HEADS, GD_DK, GD_DV, GD_CONV = 4, 128, 128, 4
GD_W = GD_HEADS * GD_DV
GD_CONV_CH = 2 * GD_HEADS * GD_DK + GD_W
N_BRANCH, BRANCH_W = 4, 512
LIN_CHUNK = 64
ROPE_BASE = 10000.0
EPS = 1e-6
NEG = -1e30
IN_SPLITS = (RET_HEADS * RET_DK, RET_HEADS * RET_DK, RET_W, RET_W,
             Q_LORA, KV_LORA, MLA_ROPE, MLA_W,
             SG_W, SG_W, SG_W,
             GD_CONV_CH, GD_HEADS, GD_HEADS, GD_W,
             N_BRANCH * D_MODEL)
_OFF = np.concatenate([[0], np.cumsum(IN_SPLITS)]).tolist()

LANES = 128
KCAT_W = KV_LORA + LANES
ATT_TK = 256
ATT_BQ = 256
ATT_LAG = 6
ATT_UNROLL = 3
GD_RB = 512
DEC_GROUP = 32
DEC_LOOK = 3
GD_BASE = 8
QSCALE = MLA_SCALE * math.log2(math.e)
VMEM_LIMIT = 56 * 1024 * 1024

_TB = (((1,), (1,)), ((), ()))
_TA = (((0,), (0,)), ((), ()))


def _dot(a, b):
    return jnp.dot(a, b, preferred_element_type=F32)


def _dot_tb(a, b):
    return lax.dot_general(a, b, _TB, preferred_element_type=F32)


def _dot_ta(a, b):
    return lax.dot_general(a, b, _TA, preferred_element_type=F32)


def _dot_mask(mask_bf16, x):
    hi = x.astype(BF16)
    r1 = x - hi.astype(F32)
    mid = r1.astype(BF16)
    lo = (r1 - mid.astype(F32)).astype(BF16)
    return _dot(mask_bf16, hi) + (_dot(mask_bf16, mid) + _dot(mask_bf16, lo))


def _silu(x):
    return x * jax.nn.sigmoid(x)


def _gelu(x):
    return 0.5 * x * (1.0 + lax.erf(x * np.float32(math.sqrt(0.5))))


def _rope(x, cos, sin_signed, d):
    w = x.shape[-1]
    half = d // 2
    lane = lax.broadcasted_iota(jnp.int32, x.shape, 1)
    first = (lane % d) < half
    rot = jnp.where(first, pltpu.roll(x, w - half, 1), pltpu.roll(x, half, 1))
    return x * cos + rot * sin_signed


def _const_spec(shape):
    nd = len(shape)
    return pl.BlockSpec(shape, lambda *_: (0,) * nd, pipeline_mode=pl.Buffered(1))


def _layer_spec(a, layer):
    nd = a.ndim - 1
    return pl.BlockSpec((None,) + a.shape[1:], lambda *_: (layer,) + (0,) * nd, pipeline_mode=pl.Buffered(1))


def _params(sem):
    return pltpu.CompilerParams(dimension_semantics=sem, vmem_limit_bytes=VMEM_LIMIT)


def _ret_kernel(*refs, R, NS, RB, cdec, has_s0):
    if has_s0:
        (x_ref, w_ref, cos_ref, sin_ref, gn_ref, dmat_ref, qdec_ref, kdec_ref, s0_ref,
         y_ref, sfin_ref) = refs
    else:
        (x_ref, w_ref, cos_ref, sin_ref, gn_ref, dmat_ref, qdec_ref, kdec_ref,
         y_ref, sfin_ref) = refs
    C = R // NS
    h = _dot(x_ref[...].astype(BF16), w_ref[...])
    cos = cos_ref[...]
    sin = sin_ref[...]
    hq = RET_HEADS * RET_DK
    q = _rope(h[:, 0:hq], cos, sin, RET_DK)
    k = _rope(h[:, hq:2 * hq], cos, sin, RET_DK) * (RET_DK ** -0.5)
    v = h[:, 2 * hq:2 * hq + RET_W]
    g = _silu(h[:, 2 * hq + RET_W:2 * hq + 2 * RET_W])

    if has_s0:
        sfin_ref[...] = s0_ref[...]
    else:
        @pl.when(pl.program_id(1) == 0)
        def _():
            sfin_ref[...] = jnp.zeros(sfin_ref.shape, F32)

    probs = [(c, hh) for c in range(RB // R) for hh in range(RET_HEADS)]
    qh, kh, vh, sc, kv = {}, {}, {}, {}, {}
    for pr in probs:
        c, hh = pr
        rs = slice(c * R, (c + 1) * R)
        qh[pr] = q[rs, hh * RET_DK:(hh + 1) * RET_DK].astype(BF16)
        kh[pr] = k[rs, hh * RET_DK:(hh + 1) * RET_DK].astype(BF16)
        vh[pr] = v[rs, hh * RET_DV:(hh + 1) * RET_DV]
        sc[pr] = _dot_tb(qh[pr], kh[pr])
        vk = (vh[pr] * kdec_ref[hh]).astype(BF16)
        for s in range(NS):
            kv[pr, s] = _dot_ta(kh[pr][s * C:(s + 1) * C], vk[s * C:(s + 1) * C])
    sprev = {}
    for pr in probs:
        c, hh = pr
        for s in range(NS):
            sidx = c * NS + s if has_s0 else 0
            st = sfin_ref[sidx, hh]
            sprev[pr, s] = st.astype(BF16)
            sfin_ref[sidx, hh] = st * cdec[hh] + kv[pr, s]
    for pr in probs:
        c, hh = pr
        rs = slice(c * R, (c + 1) * R)
        vs = slice(hh * RET_DV, (hh + 1) * RET_DV)
        o = _dot((sc[pr] * dmat_ref[hh]).astype(BF16), vh[pr].astype(BF16))
        qd = qdec_ref[hh]
        parts = [_dot(qh[pr][s * C:(s + 1) * C], sprev[pr, s]) * qd[s * C:(s + 1) * C] for s in range(NS)]
        o = o + (parts[0] if NS == 1 else jnp.concatenate(parts, axis=0))
        mu = jnp.mean(o, axis=-1, keepdims=True)
        d = o - mu
        var = jnp.mean(d * d, axis=-1, keepdims=True)
        on = d * lax.rsqrt(var + EPS) * gn_ref[:, vs]
        y_ref[rs, vs] = (g[rs, vs] * on).astype(y_ref.dtype)


def _ret_consts(R, NS):
    C = R // NS
    lg = np.log1p(-np.exp2(-5.0 - np.arange(RET_HEADS, dtype=np.float64)))
    idx = np.arange(R)
    tok = idx % C
    same = (idx[:, None] // C) == (idx[None, :] // C)
    rel = tok[:, None] - tok[None, :]
    low = same & (rel >= 0)
    dmat = np.where(low[None], np.exp(lg[:, None, None] * np.where(low, rel, 0)[None]), 0.0)
    qdec = np.exp(lg[:, None] * (tok[None, :] + 1.0))
    kdec = np.exp(lg[:, None] * (C - 1.0 - tok[None, :]))
    cdec = tuple(float(v) for v in np.exp(lg * C))
    bc = lambda a: jnp.asarray(np.broadcast_to(a[:, :, None], (RET_HEADS, R, LANES)), F32)
    return jnp.asarray(dmat, F32), bc(qdec), bc(kdec), cdec


def _ret_call(x, w, cos, sin, gn, s0, *, B, T, layer, out_dtype):
    N = x.shape[0]
    has_s0 = s0 is not None
    R = LIN_CHUNK
    if has_s0:
        C = math.gcd(T, LIN_CHUNK)
        assert C == T, "sample stream is a single chunk per sequence"
        NS = R // C
        RB = 2 * R
        nb, nt = N // RB, 1
        nsb = RB // C
    else:
        NS, RB = 1, 512
        assert T % RB == 0 and math.gcd(T, LIN_CHUNK) == R
        nb, nt = B, T // RB
        nsb = 1
    dmat, qdec, kdec, cdec = _ret_consts(R, NS)
    tr = cos.shape[0] // RB
    hq = RET_HEADS * RET_DK
    in_specs = [
        pl.BlockSpec((RB, D_MODEL), lambda b, t: (b * nt + t, 0)),
        _layer_spec(w, layer),
        pl.BlockSpec((RB, hq), lambda b, t: ((b * nt + t) % tr, 0)),
        pl.BlockSpec((RB, hq), lambda b, t: ((b * nt + t) % tr, 0)),
        _const_spec(gn.shape), _const_spec(dmat.shape), _const_spec(qdec.shape), _const_spec(kdec.shape),
    ]
    args = [x, w, cos, sin, gn, dmat, qdec, kdec]
    st_spec = pl.BlockSpec((nsb, RET_HEADS, RET_DK, RET_DV), lambda b, t: (b, 0, 0, 0))
    if has_s0:
        in_specs.append(pl.BlockSpec((None, nsb, RET_HEADS, RET_DK, RET_DV), lambda b, t: (layer, b, 0, 0, 0)))
        args.append(s0)
    nseq = N // T
    return pl.pallas_call(
        functools.partial(_ret_kernel, R=R, NS=NS, RB=RB, cdec=cdec, has_s0=has_s0),
        grid=(nb, nt),
        in_specs=in_specs,
        out_specs=[pl.BlockSpec((RB, RET_W), lambda b, t: (b * nt + t, 0)), st_spec],
        out_shape=[jax.ShapeDtypeStruct((N, RET_W), out_dtype),
                   jax.ShapeDtypeStruct((nseq, RET_HEADS, RET_DK, RET_DV), F32)],
        compiler_params=_params(("arbitrary", "arbitrary")),
        name="ret",
    )(*args)


def _mlap_kernel(x_ref, w_ref, wuq_ref, qn_ref, kvn_ref, cos_ref, sin_ref, *outs, TM, emit_kt):
    if emit_kt:
        qnope_ref, qpe_ref, ckv_ref, kpe_ref, gm_ref, kt_ref, vb_ref = outs
    else:
        qnope_ref, qpe_ref, ckv_ref, kpe_ref, gm_ref = outs
    h = _dot(x_ref[...].astype(BF16), w_ref[...])
    cq = h[:, 0:Q_LORA]
    cq = cq * lax.rsqrt(jnp.mean(cq * cq, axis=-1, keepdims=True) + EPS) * qn_ref[...]
    qf = _dot(cq.astype(BF16), wuq_ref[...])
    nw = MLA_HEADS * MLA_NOPE
    qnope_ref[...] = (qf[:, 0:nw] * QSCALE).astype(qnope_ref.dtype)
    cos = cos_ref[...]
    sin = sin_ref[...]
    qpe_ref[...] = _rope(qf[:, nw:nw + MLA_HEADS * MLA_ROPE], cos, sin, MLA_ROPE) * QSCALE
    c0 = Q_LORA
    kv = h[:, c0:c0 + KV_LORA]
    ckv = kv * lax.rsqrt(jnp.mean(kv * kv, axis=-1, keepdims=True) + EPS) * kvn_ref[...]
    ckv_ref[...] = ckv
    g0 = c0 + KV_LORA
    gm_ref[...] = _silu(h[:, g0:g0 + MLA_W]).astype(gm_ref.dtype)
    p0 = g0 + MLA_W
    kblk = _rope(h[:, p0:p0 + LANES], cos[:, 0:LANES], sin[:, 0:LANES], MLA_ROPE)
    kpe_ref[...] = kblk[:, 0:MLA_ROPE]
    if emit_kt:
        rep = kblk
        for i in range(1, LANES // MLA_ROPE):
            rep = rep + pltpu.roll(kblk, i * MLA_ROPE, 1)
        kct = jnp.concatenate([ckv, rep], axis=1).T
        for i in range(TM // ATT_TK):
            kt_ref[i] = kct[:, i * ATT_TK:(i + 1) * ATT_TK].astype(BF16)
        vb_ref[...] = ckv.astype(BF16)


def _mlap_call(x, w, wuq, qn, kvn, cos, sin, *, layer, act_dtype, emit_kt):
    N = x.shape[0]
    TM = 512
    assert N % TM == 0 and TM % ATT_TK == 0
    tr = cos.shape[0] // TM
    row = lambda wd: pl.BlockSpec((TM, wd), lambda i: (i, 0))
    tab = pl.BlockSpec((TM, MLA_HEADS * MLA_ROPE), lambda i: (i % tr, 0))
    nw = MLA_HEADS * MLA_NOPE
    out_specs = [row(nw), row(MLA_HEADS * MLA_ROPE), row(KV_LORA), row(MLA_ROPE), row(MLA_W)]
    out_shape = [jax.ShapeDtypeStruct((N, nw), act_dtype),
                 jax.ShapeDtypeStruct((N, MLA_HEADS * MLA_ROPE), F32),
                 jax.ShapeDtypeStruct((N, KV_LORA), F32),
                 jax.ShapeDtypeStruct((N, MLA_ROPE), F32),
                 jax.ShapeDtypeStruct((N, MLA_W), act_dtype)]
    if emit_kt:
        out_specs += [pl.BlockSpec((TM // ATT_TK, KCAT_W, ATT_TK), lambda i: (i, 0, 0)), row(KV_LORA)]
        out_shape += [jax.ShapeDtypeStruct((N // ATT_TK, KCAT_W, ATT_TK), BF16),
                      jax.ShapeDtypeStruct((N, KV_LORA), BF16)]
    return pl.pallas_call(
        functools.partial(_mlap_kernel, TM=TM, emit_kt=emit_kt),
        grid=(N // TM,),
        in_specs=[row(D_MODEL), _layer_spec(w, layer), _layer_spec(wuq, layer), _const_spec(qn.shape),
                  _const_spec(kvn.shape), tab, tab],
        out_specs=out_specs,
        out_shape=out_shape,
        compiler_params=_params(("arbitrary",)),
        name="mlap",
    )(x, w, wuq, qn, kvn, cos, sin)


def _attn_prefill_kernel(qn_ref, qpe_ref, kt_ref, v_ref, wuk_ref, gm_ref, wuv_ref, y_ref,
                         q_s, m_s, l_s, acc_s, *, BQ, TK):
    i = pl.program_id(1)
    H = MLA_HEADS
    lane = lax.broadcasted_iota(jnp.int32, (BQ, LANES), 1)
    for hh in range(H):
        ql = _dot(qn_ref[:, hh * MLA_NOPE:(hh + 1) * MLA_NOPE], wuk_ref[hh])
        q_s[hh * BQ:(hh + 1) * BQ, 0:KV_LORA] = ql.astype(BF16)
        per = LANES // MLA_ROPE
        blk = qpe_ref[:, (hh // per) * LANES:(hh // per + 1) * LANES]
        off = (hh % per) * MLA_ROPE
        keep = (lane >= off) & (lane < off + MLA_ROPE)
        q_s[hh * BQ:(hh + 1) * BQ, KV_LORA:KCAT_W] = jnp.where(keep, blk, 0.0).astype(BF16)
    m_s[...] = jnp.full(m_s.shape, NEG, F32)
    l_s[...] = jnp.zeros(l_s.shape, F32)
    acc_s[...] = jnp.zeros(acc_s.shape, F32)

    def steps(blocks):
        kts = [kt_ref[j] for j, _ in blocks]
        vbs = [v_ref[pl.ds(pl.multiple_of(j * TK, TK), TK), :] for j, _ in blocks]
        row = lax.broadcasted_iota(jnp.int32, (BQ, TK), 0)
        col = lax.broadcasted_iota(jnp.int32, (BQ, TK), 1)
        items = [(bi, hh) for bi in range(len(blocks)) for hh in range(H)]
        scores = {}
        for t in range(len(items) + ATT_LAG):
            if t < len(items):
                bi, hh = items[t]
                scores[t] = _dot(q_s[hh * BQ:(hh + 1) * BQ, :], kts[bi])
            if t < ATT_LAG:
                continue
            bi, hh = items[t - ATT_LAG]
            rs = slice(hh * BQ, (hh + 1) * BQ)
            s = scores.pop(t - ATT_LAG)
            if blocks[bi][1]:
                s = jnp.where(blocks[bi][0] * TK + col <= i * BQ + row, s, NEG)
            m_prev = m_s[rs, :]
            m_new = jnp.maximum(m_prev, jnp.max(s, axis=-1, keepdims=True))
            alpha = jnp.exp2(m_prev - m_new)
            p = jnp.exp2(s - jnp.concatenate([m_new] * (TK // LANES), axis=1))
            l_s[rs, :] = alpha[:, 0:1] * l_s[rs, :] + jnp.sum(p, axis=-1, keepdims=True)
            acc_s[rs, :] = (jnp.concatenate([alpha] * (KV_LORA // LANES), axis=1) * acc_s[rs, :]
                            + _dot(p.astype(BF16), vbs[bi]))
            m_s[rs, :] = m_new

    last = (i * BQ + BQ - 1) // TK
    npair = last // ATT_UNROLL

    def body(jj, carry):
        steps([(jj * ATT_UNROLL + u, False) for u in range(ATT_UNROLL)])
        return carry

    lax.fori_loop(0, npair, body, 0)

    def tail(j, carry):
        steps([(j, False)])
        return carry

    lax.fori_loop(npair * ATT_UNROLL, last, tail, 0)
    steps([(last, True)])

    for p2 in range(H // 2):
        halves = []
        for hh in (2 * p2, 2 * p2 + 1):
            rs = slice(hh * BQ, (hh + 1) * BQ)
            halves.append((acc_s[rs, :] / l_s[rs, :]).astype(BF16))
        yb = _dot(jnp.concatenate(halves, axis=1), wuv_ref[p2])
        y_ref[:, p2 * LANES:(p2 + 1) * LANES] = (gm_ref[:, p2 * LANES:(p2 + 1) * LANES].astype(F32) * yb).astype(y_ref.dtype)


def _attn_prefill_call(qn, qpe, kt, vb, wuk, gm, wuv, *, B, T, layer):
    N = qn.shape[0]
    BQ, TK = ATT_BQ, ATT_TK
    assert T % TK == 0 and T % BQ == 0
    nq = T // BQ
    H = MLA_HEADS
    row = lambda wd: pl.BlockSpec((BQ, wd), lambda b, i: (b * nq + i, 0))
    return pl.pallas_call(
        functools.partial(_attn_prefill_kernel, BQ=BQ, TK=TK),
        grid=(B, nq),
        in_specs=[row(H * MLA_NOPE), row(H * MLA_ROPE),
                  pl.BlockSpec((T // TK, KCAT_W, TK), lambda b, i: (b, 0, 0)),
                  pl.BlockSpec((T, KV_LORA), lambda b, i: (b, 0)),
                  _layer_spec(wuk, layer), row(MLA_W), _layer_spec(wuv, layer)],
        out_specs=row(MLA_W),
        out_shape=jax.ShapeDtypeStruct((N, MLA_W), BF16),
        scratch_shapes=[pltpu.VMEM((H * BQ, KCAT_W), BF16), pltpu.VMEM((H * BQ, LANES), F32),
                        pltpu.VMEM((H * BQ, 1), F32), pltpu.VMEM((H * BQ, KV_LORA), F32)],
        compiler_params=_params(("arbitrary", "arbitrary")),
        name="attn_prefill",
    )(qn, qpe, kt, vb, wuk, gm, wuv)


def _attn_decode_kernel(pt_ref, qn_ref, qpe_ref, ckvn_ref, kpen_ref, gm_ref, wuk_ref, wuv_ref,
                        cckv_ref, ckpe_ref, y_ref,
                        kbuf, pbuf, sem, *, T, G, NSTEP, NSUB, LOOK, layer, PAGE):
    b = pl.program_id(0)
    nb = pl.num_programs(0)
    H = MLA_HEADS

    def copies(bb, jj):
        out = []
        for g in range(G):
            page = pt_ref[bb, jj * G + g]
            out.append(pltpu.make_async_copy(cckv_ref.at[layer, page], kbuf.at[jj, g], sem.at[0, jj]))
            out.append(pltpu.make_async_copy(ckpe_ref.at[layer, page], pbuf.at[jj, g], sem.at[1, jj]))
        return out

    def start_all(cs):
        for c in cs:
            c.start()

    @pl.when(b == 0)
    def _():
        for jj in range(LOOK):
            start_all(copies(b, jj))

    ql = jnp.concatenate([_dot(qn_ref[:, hh * MLA_NOPE:(hh + 1) * MLA_NOPE].astype(BF16), wuk_ref[hh])
                          for hh in range(H)], axis=0).astype(BF16)
    qp = jnp.concatenate([qpe_ref[:, hh * MLA_ROPE:(hh + 1) * MLA_ROPE] for hh in range(H)], axis=0).astype(BF16)
    kn = ckvn_ref[...].astype(BF16)
    s = _dot_tb(ql, kn) + _dot_tb(qp, kpen_ref[...].astype(BF16))
    row = lax.broadcasted_iota(jnp.int32, s.shape, 0)
    col = lax.broadcasted_iota(jnp.int32, s.shape, 1)
    s = jnp.where(col <= row % T, s, NEG)
    m = jnp.max(s, axis=-1, keepdims=True)
    p = jnp.exp2(s - m)
    l = jnp.sum(p, axis=-1, keepdims=True)
    acc = _dot(p.astype(BF16), kn)

    GS = G // NSUB
    for j in range(NSTEP):
        jn = j + LOOK
        if jn < NSTEP:
            start_all(copies(b, jn))
        else:
            @pl.when(b + 1 < nb)
            def _():
                start_all(copies(b + 1, jn - NSTEP))
        for c in copies(b, j):
            c.wait()
        kbs, scores = [], []
        for u in range(NSUB):
            kb = kbuf[j, u * GS:(u + 1) * GS].reshape(GS * PAGE, KV_LORA).astype(BF16)
            pcat = jnp.concatenate([pbuf[j, u * GS + g] for g in range(GS)], axis=1).astype(BF16)
            kbs.append(kb)
            scores.append(_dot_tb(ql, kb) + _dot(qp, pcat))
        for u in range(NSUB):
            kb, s = kbs[u], scores[u]
            m_new = jnp.maximum(m, jnp.max(s, axis=-1, keepdims=True))
            alpha = jnp.exp2(m - m_new)
            p = jnp.exp2(s - m_new)
            l = alpha * l + jnp.sum(p, axis=-1, keepdims=True)
            acc = alpha * acc + _dot(p.astype(BF16), kb)
            m = m_new

    o = acc / l
    for p2 in range(H // 2):
        lhs = jnp.concatenate([o[(2 * p2) * T:(2 * p2 + 1) * T], o[(2 * p2 + 1) * T:(2 * p2 + 2) * T]], axis=1)
        yb = _dot(lhs.astype(BF16), wuv_ref[p2])
        y_ref[:, p2 * LANES:(p2 + 1) * LANES] = gm_ref[:, p2 * LANES:(p2 + 1) * LANES] * yb


def _attn_decode_call(page_table, qn, qpe, ckv, kpe, gm, wuk, wuv, cache_ckv, cache_kpet, *, B, T, layer):
    N = qn.shape[0]
    n_pages = page_table.shape[1]
    PAGE = cache_ckv.shape[2]
    G = math.gcd(n_pages, DEC_GROUP)
    if n_pages // G < 2 and G % 2 == 0:
        G //= 2
    NSTEP = n_pages // G
    assert NSTEP >= 2, "the page groups of one sample are double buffered"
    page_bytes = PAGE * (KV_LORA + MLA_ROPE) * 4
    assert n_pages * page_bytes <= VMEM_LIMIT // 2, "one sample's pages must fit the VMEM buffers"
    LOOK = min(DEC_LOOK, NSTEP - 1)
    NSUB = math.gcd(G, 4)
    H = MLA_HEADS
    row = lambda wd: pl.BlockSpec((T, wd), lambda b, pt: (b, 0))
    grid_spec = pltpu.PrefetchScalarGridSpec(
        num_scalar_prefetch=1,
        grid=(B,),
        in_specs=[row(H * MLA_NOPE), row(H * MLA_ROPE), row(KV_LORA), row(MLA_ROPE), row(MLA_W),
                  _layer_spec(wuk, layer), _layer_spec(wuv, layer),
                  pl.BlockSpec(memory_space=pl.ANY), pl.BlockSpec(memory_space=pl.ANY)],
        out_specs=row(MLA_W),
        scratch_shapes=[pltpu.VMEM((NSTEP, G, PAGE, KV_LORA), F32), pltpu.VMEM((NSTEP, G, MLA_ROPE, PAGE), F32),
                        pltpu.SemaphoreType.DMA((2, NSTEP))],
    )
    return pl.pallas_call(
        functools.partial(_attn_decode_kernel, T=T, G=G, NSTEP=NSTEP, NSUB=NSUB, LOOK=LOOK, layer=layer, PAGE=PAGE),
        grid_spec=grid_spec,
        out_shape=jax.ShapeDtypeStruct((N, MLA_W), F32),
        compiler_params=_params(("arbitrary",)),
        name="attn_decode",
    )(page_table, qn, qpe, ckv, kpe, gm, wuk, wuv, cache_ckv, cache_kpet)


def _sg_kernel(x_ref, w_ref, lng_ref, lnb_ref, wt_ref, bs_ref, y_ref, *rest, TM, emit_v):
    h = _dot(x_ref[...].astype(BF16), w_ref[...])
    u = _gelu(h[:, 0:SG_W])
    gv = _gelu(h[:, SG_W:2 * SG_W])
    mu = jnp.mean(gv, axis=-1, keepdims=True)
    d = gv - mu
    var = jnp.mean(d * d, axis=-1, keepdims=True)
    v = d * lax.rsqrt(var + EPS) * lng_ref[...] + lnb_ref[...]
    if emit_v:
        rest[0][...] = v
    gs = _silu(h[:, 2 * SG_W:3 * SG_W])
    vb = v.astype(BF16)
    for c in range(TM // SG_CHUNK):
        rows = slice(c * SG_CHUNK, (c + 1) * SG_CHUNK)
        for g in range(SG_GROUPS):
            cols = slice(g * SG_GDIM, (g + 1) * SG_GDIM)
            s = _dot(wt_ref[g], vb[rows, cols]) + bs_ref[g]
            y_ref[rows, cols] = (gs[rows, cols] * (u[rows, cols] * s)).astype(y_ref.dtype)


def _sg_call(x, w, lng, lnb, wt, bs, *, layer, emit_v, out_dtype):
    N = x.shape[0]
    TM = 512
    assert N % TM == 0
    row = lambda wd: pl.BlockSpec((TM, wd), lambda i: (i, 0))
    out_specs = [row(SG_W)]
    out_shape = [jax.ShapeDtypeStruct((N, SG_W), out_dtype)]
    if emit_v:
        out_specs.append(row(SG_W))
        out_shape.append(jax.ShapeDtypeStruct((N, SG_W), F32))
    return pl.pallas_call(
        functools.partial(_sg_kernel, TM=TM, emit_v=emit_v),
        grid=(N // TM,),
        in_specs=[row(D_MODEL), _layer_spec(w, layer), _const_spec(lng.shape), _const_spec(lnb.shape),
                  _layer_spec(wt, layer), _layer_spec(bs, layer)],
        out_specs=out_specs,
        out_shape=out_shape,
        compiler_params=_params(("arbitrary",)),
        name="sg",
    )(x, w, lng, lnb, wt, bs)


def _gd_kernel(*refs, R, NS, RB, has_s0):
    if has_s0:
        (x_ref, w_ref, cw_ref, av_ref, dtb_ref, ng_ref, msk_ref, bufp_ref, s0_ref,
         y_ref, sfin_ref, conv_ref) = refs
    else:
        (x_ref, w_ref, cw_ref, av_ref, dtb_ref, ng_ref, msk_ref,
         y_ref, sfin_ref, conv_ref, carry_s) = refs
    C = R // NS
    NLEV = _gd_levels(C)
    CH = GD_CONV_CH
    h = _dot(x_ref[...].astype(BF16), w_ref[...])
    xq = h[:, 0:CH]
    y = xq * cw_ref[GD_CONV - 1:GD_CONV, :]
    if has_s0:
        x3 = xq.reshape(RB // 8, 8, CH)
        b3 = bufp_ref[...].reshape(RB // 8, 8, CH)
        tpos = lax.broadcasted_iota(jnp.int32, x3.shape, 1)
        for k in range(1, GD_CONV):
            sh = jnp.where(tpos >= k, pltpu.roll(x3, k, 1), pltpu.roll(b3, k, 1))
            y = y + sh.reshape(RB, CH) * cw_ref[GD_CONV - 1 - k:GD_CONV - k, :]
        for s in range(RB // C):
            conv_ref[s] = xq[s * C + C - (GD_CONV - 1):s * C + C, :]
    else:
        @pl.when(pl.program_id(1) == 0)
        def _():
            carry_s[0:8, :] = jnp.zeros((8, CH), F32)
        carry_s[8:8 + RB, :] = xq
        for k in range(1, GD_CONV):
            y = y + carry_s[8 - k:8 - k + RB, :] * cw_ref[GD_CONV - 1 - k:GD_CONV - k, :]
        carry_s[0:8, :] = xq[RB - 8:RB]
        conv_ref[0] = xq[RB - (GD_CONV - 1):RB, :]
    qkv = _silu(y)
    hw = GD_HEADS * GD_DK
    qn, kn = [], []
    for hh in range(GD_HEADS):
        qb = qkv[:, hh * GD_DK:(hh + 1) * GD_DK]
        qn.append(qb * lax.rsqrt(jnp.sum(qb * qb, axis=-1, keepdims=True) + EPS) * (GD_DK ** -0.5))
        kb = qkv[:, hw + hh * GD_DK:hw + (hh + 1) * GD_DK]
        kn.append(kb * lax.rsqrt(jnp.sum(kb * kb, axis=-1, keepdims=True) + EPS))
    vv = qkv[:, 2 * hw:2 * hw + GD_W]
    gate = _silu(h[:, CH:CH + GD_W])
    ab = h[:, CH + GD_W:CH + GD_W + LANES]
    lane = lax.broadcasted_iota(jnp.int32, ab.shape, 1)
    gb = jnp.where(lane < GD_HEADS, av_ref[...] * jax.nn.softplus(ab + dtb_ref[...]), jax.nn.sigmoid(ab))

    if has_s0:
        sfin_ref[...] = s0_ref[...]
    else:
        @pl.when(pl.program_id(1) == 0)
        def _():
            sfin_ref[...] = jnp.zeros(sfin_ref.shape, F32)

    lowf = msk_ref[0]
    low = lowf > 0.0
    strictf = msk_ref[1]
    lowb = lowf.astype(BF16)
    sameb = msk_ref[2].astype(BF16)
    ri = lax.broadcasted_iota(jnp.int32, (R, R), 0)
    ci = lax.broadcasted_iota(jnp.int32, (R, R), 1)
    eye = (ri == ci).astype(F32)

    nch = RB // R
    probs = [(c, hh) for c in range(nch) for hh in range(GD_HEADS)]
    gbc = [gb[c * R:(c + 1) * R] for c in range(nch)]
    gc4 = [_dot_mask(lowb, g) for g in gbc]
    gl4 = [_dot_mask(sameb, g) for g in gbc]
    qh, kh, qkk = {}, {}, {}
    for pr in probs:
        c, hh = pr
        rs = slice(c * R, (c + 1) * R)
        qh[pr], kh[pr] = qn[hh][rs], kn[hh][rs]
        khb = kh[pr].astype(BF16)
        qkk[pr] = _dot_tb(jnp.concatenate([qh[pr].astype(BF16), khb], axis=0), khb)
    gcb, glb, beta, gam, qk, amat, mp, pinv = {}, {}, {}, {}, {}, {}, {}, {}
    for pr in probs:
        c, hh = pr
        gcb[pr] = jnp.broadcast_to(gc4[c][:, hh:hh + 1], (R, LANES))
        glb[pr] = jnp.broadcast_to(gl4[c][:, hh:hh + 1], (R, LANES))
        beta[pr] = jnp.broadcast_to(gbc[c][:, GD_HEADS + hh:GD_HEADS + hh + 1], (R, LANES))
        gi = gcb[pr][:, 0:R]
        gam[pr] = jnp.where(low, jnp.exp(jnp.where(low, gi - gi.T, 0.0)), 0.0)
        qk[pr] = (qkk[pr][0:R] * gam[pr]).astype(BF16)
        amat[pr] = strictf * (beta[pr][:, 0:R] * qkk[pr][R:2 * R] * gam[pr])
        mp[pr] = -(msk_ref[3] * amat[pr])
        pinv[pr] = eye + mp[pr]
    for _ in range(int(math.log2(min(C, GD_BASE))) - 1):
        for pr in probs:
            mpb = mp[pr].astype(BF16)
            mp[pr] = _dot(mpb, mpb)
        for pr in probs:
            pinv[pr] = pinv[pr] + _dot(pinv[pr].astype(BF16), mp[pr].astype(BF16))
    for lev in range(NLEV):
        wl = {pr: _dot((msk_ref[4 + lev] * amat[pr]).astype(BF16), pinv[pr].astype(BF16)) for pr in probs}
        for pr in probs:
            pinv[pr] = pinv[pr] - _dot(pinv[pr].astype(BF16), wl[pr].astype(BF16))
    pre = {}
    for pr in probs:
        c, hh = pr
        vh = vv[c * R:(c + 1) * R, hh * GD_DK:(hh + 1) * GD_DK]
        eg = jnp.exp(gcb[pr])
        rhs = jnp.concatenate([vh * beta[pr], kh[pr] * (beta[pr] * eg)], axis=1)
        sol = _dot(pinv[pr].astype(BF16), rhs.astype(BF16))
        pre[pr] = (sol[:, 0:GD_DV], sol[:, GD_DV:GD_DV + GD_DK].astype(BF16),
                   (qh[pr] * eg).astype(BF16), (kh[pr] * jnp.exp(glb[pr] - gcb[pr])).astype(BF16),
                   jnp.exp(glb[pr]))

    for c in range(nch):
        rs = slice(c * R, (c + 1) * R)
        seqs = [(hh, s) for hh in range(GD_HEADS) for s in range(NS)]
        st, sb, ws, os_ = {}, {}, {}, {}
        for hh, s in seqs:
            sidx = c * NS + s if has_s0 else 0
            st[hh, s] = sfin_ref[sidx, hh]
            sb[hh, s] = st[hh, s].astype(BF16)
        for hh, s in seqs:
            rows = slice(s * C, (s + 1) * C)
            u, wb, qd, kd, el = pre[c, hh]
            ws[hh, s] = _dot(wb[rows], sb[hh, s])
            os_[hh, s] = _dot(qd[rows], sb[hh, s])
        vnew = {}
        for hh in range(GD_HEADS):
            u = pre[c, hh][0]
            vns = [u[s * C:(s + 1) * C] - ws[hh, s] for s in range(NS)]
            vnew[hh] = vns[0] if NS == 1 else jnp.concatenate(vns, axis=0)
        for hh, s in seqs:
            rows = slice(s * C, (s + 1) * C)
            kd, el = pre[c, hh][3], pre[c, hh][4]
            sidx = c * NS + s if has_s0 else 0
            sfin_ref[sidx, hh] = (st[hh, s] * el[s * C:s * C + 1, :]
                                  + _dot_ta(kd[rows], vnew[hh][rows].astype(BF16)))
        for hh in range(GD_HEADS):
            sl = slice(hh * GD_DK, (hh + 1) * GD_DK)
            oi = [os_[hh, s] for s in range(NS)]
            o = (oi[0] if NS == 1 else jnp.concatenate(oi, axis=0)) + _dot(qk[c, hh], vnew[hh].astype(BF16))
            on = o * lax.rsqrt(jnp.mean(o * o, axis=-1, keepdims=True) + EPS) * ng_ref[...]
            y_ref[rs, sl] = (gate[rs, sl] * on).astype(y_ref.dtype)


def _gd_levels(C):
    base = min(C, GD_BASE)
    assert C % base == 0 and (C // base) & (C // base - 1) == 0 and base & (base - 1) == 0
    return int(math.log2(C // base))


def _gd_masks(R, NS):
    C = R // NS
    idx = np.arange(R)
    blk = lambda s: (idx[:, None] // s) == (idx[None, :] // s)
    same = blk(C)
    low = same & (idx[:, None] >= idx[None, :])
    strict = same & (idx[:, None] > idx[None, :])
    base = min(C, GD_BASE)
    masks = [low, strict, same, blk(base)]
    for lev in range(_gd_levels(C)):
        s = base * 2 ** (lev + 1)
        masks.append(blk(s) & ~blk(s // 2))
    return jnp.asarray(np.stack(masks).astype(np.float32))


def _gd_call(x, w, cw, av, dtb, ng, bufp, s0, *, B, T, layer, out_dtype):
    N = x.shape[0]
    has_s0 = s0 is not None
    R = LIN_CHUNK
    CH = GD_CONV_CH
    if has_s0:
        C = math.gcd(T, LIN_CHUNK)
        assert C == T and T == 8 and T >= GD_CONV - 1
        NS = R // C
        RB = 2 * R
        nb, nt = N // RB, 1
        nsb = RB // C
    else:
        NS, RB = 1, GD_RB
        assert T % RB == 0 and math.gcd(T, LIN_CHUNK) == R
        nb, nt = B, T // RB
        nsb = 1
    msk = _gd_masks(R, NS)
    rowspec = lambda wd: pl.BlockSpec((RB, wd), lambda b, t: (b * nt + t, 0))
    in_specs = [rowspec(D_MODEL), _layer_spec(w, layer), _layer_spec(cw, layer), _const_spec(av.shape),
                _const_spec(dtb.shape), _const_spec(ng.shape), _const_spec(msk.shape)]
    args = [x, w, cw, av, dtb, ng, msk]
    st_spec = pl.BlockSpec((nsb, GD_HEADS, GD_DK, GD_DV), lambda b, t: (b, 0, 0, 0))
    scratch = []
    if has_s0:
        in_specs += [pl.BlockSpec((None, RB, CH), lambda b, t: (layer, b * nt + t, 0)),
                     pl.BlockSpec((None, nsb, GD_HEADS, GD_DK, GD_DV), lambda b, t: (layer, b, 0, 0, 0))]
        args += [bufp, s0]
    else:
        scratch.append(pltpu.VMEM((8 + RB, CH), F32))
    nseq = N // T
    return pl.pallas_call(
        functools.partial(_gd_kernel, R=R, NS=NS, RB=RB, has_s0=has_s0),
        grid=(nb, nt),
        in_specs=in_specs,
        out_specs=[rowspec(GD_W), st_spec,
                   pl.BlockSpec((nsb, GD_CONV - 1, CH), lambda b, t: (b, 0, 0))],
        out_shape=[jax.ShapeDtypeStruct((N, GD_W), out_dtype),
                   jax.ShapeDtypeStruct((nseq, GD_HEADS, GD_DK, GD_DV), F32),
                   jax.ShapeDtypeStruct((nseq, GD_CONV - 1, CH), F32)],
        scratch_shapes=scratch,
        compiler_params=_params(("arbitrary", "arbitrary")),
        name="gd",
    )(*args)


def _back_kernel(x_ref, ya_ref, yb_ref, yc_ref, yd_ref, wm_ref, wb_ref, wo_ref, lg_ref, lb_ref, o_ref, *, alpha):
    x = x_ref[...]
    gates = jax.nn.sigmoid(_dot(x.astype(BF16), wm_ref[...]))
    merged = None
    for i, yr in enumerate((ya_ref, yb_ref, yc_ref, yd_ref)):
        t = gates[:, i * D_MODEL:(i + 1) * D_MODEL] * _dot(yr[...].astype(BF16), wb_ref[i])
        merged = t if merged is None else merged + t
    z = alpha * x + _dot(merged.astype(BF16), wo_ref[...])
    mu = jnp.mean(z, axis=-1, keepdims=True)
    d = z - mu
    var = jnp.mean(d * d, axis=-1, keepdims=True)
    o_ref[...] = d * lax.rsqrt(var + EPS) * lg_ref[...] + lb_ref[...]


def _back_call(x, ya, yb, yc, yd, wm, wb, wo, lg, lb, *, layer, alpha):
    N = x.shape[0]
    TM = 256
    assert N % TM == 0
    row = lambda wd: pl.BlockSpec((TM, wd), lambda i: (i, 0))
    return pl.pallas_call(
        functools.partial(_back_kernel, alpha=alpha),
        grid=(N // TM,),
        in_specs=[row(D_MODEL), row(BRANCH_W), row(BRANCH_W), row(BRANCH_W), row(BRANCH_W),
                  _layer_spec(wm, layer), _layer_spec(wb, layer), _layer_spec(wo, layer),
                  _const_spec(lg.shape), _const_spec(lb.shape)],
        out_specs=row(D_MODEL),
        out_shape=jax.ShapeDtypeStruct((N, D_MODEL), F32),
        compiler_params=_params(("arbitrary",)),
        name="back",
    )(x, ya, yb, yc, yd, wm, wb, wo, lg, lb)


def _rope_tables(pos, d, reps, rows):
    inv = ROPE_BASE ** (-jnp.arange(0, d, 2, dtype=F32) / d)
    ang = pos.astype(F32)[:, None] * inv[None, :]
    c, s = jnp.cos(ang), jnp.sin(ang)
    cos = jnp.tile(jnp.concatenate([c, c], axis=-1), (rows // pos.shape[0], reps))
    sin = jnp.tile(jnp.concatenate([-s, s], axis=-1), (rows // pos.shape[0], reps))
    return cos, sin


def _prep_weights(w_in, mla_w_uq, mla_w_uk, mla_w_uv, w_branch, w_out):
    o = _OFF
    cols = lambda a, b: w_in[:, :, o[a]:o[b]]
    depth = w_in.shape[0]
    zpad = lambda n: jnp.zeros((depth, D_MODEL, n), w_in.dtype)
    w_ret = cols(0, 4).astype(BF16)
    w_mla = jnp.concatenate([cols(4, 6), cols(7, 8), cols(6, 7), zpad(LANES - MLA_ROPE)], axis=-1).astype(BF16)
    w_sg = cols(8, 11).astype(BF16)
    w_gd = jnp.concatenate([cols(11, 12), cols(14, 15), cols(12, 14), zpad(LANES - 2 * GD_HEADS)], axis=-1).astype(BF16)
    w_merge = cols(15, 16).astype(BF16)
    uq = mla_w_uq.reshape(depth, Q_LORA, MLA_HEADS, MLA_NOPE + MLA_ROPE)
    w_uq = jnp.concatenate([uq[..., :MLA_NOPE].reshape(depth, Q_LORA, -1),
                            uq[..., MLA_NOPE:].reshape(depth, Q_LORA, -1)], axis=-1).astype(BF16)
    w_uk = jnp.transpose(mla_w_uk, (0, 2, 3, 1)).astype(BF16)
    uv = mla_w_uv.reshape(depth, KV_LORA, MLA_HEADS // 2, 2, MLA_V)
    z = jnp.zeros((depth, KV_LORA, MLA_HEADS // 2, MLA_V), mla_w_uv.dtype)
    top = jnp.concatenate([uv[:, :, :, 0], z], axis=-1)
    bot = jnp.concatenate([z, uv[:, :, :, 1]], axis=-1)
    w_uv = jnp.transpose(jnp.concatenate([top, bot], axis=1), (0, 2, 1, 3)).astype(BF16)
    return dict(w_ret=w_ret, w_mla=w_mla, w_sg=w_sg, w_gd=w_gd, w_merge=w_merge, w_uq=w_uq, w_uk=w_uk, w_uv=w_uv,
                w_branch=w_branch.astype(BF16), w_out=w_out.astype(BF16))


def _sg_mix_weights(sg_w, sg_b, C):
    reps = SG_CHUNK // C
    w = jnp.tril(sg_w[:, :, :C, :C])
    if reps > 1:
        eye = jnp.eye(reps, dtype=w.dtype)
        w = jnp.einsum('ab,dgij->dgaibj', eye, w).reshape(w.shape[0], SG_GROUPS, SG_CHUNK, SG_CHUNK)
    b = jnp.tile(sg_b[:, :, :C], (1, 1, reps))
    bs = jnp.broadcast_to(b[..., None], b.shape + (SG_GDIM,))
    return w.astype(BF16), bs.astype(F32)


def _layer_stream(x, l, W, P, *, B, T, tabs, sample):
    cosr, sinr, cosm, sinm = tabs
    act = F32 if sample is not None else BF16
    ya, s_ret = _ret_call(x, W['w_ret'], cosr, sinr, P['ret_gn_g'][l][None],
                          sample['state_ret'] if sample else None, B=B, T=T, layer=l, out_dtype=act)
    mo = _mlap_call(x, W['w_mla'], W['w_uq'], P['mla_q_norm'][l][None],
                    P['mla_kv_norm'][l][None], cosm, sinm, layer=l, act_dtype=act, emit_kt=sample is None)
    qn, qpe, ckv, kpe, gm = mo[:5]
    if sample:
        yb = _attn_decode_call(sample['page_table'], qn, qpe, ckv, kpe, gm, W['w_uk'], W['w_uv'],
                               sample['cache_ckv'], sample['cache_kpet'], B=B, T=T, layer=l)
    else:
        yb = _attn_prefill_call(qn, qpe, mo[5], mo[6], W['w_uk'], gm, W['w_uv'], B=B, T=T, layer=l)
    wt, bs = (P['sg_wt_s'], P['sg_bs_s']) if sample else (P['sg_wt_p'], P['sg_bs_p'])
    sg_out = _sg_call(x, W['w_sg'], P['sg_ln_g'][l][None], P['sg_ln_b'][l][None], wt, bs,
                      layer=l, emit_v=sample is not None, out_dtype=act)
    yc = sg_out[0]
    sgv = sg_out[1] if sample else None
    yd, s_gd, conv_new = _gd_call(x, W['w_gd'], P['gd_conv_w'], P['gd_av'][l], P['gd_dtb'][l],
                                  P['gd_norm_g'][l][None],
                                  sample['bufp'] if sample else None,
                                  sample['state_delta'] if sample else None, B=B, T=T, layer=l, out_dtype=act)
    xn = _back_call(x, ya, yb, yc, yd, W['w_merge'], W['w_branch'], W['w_out'],
                    P['ln_g'][l][None], P['ln_b'][l][None], layer=l, alpha=P['alpha'])
    return xn, ckv, kpe, s_ret, s_gd, conv_new, sgv


def kernel(x_prompt, x_sample, cache_ckv, cache_kpe, state_ret, state_delta, state_conv, page_table,
           w_in, ret_gn_g, mla_q_norm, mla_w_uq, mla_kv_norm, mla_w_uk, mla_w_uv,
           sg_ln_g, sg_ln_b, sg_w, sg_b, gd_conv_w, gd_a_log, gd_dt_bias, gd_norm_g,
           w_branch, w_out, ln_g, ln_b):
    depth = w_in.shape[0]
    bp, tp, _ = x_prompt.shape
    bs_, ts, _ = x_sample.shape
    past_len = page_table.shape[1] * cache_ckv.shape[2]
    W = _prep_weights(w_in, mla_w_uq, mla_w_uk, mla_w_uv, w_branch, w_out)
    pad_small = lambda v: jnp.pad(v, ((0, 0), (0, LANES - v.shape[-1])))[:, None, :]
    P = dict(ret_gn_g=ret_gn_g, mla_q_norm=mla_q_norm, mla_kv_norm=mla_kv_norm, sg_ln_g=sg_ln_g, sg_ln_b=sg_ln_b,
             gd_conv_w=gd_conv_w, gd_norm_g=gd_norm_g, ln_g=ln_g, ln_b=ln_b,
             gd_av=pad_small(-jnp.exp(gd_a_log)), gd_dtb=pad_small(gd_dt_bias),
             alpha=float((2.0 * depth) ** 0.25))
    P['sg_wt_p'], P['sg_bs_p'] = _sg_mix_weights(sg_w, sg_b, min(tp, SG_CHUNK))
    P['sg_wt_s'], P['sg_bs_s'] = _sg_mix_weights(sg_w, sg_b, min(ts, SG_CHUNK))

    pos_p = jnp.arange(tp)
    pos_s = past_len + jnp.arange(ts)
    tab_rows_s = 512
    tabs_p = _rope_tables(pos_p, RET_DK, RET_HEADS, tp) + _rope_tables(pos_p, MLA_ROPE, MLA_HEADS, tp)
    tabs_s = _rope_tables(pos_s, RET_DK, RET_HEADS, tab_rows_s) + _rope_tables(pos_s, MLA_ROPE, MLA_HEADS, tab_rows_s)

    bufp = jnp.pad(state_conv, ((0, 0), (0, 0), (ts - (GD_CONV - 1), 0), (0, 0))).reshape(depth, bs_ * ts, GD_CONV_CH)
    sample = dict(state_ret=state_ret, state_delta=state_delta, bufp=bufp, page_table=page_table,
                  cache_ckv=cache_ckv, cache_kpet=jnp.swapaxes(cache_kpe, 2, 3))

    xp = x_prompt.reshape(bp * tp, D_MODEL)
    xs = x_sample.reshape(bs_ * ts, D_MODEL)
    outs_p, outs_s = [], []
    for l in range(depth):
        xp, *rest = _layer_stream(xp, l, W, P, B=bp, T=tp, tabs=tabs_p, sample=None)
        outs_p.append(rest)
        xs, *rest = _layer_stream(xs, l, W, P, B=bs_, T=ts, tabs=tabs_s, sample=sample)
        outs_s.append(rest)

    def stack(outs, k, shape):
        return jnp.stack([o[k].reshape(shape) for o in outs])

    return (xp.reshape(bp, tp, D_MODEL), xs.reshape(bs_, ts, D_MODEL),
            stack(outs_p, 0, (bp, tp, KV_LORA)), stack(outs_p, 1, (bp, tp, MLA_ROPE)),
            stack(outs_p, 2, (bp, RET_HEADS, RET_DK, RET_DV)), stack(outs_p, 3, (bp, GD_HEADS, GD_DK, GD_DV)),
            stack(outs_p, 4, (bp, GD_CONV - 1, GD_CONV_CH)),
            stack(outs_s, 0, (bs_, ts, KV_LORA)), stack(outs_s, 1, (bs_, ts, MLA_ROPE)),
            stack(outs_s, 2, (bs_, RET_HEADS, RET_DK, RET_DV)), stack(outs_s, 3, (bs_, GD_HEADS, GD_DK, GD_DV)),
            stack(outs_s, 4, (bs_, GD_CONV - 1, GD_CONV_CH)), stack(outs_s, 5, (bs_, ts, SG_W)))
```

```python
import functools
import math

import numpy as np
import jax
import jax.numpy as jnp
from jax import lax
from jax.experimental import pallas as pl
from jax.experimental.pallas import tpu as pltpu

F32 = jnp.float32
BF16 = jnp.bfloat16

D_MODEL = 1024
RET_HEADS, RET_DK, RET_DV = 4, 64, 128
RET_W = RET_HEADS * RET_DV
MLA_HEADS, MLA_NOPE, MLA_ROPE, MLA_V = 8, 64, 32, 64
Q_LORA, KV_LORA = 384, 256
MLA_W = MLA_HEADS * MLA_V
MLA_SCALE = (MLA_NOPE + MLA_ROPE) ** -0.5
SG_GROUPS, SG_GDIM, SG_CHUNK = 4, 128, 128
SG_W = SG_GROUPS * SG_GDIM
GD_HEADS, GD_DK, GD_DV, GD_CONV = 4, 128, 128, 4
GD_W = GD_HEADS * GD_DV
GD_CONV_CH = 2 * GD_HEADS * GD_DK + GD_W
N_BRANCH, BRANCH_W = 4, 512
LIN_CHUNK = 64
ROPE_BASE = 10000.0
EPS = 1e-6
NEG = -1e30
IN_SPLITS = (RET_HEADS * RET_DK, RET_HEADS * RET_DK, RET_W, RET_W,
             Q_LORA, KV_LORA, MLA_ROPE, MLA_W,
             SG_W, SG_W, SG_W,
             GD_CONV_CH, GD_HEADS, GD_HEADS, GD_W,
             N_BRANCH * D_MODEL)
_OFF = np.concatenate([[0], np.cumsum(IN_SPLITS)]).tolist()

LANES = 128
KCAT_W = KV_LORA + LANES
ATT_TK = 256
ATT_BQ = 256
ATT_LAG = 6
ATT_UNROLL = 3
GD_RB = 512
DEC_GROUP = 32
DEC_LOOK = 3
GD_BASE = 8
QSCALE = MLA_SCALE * math.log2(math.e)
VMEM_LIMIT = 56 * 1024 * 1024

_TB = (((1,), (1,)), ((), ()))
_TA = (((0,), (0,)), ((), ()))


def _dot(a, b):
    return jnp.dot(a, b, preferred_element_type=F32)


def _dot_tb(a, b):
    return lax.dot_general(a, b, _TB, preferred_element_type=F32)


def _dot_ta(a, b):
    return lax.dot_general(a, b, _TA, preferred_element_type=F32)


def _dot_mask(mask_bf16, x):
    hi = x.astype(BF16)
    r1 = x - hi.astype(F32)
    mid = r1.astype(BF16)
    lo = (r1 - mid.astype(F32)).astype(BF16)
    return _dot(mask_bf16, hi) + (_dot(mask_bf16, mid) + _dot(mask_bf16, lo))


def _silu(x):
    return x * jax.nn.sigmoid(x)


def _gelu(x):
    return 0.5 * x * (1.0 + lax.erf(x * np.float32(math.sqrt(0.5))))


def _rope(x, cos, sin_signed, d):
    w = x.shape[-1]
    half = d // 2
    lane = lax.broadcasted_iota(jnp.int32, x.shape, 1)
    first = (lane % d) < half
    rot = jnp.where(first, pltpu.roll(x, w - half, 1), pltpu.roll(x, half, 1))
    return x * cos + rot * sin_signed


def _const_spec(shape):
    nd = len(shape)
    return pl.BlockSpec(shape, lambda *_: (0,) * nd, pipeline_mode=pl.Buffered(1))


def _layer_spec(a, layer):
    nd = a.ndim - 1
    return pl.BlockSpec((None,) + a.shape[1:], lambda *_: (layer,) + (0,) * nd, pipeline_mode=pl.Buffered(1))


def _params(sem):
    return pltpu.CompilerParams(dimension_semantics=sem, vmem_limit_bytes=VMEM_LIMIT)


def _ret_kernel(*refs, R, NS, RB, cdec, has_s0):
    if has_s0:
        (x_ref, w_ref, cos_ref, sin_ref, gn_ref, dmat_ref, qdec_ref, kdec_ref, s0_ref,
         y_ref, sfin_ref) = refs
    else:
        (x_ref, w_ref, cos_ref, sin_ref, gn_ref, dmat_ref, qdec_ref, kdec_ref,
         y_ref, sfin_ref) = refs
    C = R // NS
    h = _dot(x_ref[...].astype(BF16), w_ref[...])
    cos = cos_ref[...]
    sin = sin_ref[...]
    hq = RET_HEADS * RET_DK
    q = _rope(h[:, 0:hq], cos, sin, RET_DK)
    k = _rope(h[:, hq:2 * hq], cos, sin, RET_DK) * (RET_DK ** -0.5)
    v = h[:, 2 * hq:2 * hq + RET_W]
    g = _silu(h[:, 2 * hq + RET_W:2 * hq + 2 * RET_W])

    if has_s0:
        sfin_ref[...] = s0_ref[...]
    else:
        @pl.when(pl.program_id(1) == 0)
        def _():
            sfin_ref[...] = jnp.zeros(sfin_ref.shape, F32)

    probs = [(c, hh) for c in range(RB // R) for hh in range(RET_HEADS)]
    qh, kh, vh, sc, kv = {}, {}, {}, {}, {}
    for pr in probs:
        c, hh = pr
        rs = slice(c * R, (c + 1) * R)
        qh[pr] = q[rs, hh * RET_DK:(hh + 1) * RET_DK].astype(BF16)
        kh[pr] = k[rs, hh * RET_DK:(hh + 1) * RET_DK].astype(BF16)
        vh[pr] = v[rs, hh * RET_DV:(hh + 1) * RET_DV]
        sc[pr] = _dot_tb(qh[pr], kh[pr])
        vk = (vh[pr] * kdec_ref[hh]).astype(BF16)
        for s in range(NS):
            kv[pr, s] = _dot_ta(kh[pr][s * C:(s + 1) * C], vk[s * C:(s + 1) * C])
    sprev = {}
    for pr in probs:
        c, hh = pr
        for s in range(NS):
            sidx = c * NS + s if has_s0 else 0
            st = sfin_ref[sidx, hh]
            sprev[pr, s] = st.astype(BF16)
            sfin_ref[sidx, hh] = st * cdec[hh] + kv[pr, s]
    for pr in probs:
        c, hh = pr
        rs = slice(c * R, (c + 1) * R)
        vs = slice(hh * RET_DV, (hh + 1) * RET_DV)
        o = _dot((sc[pr] * dmat_ref[hh]).astype(BF16), vh[pr].astype(BF16))
        qd = qdec_ref[hh]
        parts = [_dot(qh[pr][s * C:(s + 1) * C], sprev[pr, s]) * qd[s * C:(s + 1) * C] for s in range(NS)]
        o = o + (parts[0] if NS == 1 else jnp.concatenate(parts, axis=0))
        mu = jnp.mean(o, axis=-1, keepdims=True)
        d = o - mu
        var = jnp.mean(d * d, axis=-1, keepdims=True)
        on = d * lax.rsqrt(var + EPS) * gn_ref[:, vs]
        y_ref[rs, vs] = (g[rs, vs] * on).astype(y_ref.dtype)


def _ret_consts(R, NS):
    C = R // NS
    lg = np.log1p(-np.exp2(-5.0 - np.arange(RET_HEADS, dtype=np.float64)))
    idx = np.arange(R)
    tok = idx % C
    same = (idx[:, None] // C) == (idx[None, :] // C)
    rel = tok[:, None] - tok[None, :]
    low = same & (rel >= 0)
    dmat = np.where(low[None], np.exp(lg[:, None, None] * np.where(low, rel, 0)[None]), 0.0)
    qdec = np.exp(lg[:, None] * (tok[None, :] + 1.0))
    kdec = np.exp(lg[:, None] * (C - 1.0 - tok[None, :]))
    cdec = tuple(float(v) for v in np.exp(lg * C))
    bc = lambda a: jnp.asarray(np.broadcast_to(a[:, :, None], (RET_HEADS, R, LANES)), F32)
    return jnp.asarray(dmat, F32), bc(qdec), bc(kdec), cdec


def _ret_call(x, w, cos, sin, gn, s0, *, B, T, layer, out_dtype):
    N = x.shape[0]
    has_s0 = s0 is not None
    R = LIN_CHUNK
    if has_s0:
        C = math.gcd(T, LIN_CHUNK)
        assert C == T, "sample stream is a single chunk per sequence"
        NS = R // C
        RB = 4 * R
        nb, nt = N // RB, 1
        nsb = RB // C
    else:
        NS, RB = 1, 512
        assert T % RB == 0 and math.gcd(T, LIN_CHUNK) == R
        nb, nt = B, T // RB
        nsb = 1
    dmat, qdec, kdec, cdec = _ret_consts(R, NS)
    tr = cos.shape[0] // RB
    hq = RET_HEADS * RET_DK
    in_specs = [
        pl.BlockSpec((RB, D_MODEL), lambda b, t: (b * nt + t, 0)),
        _layer_spec(w, layer),
        pl.BlockSpec((RB, hq), lambda b, t: ((b * nt + t) % tr, 0)),
        pl.BlockSpec((RB, hq), lambda b, t: ((b * nt + t) % tr, 0)),
        _const_spec(gn.shape), _const_spec(dmat.shape), _const_spec(qdec.shape), _const_spec(kdec.shape),
    ]
    args = [x, w, cos, sin, gn, dmat, qdec, kdec]
    st_spec = pl.BlockSpec((nsb, RET_HEADS, RET_DK, RET_DV), lambda b, t: (b, 0, 0, 0))
    if has_s0:
        in_specs.append(pl.BlockSpec((None, nsb, RET_HEADS, RET_DK, RET_DV), lambda b, t: (layer, b, 0, 0, 0)))
        args.append(s0)
    nseq = N // T
    return pl.pallas_call(
        functools.partial(_ret_kernel, R=R, NS=NS, RB=RB, cdec=cdec, has_s0=has_s0),
        grid=(nb, nt),
        in_specs=in_specs,
        out_specs=[pl.BlockSpec((RB, RET_W), lambda b, t: (b * nt + t, 0)), st_spec],
        out_shape=[jax.ShapeDtypeStruct((N, RET_W), out_dtype),
                   jax.ShapeDtypeStruct((nseq, RET_HEADS, RET_DK, RET_DV), F32)],
        compiler_params=_params(("arbitrary", "arbitrary")),
        name="ret",
    )(*args)


def _mlap_kernel(x_ref, w_ref, wuq_ref, qn_ref, kvn_ref, cos_ref, sin_ref, *outs, TM, emit_kt):
    if emit_kt:
        qnope_ref, qpe_ref, ckv_ref, kpe_ref, gm_ref, kt_ref, vb_ref = outs
    else:
        qnope_ref, qpe_ref, ckv_ref, kpe_ref, gm_ref = outs
    h = _dot(x_ref[...].astype(BF16), w_ref[...])
    cq = h[:, 0:Q_LORA]
    cq = cq * lax.rsqrt(jnp.mean(cq * cq, axis=-1, keepdims=True) + EPS) * qn_ref[...]
    qf = _dot(cq.astype(BF16), wuq_ref[...])
    nw = MLA_HEADS * MLA_NOPE
    qnope_ref[...] = (qf[:, 0:nw] * QSCALE).astype(qnope_ref.dtype)
    cos = cos_ref[...]
    sin = sin_ref[...]
    qpe_ref[...] = _rope(qf[:, nw:nw + MLA_HEADS * MLA_ROPE], cos, sin, MLA_ROPE) * QSCALE
    c0 = Q_LORA
    kv = h[:, c0:c0 + KV_LORA]
    ckv = kv * lax.rsqrt(jnp.mean(kv * kv, axis=-1, keepdims=True) + EPS) * kvn_ref[...]
    ckv_ref[...] = ckv
    g0 = c0 + KV_LORA
    gm_ref[...] = _silu(h[:, g0:g0 + MLA_W]).astype(gm_ref.dtype)
    p0 = g0 + MLA_W
    kblk = _rope(h[:, p0:p0 + LANES], cos[:, 0:LANES], sin[:, 0:LANES], MLA_ROPE)
    kpe_ref[...] = kblk[:, 0:MLA_ROPE]
    if emit_kt:
        rep = kblk
        for i in range(1, LANES // MLA_ROPE):
            rep = rep + pltpu.roll(kblk, i * MLA_ROPE, 1)
        kct = jnp.concatenate([ckv, rep], axis=1).T
        for i in range(TM // ATT_TK):
            kt_ref[i] = kct[:, i * ATT_TK:(i + 1) * ATT_TK].astype(BF16)
        vb_ref[...] = ckv.astype(BF16)


def _mlap_call(x, w, wuq, qn, kvn, cos, sin, *, layer, act_dtype, emit_kt):
    N = x.shape[0]
    TM = 512
    assert N % TM == 0 and TM % ATT_TK == 0
    tr = cos.shape[0] // TM
    row = lambda wd: pl.BlockSpec((TM, wd), lambda i: (i, 0))
    tab = pl.BlockSpec((TM, MLA_HEADS * MLA_ROPE), lambda i: (i % tr, 0))
    nw = MLA_HEADS * MLA_NOPE
    out_specs = [row(nw), row(MLA_HEADS * MLA_ROPE), row(KV_LORA), row(MLA_ROPE), row(MLA_W)]
    out_shape = [jax.ShapeDtypeStruct((N, nw), act_dtype),
                 jax.ShapeDtypeStruct((N, MLA_HEADS * MLA_ROPE), F32),
                 jax.ShapeDtypeStruct((N, KV_LORA), F32),
                 jax.ShapeDtypeStruct((N, MLA_ROPE), F32),
                 jax.ShapeDtypeStruct((N, MLA_W), act_dtype)]
    if emit_kt:
        out_specs += [pl.BlockSpec((TM // ATT_TK, KCAT_W, ATT_TK), lambda i: (i, 0, 0)), row(KV_LORA)]
        out_shape += [jax.ShapeDtypeStruct((N // ATT_TK, KCAT_W, ATT_TK), BF16),
                      jax.ShapeDtypeStruct((N, KV_LORA), BF16)]
    return pl.pallas_call(
        functools.partial(_mlap_kernel, TM=TM, emit_kt=emit_kt),
        grid=(N // TM,),
        in_specs=[row(D_MODEL), _layer_spec(w, layer), _layer_spec(wuq, layer), _const_spec(qn.shape),
                  _const_spec(kvn.shape), tab, tab],
        out_specs=out_specs,
        out_shape=out_shape,
        compiler_params=_params(("arbitrary",)),
        name="mlap",
    )(x, w, wuq, qn, kvn, cos, sin)


def _attn_prefill_kernel(qn_ref, qpe_ref, kt_ref, v_ref, wuk_ref, gm_ref, wuv_ref, y_ref,
                         q_s, m_s, l_s, acc_s, *, BQ, TK):
    i = pl.program_id(1)
    H = MLA_HEADS
    lane = lax.broadcasted_iota(jnp.int32, (BQ, LANES), 1)
    for hh in range(H):
        ql = _dot(qn_ref[:, hh * MLA_NOPE:(hh + 1) * MLA_NOPE], wuk_ref[hh])
        q_s[hh * BQ:(hh + 1) * BQ, 0:KV_LORA] = ql.astype(BF16)
        per = LANES // MLA_ROPE
        blk = qpe_ref[:, (hh // per) * LANES:(hh // per + 1) * LANES]
        off = (hh % per) * MLA_ROPE
        keep = (lane >= off) & (lane < off + MLA_ROPE)
        q_s[hh * BQ:(hh + 1) * BQ, KV_LORA:KCAT_W] = jnp.where(keep, blk, 0.0).astype(BF16)
    m_s[...] = jnp.full(m_s.shape, NEG, F32)
    l_s[...] = jnp.zeros(l_s.shape, F32)
    acc_s[...] = jnp.zeros(acc_s.shape, F32)

    def steps(blocks):
        kts = [kt_ref[j] for j, _ in blocks]
        vbs = [v_ref[pl.ds(pl.multiple_of(j * TK, TK), TK), :] for j, _ in blocks]
        row = lax.broadcasted_iota(jnp.int32, (BQ, TK), 0)
        col = lax.broadcasted_iota(jnp.int32, (BQ, TK), 1)
        items = [(bi, hh) for bi in range(len(blocks)) for hh in range(H)]
        scores = {}
        for t in range(len(items) + ATT_LAG):
            if t < len(items):
                bi, hh = items[t]
                scores[t] = _dot(q_s[hh * BQ:(hh + 1) * BQ, :], kts[bi])
            if t < ATT_LAG:
                continue
            bi, hh = items[t - ATT_LAG]
            rs = slice(hh * BQ, (hh + 1) * BQ)
            s = scores.pop(t - ATT_LAG)
            if blocks[bi][1]:
                s = jnp.where(blocks[bi][0] * TK + col <= i * BQ + row, s, NEG)
            m_prev = m_s[rs, :]
            m_new = jnp.maximum(m_prev, jnp.max(s, axis=-1, keepdims=True))
            alpha = jnp.exp2(m_prev - m_new)
            p = jnp.exp2(s - jnp.concatenate([m_new] * (TK // LANES), axis=1))
            l_s[rs, :] = alpha[:, 0:1] * l_s[rs, :] + jnp.sum(p, axis=-1, keepdims=True)
            acc_s[rs, :] = (jnp.concatenate([alpha] * (KV_LORA // LANES), axis=1) * acc_s[rs, :]
                            + _dot(p.astype(BF16), vbs[bi]))
            m_s[rs, :] = m_new

    last = (i * BQ + BQ - 1) // TK
    npair = last // ATT_UNROLL

    def body(jj, carry):
        steps([(jj * ATT_UNROLL + u, False) for u in range(ATT_UNROLL)])
        return carry

    lax.fori_loop(0, npair, body, 0)

    def tail(j, carry):
        steps([(j, False)])
        return carry

    lax.fori_loop(npair * ATT_UNROLL, last, tail, 0)
    steps([(last, True)])

    for p2 in range(H // 2):
        halves = []
        for hh in (2 * p2, 2 * p2 + 1):
            rs = slice(hh * BQ, (hh + 1) * BQ)
            halves.append((acc_s[rs, :] / l_s[rs, :]).astype(BF16))
        yb = _dot(jnp.concatenate(halves, axis=1), wuv_ref[p2])
        y_ref[:, p2 * LANES:(p2 + 1) * LANES] = (gm_ref[:, p2 * LANES:(p2 + 1) * LANES].astype(F32) * yb).astype(y_ref.dtype)


def _attn_prefill_call(qn, qpe, kt, vb, wuk, gm, wuv, *, B, T, layer):
    N = qn.shape[0]
    BQ, TK = ATT_BQ, ATT_TK
    assert T % TK == 0 and T % BQ == 0
    nq = T // BQ
    H = MLA_HEADS
    row = lambda wd: pl.BlockSpec((BQ, wd), lambda b, i: (b * nq + i, 0))
    return pl.pallas_call(
        functools.partial(_attn_prefill_kernel, BQ=BQ, TK=TK),
        grid=(B, nq),
        in_specs=[row(H * MLA_NOPE), row(H * MLA_ROPE),
                  pl.BlockSpec((T // TK, KCAT_W, TK), lambda b, i: (b, 0, 0)),
                  pl.BlockSpec((T, KV_LORA), lambda b, i: (b, 0)),
                  _layer_spec(wuk, layer), row(MLA_W), _layer_spec(wuv, layer)],
        out_specs=row(MLA_W),
        out_shape=jax.ShapeDtypeStruct((N, MLA_W), BF16),
        scratch_shapes=[pltpu.VMEM((H * BQ, KCAT_W), BF16), pltpu.VMEM((H * BQ, LANES), F32),
                        pltpu.VMEM((H * BQ, 1), F32), pltpu.VMEM((H * BQ, KV_LORA), F32)],
        compiler_params=_params(("arbitrary", "arbitrary")),
        name="attn_prefill",
    )(qn, qpe, kt, vb, wuk, gm, wuv)


def _attn_decode_kernel(pt_ref, qn_ref, qpe_ref, ckvn_ref, kpen_ref, gm_ref, wuk_ref, wuv_ref,
                        cckv_ref, ckpe_ref, y_ref,
                        kbuf, pbuf, sem, *, T, G, NSTEP, NSUB, LOOK, layer, PAGE):
    b = pl.program_id(0)
    nb = pl.num_programs(0)
    H = MLA_HEADS

    def copies(bb, jj):
        out = []
        for g in range(G):
            page = pt_ref[bb, jj * G + g]
            out.append(pltpu.make_async_copy(cckv_ref.at[layer, page], kbuf.at[jj, g], sem.at[0, jj]))
            out.append(pltpu.make_async_copy(ckpe_ref.at[layer, page], pbuf.at[jj, g], sem.at[1, jj]))
        return out

    def start_all(cs):
        for c in cs:
            c.start()

    @pl.when(b == 0)
    def _():
        for jj in range(LOOK):
            start_all(copies(b, jj))

    ql = jnp.concatenate([_dot(qn_ref[:, hh * MLA_NOPE:(hh + 1) * MLA_NOPE].astype(BF16), wuk_ref[hh])
                          for hh in range(H)], axis=0).astype(BF16)
    qp = jnp.concatenate([qpe_ref[:, hh * MLA_ROPE:(hh + 1) * MLA_ROPE] for hh in range(H)], axis=0).astype(BF16)
    kn = ckvn_ref[...].astype(BF16)
    s = _dot_tb(ql, kn) + _dot_tb(qp, kpen_ref[...].astype(BF16))
    row = lax.broadcasted_iota(jnp.int32, s.shape, 0)
    col = lax.broadcasted_iota(jnp.int32, s.shape, 1)
    s = jnp.where(col <= row % T, s, NEG)
    m = jnp.max(s, axis=-1, keepdims=True)
    p = jnp.exp2(s - m)
    l = jnp.sum(p, axis=-1, keepdims=True)
    acc = _dot(p.astype(BF16), kn)

    GS = G // NSUB
    for j in range(NSTEP):
        jn = j + LOOK
        if jn < NSTEP:
            start_all(copies(b, jn))
        else:
            @pl.when(b + 1 < nb)
            def _():
                start_all(copies(b + 1, jn - NSTEP))
        for c in copies(b, j):
            c.wait()
        kbs, scores = [], []
        for u in range(NSUB):
            kb = kbuf[j, u * GS:(u + 1) * GS].reshape(GS * PAGE, KV_LORA).astype(BF16)
            pcat = jnp.concatenate([pbuf[j, u * GS + g] for g in range(GS)], axis=1).astype(BF16)
            kbs.append(kb)
            scores.append(_dot_tb(ql, kb) + _dot(qp, pcat))
        for u in range(NSUB):
            kb, s = kbs[u], scores[u]
            m_new = jnp.maximum(m, jnp.max(s, axis=-1, keepdims=True))
            alpha = jnp.exp2(m - m_new)
            p = jnp.exp2(s - m_new)
            l = alpha * l + jnp.sum(p, axis=-1, keepdims=True)
            acc = alpha * acc + _dot(p.astype(BF16), kb)
            m = m_new

    o = acc / l
    for p2 in range(H // 2):
        lhs = jnp.concatenate([o[(2 * p2) * T:(2 * p2 + 1) * T], o[(2 * p2 + 1) * T:(2 * p2 + 2) * T]], axis=1)
        yb = _dot(lhs.astype(BF16), wuv_ref[p2])
        y_ref[:, p2 * LANES:(p2 + 1) * LANES] = gm_ref[:, p2 * LANES:(p2 + 1) * LANES] * yb


def _attn_decode_call(page_table, qn, qpe, ckv, kpe, gm, wuk, wuv, cache_ckv, cache_kpet, *, B, T, layer):
    N = qn.shape[0]
    n_pages = page_table.shape[1]
    PAGE = cache_ckv.shape[2]
    G = math.gcd(n_pages, DEC_GROUP)
    if n_pages // G < 2 and G % 2 == 0:
        G //= 2
    NSTEP = n_pages // G
    assert NSTEP >= 2, "the page groups of one sample are double buffered"
    page_bytes = PAGE * (KV_LORA + MLA_ROPE) * 4
    assert n_pages * page_bytes <= VMEM_LIMIT // 2, "one sample's pages must fit the VMEM buffers"
    LOOK = min(DEC_LOOK, NSTEP - 1)
    NSUB = math.gcd(G, 4)
    H = MLA_HEADS
    row = lambda wd: pl.BlockSpec((T, wd), lambda b, pt: (b, 0))
    grid_spec = pltpu.PrefetchScalarGridSpec(
        num_scalar_prefetch=1,
        grid=(B,),
        in_specs=[row(H * MLA_NOPE), row(H * MLA_ROPE), row(KV_LORA), row(MLA_ROPE), row(MLA_W),
                  _layer_spec(wuk, layer), _layer_spec(wuv, layer),
                  pl.BlockSpec(memory_space=pl.ANY), pl.BlockSpec(memory_space=pl.ANY)],
        out_specs=row(MLA_W),
        scratch_shapes=[pltpu.VMEM((NSTEP, G, PAGE, KV_LORA), F32), pltpu.VMEM((NSTEP, G, MLA_ROPE, PAGE), F32),
                        pltpu.SemaphoreType.DMA((2, NSTEP))],
    )
    return pl.pallas_call(
        functools.partial(_attn_decode_kernel, T=T, G=G, NSTEP=NSTEP, NSUB=NSUB, LOOK=LOOK, layer=layer, PAGE=PAGE),
        grid_spec=grid_spec,
        out_shape=jax.ShapeDtypeStruct((N, MLA_W), F32),
        compiler_params=_params(("arbitrary",)),
        name="attn_decode",
    )(page_table, qn, qpe, ckv, kpe, gm, wuk, wuv, cache_ckv, cache_kpet)


def _sg_kernel(x_ref, w_ref, lng_ref, lnb_ref, wt_ref, bs_ref, y_ref, *rest, TM, emit_v):
    h = _dot(x_ref[...].astype(BF16), w_ref[...])
    u = _gelu(h[:, 0:SG_W])
    gv = _gelu(h[:, SG_W:2 * SG_W])
    mu = jnp.mean(gv, axis=-1, keepdims=True)
    d = gv - mu
    var = jnp.mean(d * d, axis=-1, keepdims=True)
    v = d * lax.rsqrt(var + EPS) * lng_ref[...] + lnb_ref[...]
    if emit_v:
        rest[0][...] = v
    gs = _silu(h[:, 2 * SG_W:3 * SG_W])
    vb = v.astype(BF16)
    nc = TM // SG_CHUNK
    for g in range(SG_GROUPS):
        cols = slice(g * SG_GDIM, (g + 1) * SG_GDIM)
        vcat = jnp.concatenate([vb[c * SG_CHUNK:(c + 1) * SG_CHUNK, cols] for c in range(nc)], axis=1)
        sall = _dot(wt_ref[g], vcat)
        for c in range(nc):
            rows = slice(c * SG_CHUNK, (c + 1) * SG_CHUNK)
            s = sall[:, c * SG_GDIM:(c + 1) * SG_GDIM] + bs_ref[g]
            y_ref[rows, cols] = (gs[rows, cols] * (u[rows, cols] * s)).astype(y_ref.dtype)


def _sg_call(x, w, lng, lnb, wt, bs, *, layer, emit_v, out_dtype):
    N = x.shape[0]
    TM = 512
    assert N % TM == 0
    row = lambda wd: pl.BlockSpec((TM, wd), lambda i: (i, 0))
    out_specs = [row(SG_W)]
    out_shape = [jax.ShapeDtypeStruct((N, SG_W), out_dtype)]
    if emit_v:
        out_specs.append(row(SG_W))
        out_shape.append(jax.ShapeDtypeStruct((N, SG_W), F32))
    return pl.pallas_call(
        functools.partial(_sg_kernel, TM=TM, emit_v=emit_v),
        grid=(N // TM,),
        in_specs=[row(D_MODEL), _layer_spec(w, layer), _const_spec(lng.shape), _const_spec(lnb.shape),
                  _layer_spec(wt, layer), _layer_spec(bs, layer)],
        out_specs=out_specs,
        out_shape=out_shape,
        compiler_params=_params(("arbitrary",)),
        name="sg",
    )(x, w, lng, lnb, wt, bs)


def _gd_kernel(*refs, R, NS, RB, has_s0):
    if has_s0:
        (x_ref, w_ref, cw_ref, av_ref, dtb_ref, ng_ref, msk_ref, bufp_ref, s0_ref,
         y_ref, sfin_ref, conv_ref) = refs
    else:
        (x_ref, w_ref, cw_ref, av_ref, dtb_ref, ng_ref, msk_ref,
         y_ref, sfin_ref, conv_ref, carry_s) = refs
    C = R // NS
    NLEV = _gd_levels(C)
    CH = GD_CONV_CH
    h = _dot(x_ref[...].astype(BF16), w_ref[...])
    xq = h[:, 0:CH]
    y = xq * cw_ref[GD_CONV - 1:GD_CONV, :]
    if has_s0:
        x3 = xq.reshape(RB // 8, 8, CH)
        b3 = bufp_ref[...].reshape(RB // 8, 8, CH)
        tpos = lax.broadcasted_iota(jnp.int32, x3.shape, 1)
        for k in range(1, GD_CONV):
            sh = jnp.where(tpos >= k, pltpu.roll(x3, k, 1), pltpu.roll(b3, k, 1))
            y = y + sh.reshape(RB, CH) * cw_ref[GD_CONV - 1 - k:GD_CONV - k, :]
        for s in range(RB // C):
            conv_ref[s] = xq[s * C + C - (GD_CONV - 1):s * C + C, :]
    else:
        @pl.when(pl.program_id(1) == 0)
        def _():
            carry_s[0:8, :] = jnp.zeros((8, CH), F32)
        carry_s[8:8 + RB, :] = xq
        for k in range(1, GD_CONV):
            y = y + carry_s[8 - k:8 - k + RB, :] * cw_ref[GD_CONV - 1 - k:GD_CONV - k, :]
        carry_s[0:8, :] = xq[RB - 8:RB]
        conv_ref[0] = xq[RB - (GD_CONV - 1):RB, :]
    qkv = _silu(y)
    hw = GD_HEADS * GD_DK
    qn, kn = [], []
    for hh in range(GD_HEADS):
        qb = qkv[:, hh * GD_DK:(hh + 1) * GD_DK]
        qn.append(qb * lax.rsqrt(jnp.sum(qb * qb, axis=-1, keepdims=True) + EPS) * (GD_DK ** -0.5))
        kb = qkv[:, hw + hh * GD_DK:hw + (hh + 1) * GD_DK]
        kn.append(kb * lax.rsqrt(jnp.sum(kb * kb, axis=-1, keepdims=True) + EPS))
    vv = qkv[:, 2 * hw:2 * hw + GD_W]
    gate = _silu(h[:, CH:CH + GD_W])
    ab = h[:, CH + GD_W:CH + GD_W + LANES]
    lane = lax.broadcasted_iota(jnp.int32, ab.shape, 1)
    gb = jnp.where(lane < GD_HEADS, av_ref[...] * jax.nn.softplus(ab + dtb_ref[...]), jax.nn.sigmoid(ab))

    if has_s0:
        sfin_ref[...] = s0_ref[...]
    else:
        @pl.when(pl.program_id(1) == 0)
        def _():
            sfin_ref[...] = jnp.zeros(sfin_ref.shape, F32)

    lowf = msk_ref[0]
    low = lowf > 0.0
    strictf = msk_ref[1]
    lowb = lowf.astype(BF16)
    sameb = msk_ref[2].astype(BF16)
    ri = lax.broadcasted_iota(jnp.int32, (R, R), 0)
    ci = lax.broadcasted_iota(jnp.int32, (R, R), 1)
    eye = (ri == ci).astype(F32)

    nch = RB // R
    probs = [(c, hh) for c in range(nch) for hh in range(GD_HEADS)]
    gbc = [gb[c * R:(c + 1) * R] for c in range(nch)]
    gc4 = [_dot_mask(lowb, g) for g in gbc]
    gl4 = [_dot_mask(sameb, g) for g in gbc]
    qh, kh, qkk = {}, {}, {}
    for pr in probs:
        c, hh = pr
        rs = slice(c * R, (c + 1) * R)
        qh[pr], kh[pr] = qn[hh][rs], kn[hh][rs]
        khb = kh[pr].astype(BF16)
        qkk[pr] = _dot_tb(jnp.concatenate([qh[pr].astype(BF16), khb], axis=0), khb)
    gcb, glb, beta, gam, qk, amat, mp, pinv = {}, {}, {}, {}, {}, {}, {}, {}
    for pr in probs:
        c, hh = pr
        gcb[pr] = jnp.broadcast_to(gc4[c][:, hh:hh + 1], (R, LANES))
        glb[pr] = jnp.broadcast_to(gl4[c][:, hh:hh + 1], (R, LANES))
        beta[pr] = jnp.broadcast_to(gbc[c][:, GD_HEADS + hh:GD_HEADS + hh + 1], (R, LANES))
        gi = gcb[pr][:, 0:R]
        gam[pr] = jnp.where(low, jnp.exp(jnp.where(low, gi - gi.T, 0.0)), 0.0)
        qk[pr] = (qkk[pr][0:R] * gam[pr]).astype(BF16)
        amat[pr] = strictf * (beta[pr][:, 0:R] * qkk[pr][R:2 * R] * gam[pr])
        mp[pr] = -(msk_ref[3] * amat[pr])
        pinv[pr] = eye + mp[pr]
    for _ in range(int(math.log2(min(C, GD_BASE))) - 1):
        for pr in probs:
            mpb = mp[pr].astype(BF16)
            mp[pr] = _dot(mpb, mpb)
        for pr in probs:
            pinv[pr] = pinv[pr] + _dot(pinv[pr].astype(BF16), mp[pr].astype(BF16))
    for lev in range(NLEV):
        wl = {pr: _dot((msk_ref[4 + lev] * amat[pr]).astype(BF16), pinv[pr].astype(BF16)) for pr in probs}
        for pr in probs:
            pinv[pr] = pinv[pr] - _dot(pinv[pr].astype(BF16), wl[pr].astype(BF16))
    pre = {}
    for pr in probs:
        c, hh = pr
        vh = vv[c * R:(c + 1) * R, hh * GD_DK:(hh + 1) * GD_DK]
        eg = jnp.exp(gcb[pr])
        rhs = jnp.concatenate([vh * beta[pr], kh[pr] * (beta[pr] * eg)], axis=1)
        sol = _dot(pinv[pr].astype(BF16), rhs.astype(BF16))
        pre[pr] = (sol[:, 0:GD_DV], sol[:, GD_DV:GD_DV + GD_DK].astype(BF16),
                   (qh[pr] * eg).astype(BF16), (kh[pr] * jnp.exp(glb[pr] - gcb[pr])).astype(BF16),
                   jnp.exp(glb[pr]))

    for c in range(nch):
        rs = slice(c * R, (c + 1) * R)
        seqs = [(hh, s) for hh in range(GD_HEADS) for s in range(NS)]
        st, sb, ws, os_ = {}, {}, {}, {}
        for hh, s in seqs:
            sidx = c * NS + s if has_s0 else 0
            st[hh, s] = sfin_ref[sidx, hh]
            sb[hh, s] = st[hh, s].astype(BF16)
        for hh, s in seqs:
            rows = slice(s * C, (s + 1) * C)
            u, wb, qd, kd, el = pre[c, hh]
            ws[hh, s] = _dot(wb[rows], sb[hh, s])
            os_[hh, s] = _dot(qd[rows], sb[hh, s])
        vnew = {}
        for hh in range(GD_HEADS):
            u = pre[c, hh][0]
            vns = [u[s * C:(s + 1) * C] - ws[hh, s] for s in range(NS)]
            vnew[hh] = vns[0] if NS == 1 else jnp.concatenate(vns, axis=0)
        for hh, s in seqs:
            rows = slice(s * C, (s + 1) * C)
            kd, el = pre[c, hh][3], pre[c, hh][4]
            sidx = c * NS + s if has_s0 else 0
            sfin_ref[sidx, hh] = (st[hh, s] * el[s * C:s * C + 1, :]
                                  + _dot_ta(kd[rows], vnew[hh][rows].astype(BF16)))
        for hh in range(GD_HEADS):
            sl = slice(hh * GD_DK, (hh + 1) * GD_DK)
            oi = [os_[hh, s] for s in range(NS)]
            o = (oi[0] if NS == 1 else jnp.concatenate(oi, axis=0)) + _dot(qk[c, hh], vnew[hh].astype(BF16))
            on = o * lax.rsqrt(jnp.mean(o * o, axis=-1, keepdims=True) + EPS) * ng_ref[...]
            y_ref[rs, sl] = (gate[rs, sl] * on).astype(y_ref.dtype)


def _gd_levels(C):
    base = min(C, GD_BASE)
    assert C % base == 0 and (C // base) & (C // base - 1) == 0 and base & (base - 1) == 0
    return int(math.log2(C // base))


def _gd_masks(R, NS):
    C = R // NS
    idx = np.arange(R)
    blk = lambda s: (idx[:, None] // s) == (idx[None, :] // s)
    same = blk(C)
    low = same & (idx[:, None] >= idx[None, :])
    strict = same & (idx[:, None] > idx[None, :])
    base = min(C, GD_BASE)
    masks = [low, strict, same, blk(base)]
    for lev in range(_gd_levels(C)):
        s = base * 2 ** (lev + 1)
        masks.append(blk(s) & ~blk(s // 2))
    return jnp.asarray(np.stack(masks).astype(np.float32))


def _gd_call(x, w, cw, av, dtb, ng, bufp, s0, *, B, T, layer, out_dtype):
    N = x.shape[0]
    has_s0 = s0 is not None
    R = LIN_CHUNK
    CH = GD_CONV_CH
    if has_s0:
        C = math.gcd(T, LIN_CHUNK)
        assert C == T and T == 8 and T >= GD_CONV - 1
        NS = R // C
        RB = 4 * R
        nb, nt = N // RB, 1
        nsb = RB // C
    else:
        NS, RB = 1, GD_RB
        assert T % RB == 0 and math.gcd(T, LIN_CHUNK) == R
        nb, nt = B, T // RB
        nsb = 1
    msk = _gd_masks(R, NS)
    rowspec = lambda wd: pl.BlockSpec((RB, wd), lambda b, t: (b * nt + t, 0))
    in_specs = [rowspec(D_MODEL), _layer_spec(w, layer), _layer_spec(cw, layer), _const_spec(av.shape),
                _const_spec(dtb.shape), _const_spec(ng.shape), _const_spec(msk.shape)]
    args = [x, w, cw, av, dtb, ng, msk]
    st_spec = pl.BlockSpec((nsb, GD_HEADS, GD_DK, GD_DV), lambda b, t: (b, 0, 0, 0))
    scratch = []
    if has_s0:
        in_specs += [pl.BlockSpec((None, RB, CH), lambda b, t: (layer, b * nt + t, 0)),
                     pl.BlockSpec((None, nsb, GD_HEADS, GD_DK, GD_DV), lambda b, t: (layer, b, 0, 0, 0))]
        args += [bufp, s0]
    else:
        scratch.append(pltpu.VMEM((8 + RB, CH), F32))
    nseq = N // T
    return pl.pallas_call(
        functools.partial(_gd_kernel, R=R, NS=NS, RB=RB, has_s0=has_s0),
        grid=(nb, nt),
        in_specs=in_specs,
        out_specs=[rowspec(GD_W), st_spec,
                   pl.BlockSpec((nsb, GD_CONV - 1, CH), lambda b, t: (b, 0, 0))],
        out_shape=[jax.ShapeDtypeStruct((N, GD_W), out_dtype),
                   jax.ShapeDtypeStruct((nseq, GD_HEADS, GD_DK, GD_DV), F32),
                   jax.ShapeDtypeStruct((nseq, GD_CONV - 1, CH), F32)],
        scratch_shapes=scratch,
        compiler_params=_params(("arbitrary", "arbitrary")),
        name="gd",
    )(*args)


def _back_kernel(x_ref, ya_ref, yb_ref, yc_ref, yd_ref, wm_ref, wb_ref, wo_ref, lg_ref, lb_ref, o_ref, *, alpha):
    x = x_ref[...]
    gates = jax.nn.sigmoid(_dot(x.astype(BF16), wm_ref[...]))
    merged = None
    for i, yr in enumerate((ya_ref, yb_ref, yc_ref, yd_ref)):
        t = gates[:, i * D_MODEL:(i + 1) * D_MODEL] * _dot(yr[...].astype(BF16), wb_ref[i])
        merged = t if merged is None else merged + t
    z = alpha * x + _dot(merged.astype(BF16), wo_ref[...])
    mu = jnp.mean(z, axis=-1, keepdims=True)
    d = z - mu
    var = jnp.mean(d * d, axis=-1, keepdims=True)
    o_ref[...] = d * lax.rsqrt(var + EPS) * lg_ref[...] + lb_ref[...]


def _back_call(x, ya, yb, yc, yd, wm, wb, wo, lg, lb, *, layer, alpha):
    N = x.shape[0]
    TM = 256
    assert N % TM == 0
    row = lambda wd: pl.BlockSpec((TM, wd), lambda i: (i, 0))
    return pl.pallas_call(
        functools.partial(_back_kernel, alpha=alpha),
        grid=(N // TM,),
        in_specs=[row(D_MODEL), row(BRANCH_W), row(BRANCH_W), row(BRANCH_W), row(BRANCH_W),
                  _layer_spec(wm, layer), _layer_spec(wb, layer), _layer_spec(wo, layer),
                  _const_spec(lg.shape), _const_spec(lb.shape)],
        out_specs=row(D_MODEL),
        out_shape=jax.ShapeDtypeStruct((N, D_MODEL), F32),
        compiler_params=_params(("arbitrary",)),
        name="back",
    )(x, ya, yb, yc, yd, wm, wb, wo, lg, lb)


def _rope_tables(pos, d, reps, rows):
    inv = ROPE_BASE ** (-jnp.arange(0, d, 2, dtype=F32) / d)
    ang = pos.astype(F32)[:, None] * inv[None, :]
    c, s = jnp.cos(ang), jnp.sin(ang)
    cos = jnp.tile(jnp.concatenate([c, c], axis=-1), (rows // pos.shape[0], reps))
    sin = jnp.tile(jnp.concatenate([-s, s], axis=-1), (rows // pos.shape[0], reps))
    return cos, sin


def _prep_weights(w_in, mla_w_uq, mla_w_uk, mla_w_uv, w_branch, w_out):
    o = _OFF
    cols = lambda a, b: w_in[:, :, o[a]:o[b]]
    depth = w_in.shape[0]
    zpad = lambda n: jnp.zeros((depth, D_MODEL, n), w_in.dtype)
    w_ret = cols(0, 4).astype(BF16)
    w_mla = jnp.concatenate([cols(4, 6), cols(7, 8), cols(6, 7), zpad(LANES - MLA_ROPE)], axis=-1).astype(BF16)
    w_sg = cols(8, 11).astype(BF16)
    w_gd = jnp.concatenate([cols(11, 12), cols(14, 15), cols(12, 14), zpad(LANES - 2 * GD_HEADS)], axis=-1).astype(BF16)
    w_merge = cols(15, 16).astype(BF16)
    uq = mla_w_uq.reshape(depth, Q_LORA, MLA_HEADS, MLA_NOPE + MLA_ROPE)
    w_uq = jnp.concatenate([uq[..., :MLA_NOPE].reshape(depth, Q_LORA, -1),
                            uq[..., MLA_NOPE:].reshape(depth, Q_LORA, -1)], axis=-1).astype(BF16)
    w_uk = jnp.transpose(mla_w_uk, (0, 2, 3, 1)).astype(BF16)
    uv = mla_w_uv.reshape(depth, KV_LORA, MLA_HEADS // 2, 2, MLA_V)
    z = jnp.zeros((depth, KV_LORA, MLA_HEADS // 2, MLA_V), mla_w_uv.dtype)
    top = jnp.concatenate([uv[:, :, :, 0], z], axis=-1)
    bot = jnp.concatenate([z, uv[:, :, :, 1]], axis=-1)
    w_uv = jnp.transpose(jnp.concatenate([top, bot], axis=1), (0, 2, 1, 3)).astype(BF16)
    return dict(w_ret=w_ret, w_mla=w_mla, w_sg=w_sg, w_gd=w_gd, w_merge=w_merge, w_uq=w_uq, w_uk=w_uk, w_uv=w_uv,
                w_branch=w_branch.astype(BF16), w_out=w_out.astype(BF16))


def _sg_mix_weights(sg_w, sg_b, C):
    reps = SG_CHUNK // C
    w = jnp.tril(sg_w[:, :, :C, :C])
    if reps > 1:
        eye = jnp.eye(reps, dtype=w.dtype)
        w = jnp.einsum('ab,dgij->dgaibj', eye, w).reshape(w.shape[0], SG_GROUPS, SG_CHUNK, SG_CHUNK)
    b = jnp.tile(sg_b[:, :, :C], (1, 1, reps))
    bs = jnp.broadcast_to(b[..., None], b.shape + (SG_GDIM,))
    return w.astype(BF16), bs.astype(F32)


def _layer_stream(x, l, W, P, *, B, T, tabs, sample):
    cosr, sinr, cosm, sinm = tabs
    act = F32 if sample is not None else BF16
    ya, s_ret = _ret_call(x, W['w_ret'], cosr, sinr, P['ret_gn_g'][l][None],
                          sample['state_ret'] if sample else None, B=B, T=T, layer=l, out_dtype=act)
    mo = _mlap_call(x, W['w_mla'], W['w_uq'], P['mla_q_norm'][l][None],
                    P['mla_kv_norm'][l][None], cosm, sinm, layer=l, act_dtype=act, emit_kt=sample is None)
    qn, qpe, ckv, kpe, gm = mo[:5]
    if sample:
        yb = _attn_decode_call(sample['page_table'], qn, qpe, ckv, kpe, gm, W['w_uk'], W['w_uv'],
                               sample['cache_ckv'], sample['cache_kpet'], B=B, T=T, layer=l)
    else:
        yb = _attn_prefill_call(qn, qpe, mo[5], mo[6], W['w_uk'], gm, W['w_uv'], B=B, T=T, layer=l)
    wt, bs = (P['sg_wt_s'], P['sg_bs_s']) if sample else (P['sg_wt_p'], P['sg_bs_p'])
    sg_out = _sg_call(x, W['w_sg'], P['sg_ln_g'][l][None], P['sg_ln_b'][l][None], wt, bs,
                      layer=l, emit_v=sample is not None, out_dtype=act)
    yc = sg_out[0]
    sgv = sg_out[1] if sample else None
    yd, s_gd, conv_new = _gd_call(x, W['w_gd'], P['gd_conv_w'], P['gd_av'][l], P['gd_dtb'][l],
                                  P['gd_norm_g'][l][None],
                                  sample['bufp'] if sample else None,
                                  sample['state_delta'] if sample else None, B=B, T=T, layer=l, out_dtype=act)
    xn = _back_call(x, ya, yb, yc, yd, W['w_merge'], W['w_branch'], W['w_out'],
                    P['ln_g'][l][None], P['ln_b'][l][None], layer=l, alpha=P['alpha'])
    return xn, ckv, kpe, s_ret, s_gd, conv_new, sgv


def kernel(x_prompt, x_sample, cache_ckv, cache_kpe, state_ret, state_delta, state_conv, page_table,
           w_in, ret_gn_g, mla_q_norm, mla_w_uq, mla_kv_norm, mla_w_uk, mla_w_uv,
           sg_ln_g, sg_ln_b, sg_w, sg_b, gd_conv_w, gd_a_log, gd_dt_bias, gd_norm_g,
           w_branch, w_out, ln_g, ln_b):
    depth = w_in.shape[0]
    bp, tp, _ = x_prompt.shape
    bs_, ts, _ = x_sample.shape
    past_len = page_table.shape[1] * cache_ckv.shape[2]
    W = _prep_weights(w_in, mla_w_uq, mla_w_uk, mla_w_uv, w_branch, w_out)
    pad_small = lambda v: jnp.pad(v, ((0, 0), (0, LANES - v.shape[-1])))[:, None, :]
    P = dict(ret_gn_g=ret_gn_g, mla_q_norm=mla_q_norm, mla_kv_norm=mla_kv_norm, sg_ln_g=sg_ln_g, sg_ln_b=sg_ln_b,
             gd_conv_w=gd_conv_w, gd_norm_g=gd_norm_g, ln_g=ln_g, ln_b=ln_b,
             gd_av=pad_small(-jnp.exp(gd_a_log)), gd_dtb=pad_small(gd_dt_bias),
             alpha=float((2.0 * depth) ** 0.25))
    P['sg_wt_p'], P['sg_bs_p'] = _sg_mix_weights(sg_w, sg_b, min(tp, SG_CHUNK))
    P['sg_wt_s'], P['sg_bs_s'] = _sg_mix_weights(sg_w, sg_b, min(ts, SG_CHUNK))

    pos_p = jnp.arange(tp)
    pos_s = past_len + jnp.arange(ts)
    tab_rows_s = 512
    tabs_p = _rope_tables(pos_p, RET_DK, RET_HEADS, tp) + _rope_tables(pos_p, MLA_ROPE, MLA_HEADS, tp)
    tabs_s = _rope_tables(pos_s, RET_DK, RET_HEADS, tab_rows_s) + _rope_tables(pos_s, MLA_ROPE, MLA_HEADS, tab_rows_s)

    bufp = jnp.pad(state_conv, ((0, 0), (0, 0), (ts - (GD_CONV - 1), 0), (0, 0))).reshape(depth, bs_ * ts, GD_CONV_CH)
    sample = dict(state_ret=state_ret, state_delta=state_delta, bufp=bufp, page_table=page_table,
                  cache_ckv=cache_ckv, cache_kpet=jnp.swapaxes(cache_kpe, 2, 3))

    xp = x_prompt.reshape(bp * tp, D_MODEL)
    xs = x_sample.reshape(bs_ * ts, D_MODEL)
    outs_p, outs_s = [], []
    for l in range(depth):
        xp, *rest = _layer_stream(xp, l, W, P, B=bp, T=tp, tabs=tabs_p, sample=None)
        outs_p.append(rest)
        xs, *rest = _layer_stream(xs, l, W, P, B=bs_, T=ts, tabs=tabs_s, sample=sample)
        outs_s.append(rest)

    def stack(outs, k, shape):
        return jnp.stack([o[k].reshape(shape) for o in outs])

    return (xp.reshape(bp, tp, D_MODEL), xs.reshape(bs_, ts, D_MODEL),
            stack(outs_p, 0, (bp, tp, KV_LORA)), stack(outs_p, 1, (bp, tp, MLA_ROPE)),
            stack(outs_p, 2, (bp, RET_HEADS, RET_DK, RET_DV)), stack(outs_p, 3, (bp, GD_HEADS, GD_DK, GD_DV)),
            stack(outs_p, 4, (bp, GD_CONV - 1, GD_CONV_CH)),
            stack(outs_s, 0, (bs_, ts, KV_LORA)), stack(outs_s, 1, (bs_, ts, MLA_ROPE)),
            stack(outs_s, 2, (bs_, RET_HEADS, RET_DK, RET_DV)), stack(outs_s, 3, (bs_, GD_HEADS, GD_DK, GD_DV)),
            stack(outs_s, 4, (bs_, GD_CONV - 1, GD_CONV_CH)), stack(outs_s, 5, (bs_, ts, SG_W)))
```
